```python
import math
import jax, jax.numpy as jnp
from jax import lax
import numpy as np

D_MODEL = 1024
BATCH = 1
SEQ = 16384
DEPTH = 1
DEC_BATCH = 8
DEC_SEQ = 8192
PAST_LEN = 128

N_HEADS = 8
N_KV_HEADS = 2
HEAD_DIM = 128
GROUP = N_HEADS // N_KV_HEADS
D_ATTN = N_HEADS * HEAD_DIM
D_KV = N_KV_HEADS * HEAD_DIM
Q_BLOCK = 128
ROPE_THETA = 10000.0
GRID_W = 64
D_LRU = D_MODEL
N_LRU_BLOCKS = 8
LRU_BLOCK = D_LRU // N_LRU_BLOCKS
CONV_W = 4
RG_C = 8.0
N_BRANCH = 2
EPS = 1e-6
D_IN_PROJ = D_ATTN + 2 * D_KV + D_ATTN + 2 * D_LRU + N_BRANCH * D_MODEL

kernel_name = "hybrid_gqa_rglru_gated_merge_encoder"


def rmsnorm(x, g):
    xf = x.astype(jnp.float32)
    y = xf * lax.rsqrt(jnp.mean(xf * xf, axis=-1, keepdims=True) + EPS)
    return (y * g.astype(jnp.float32)).astype(x.dtype)


def axial_rope(x):
    b, s, h, d = x.shape
    rows_n = s // GRID_W
    rows = jnp.repeat(jnp.arange(rows_n, dtype=jnp.float32), GRID_W)
    cols = jnp.tile(jnp.arange(GRID_W, dtype=jnp.float32), rows_n)
    n_pair_axis = d // 4
    inv_freq = ROPE_THETA ** (-jnp.arange(n_pair_axis, dtype=jnp.float32) / n_pair_axis)
    ang = jnp.concatenate([rows[:, None] * inv_freq, cols[:, None] * inv_freq], axis=-1)
    cos = jnp.cos(ang)[None, :, None, :]
    sin = jnp.sin(ang)[None, :, None, :]
    xp = x.astype(jnp.float32).reshape(b, s, h, d // 2, 2)
    x0, x1 = xp[..., 0], xp[..., 1]
    out = jnp.stack([x0 * cos - x1 * sin, x0 * sin + x1 * cos], axis=-1)
    return out.reshape(b, s, h, d).astype(x.dtype)


def block_attention(q, k, v):
    b, s, nkv, g, d = q.shape
    nb = s // Q_BLOCK
    scale = 1.0 / math.sqrt(d)
    qb = q.reshape(b, nb, Q_BLOCK, nkv, g, d).transpose(1, 0, 2, 3, 4, 5)

    def one_block(qblk):
        sc = jnp.einsum('bqkgd,bskd->bkgqs', qblk, k).astype(jnp.float32) * scale
        p = jax.nn.softmax(sc, axis=-1).astype(v.dtype)
        return jnp.einsum('bkgqs,bskd->bqkgd', p, v)

    o = lax.map(one_block, qb)
    return o.transpose(1, 0, 2, 3, 4, 5).reshape(b, s, nkv * g * d)


def centred_dwconv(x, w, bias):
    s = x.shape[1]
    left = CONV_W // 2
    xp = jnp.pad(x, ((0, 0), (left, CONV_W - 1 - left), (0, 0)))
    y = xp[:, 0:s] * w[0]
    for j in range(1, CONV_W):
        y = y + xp[:, j:j + s] * w[j]
    return y + bias


def _lin_combine(c1, c2):
    a1, b1 = c1
    a2, b2 = c2
    return a1 * a2, a2 * b1 + b2


def rg_lru(xc, w_r, b_r, w_i, b_i, lam, reverse):
    b, s, c = xc.shape
    xb = xc.reshape(b, s, N_LRU_BLOCKS, LRU_BLOCK)
    r = jax.nn.sigmoid(jnp.einsum('bsnc,ncd->bsnd', xb, w_r).reshape(b, s, c) + b_r)
    i = jax.nn.sigmoid(jnp.einsum('bsnc,ncd->bsnd', xb, w_i).reshape(b, s, c) + b_i)
    log_a = -RG_C * r.astype(jnp.float32) * jax.nn.softplus(-lam.astype(jnp.float32))
    a = jnp.exp(log_a)
    u = jnp.sqrt(-jnp.expm1(2.0 * log_a)) * (i * xc).astype(jnp.float32)
    _, h = lax.associative_scan(_lin_combine, (a, u), axis=1, reverse=reverse)
    return h


def layer(x, norm_in, w_in, b_merge, q_norm, k_norm, conv_w, conv_b,
          w_rgate, b_rgate, w_igate, b_igate, lam, w_branch, w_out):
    b, s, _ = x.shape
    h = rmsnorm(x, norm_in)
    z = jnp.einsum('bsd,de->bse', h, w_in)
    splits = np.cumsum([D_ATTN, D_KV, D_KV, D_ATTN, D_LRU, D_LRU])
    q, k, v, g_attn, x_lru, g_lru, m_logit = jnp.split(z, splits, axis=-1)

    q = axial_rope(rmsnorm(q.reshape(b, s, N_HEADS, HEAD_DIM), q_norm))
    k = axial_rope(rmsnorm(k.reshape(b, s, N_KV_HEADS, HEAD_DIM), k_norm))
    v = v.reshape(b, s, N_KV_HEADS, HEAD_DIM)
    q = q.reshape(b, s, N_KV_HEADS, GROUP, HEAD_DIM)
    a_out = block_attention(q, k, v) * jax.nn.silu(g_attn)

    xc = centred_dwconv(x_lru, conv_w, conv_b)
    h_f = rg_lru(xc, w_rgate[0], b_rgate[0], w_igate[0], b_igate[0], lam[0], False)
    h_b = rg_lru(xc, w_rgate[1], b_rgate[1], w_igate[1], b_igate[1], lam[1], True)
    l_out = (h_f + h_b).astype(x.dtype) * jax.nn.silu(g_lru)

    a_proj = jnp.einsum('bsc,cd->bsd', a_out, w_branch[0])
    l_proj = jnp.einsum('bsc,cd->bsd', l_out, w_branch[1])
    gates = jax.nn.sigmoid(m_logit + b_merge)
    g_a, g_l = jnp.split(gates, 2, axis=-1)
    merged = g_a * a_proj + g_l * l_proj
    return x + jnp.einsum('bsd,de->bse', merged, w_out)


def trunk(x, norm_in, w_in, b_merge, q_norm, k_norm, conv_w, conv_b,
          w_rgate, b_rgate, w_igate, b_igate, lam, w_branch, w_out, norm_final):
    for l in range(DEPTH):
        x = layer(x, norm_in[l], w_in[l], b_merge[l], q_norm[l], k_norm[l], conv_w[l], conv_b[l],
                  w_rgate[l], b_rgate[l], w_igate[l], b_igate[l], lam[l], w_branch[l], w_out[l])
    return rmsnorm(x, norm_final)


def setup_inputs(seed: int = 0) -> dict:
    key = jax.random.key(seed)
    ks = jax.random.split(key, 20)
    f32 = jnp.float32
    nrm = lambda k, shape, sc: jax.random.normal(k, shape, f32) * sc
    a_c = jax.random.uniform(ks[12], (DEPTH, 2, D_LRU), f32, 0.9, 0.999)
    a_base = a_c ** (1.0 / RG_C)
    lam = jnp.log(a_base) - jnp.log1p(-a_base)
    return {
        "x_prompt": nrm(ks[0], (BATCH, SEQ, D_MODEL), 1.0),
        "x_sample": nrm(ks[1], (DEC_BATCH, DEC_SEQ, D_MODEL), 1.0),
        "norm_in": 1.0 + nrm(ks[2], (DEPTH, D_MODEL), 0.02),
        "w_in": nrm(ks[3], (DEPTH, D_MODEL, D_IN_PROJ), D_MODEL ** -0.5),
        "b_merge": nrm(ks[4], (DEPTH, N_BRANCH * D_MODEL), 0.1),
        "q_norm": 1.0 + nrm(ks[5], (DEPTH, HEAD_DIM), 0.02),
        "k_norm": 1.0 + nrm(ks[6], (DEPTH, HEAD_DIM), 0.02),
        "conv_w": nrm(ks[7], (DEPTH, CONV_W, D_LRU), CONV_W ** -0.5),
        "conv_b": nrm(ks[8], (DEPTH, D_LRU), 0.02),
        "w_rgate": nrm(ks[9], (DEPTH, 2, N_LRU_BLOCKS, LRU_BLOCK, LRU_BLOCK), LRU_BLOCK ** -0.5),
        "b_rgate": nrm(ks[10], (DEPTH, 2, D_LRU), 0.1),
        "w_igate": nrm(ks[11], (DEPTH, 2, N_LRU_BLOCKS, LRU_BLOCK, LRU_BLOCK), LRU_BLOCK ** -0.5),
        "b_igate": nrm(ks[13], (DEPTH, 2, D_LRU), 0.1),
        "lam": lam,
        "w_branch": nrm(ks[14], (DEPTH, N_BRANCH, D_ATTN, D_MODEL), D_ATTN ** -0.5),
        "w_out": nrm(ks[15], (DEPTH, D_MODEL, D_MODEL), D_MODEL ** -0.5),
        "norm_final": 1.0 + nrm(ks[16], (D_MODEL,), 0.02),
    }


def reference(x_prompt, x_sample, norm_in, w_in, b_merge, q_norm, k_norm, conv_w, conv_b,
              w_rgate, b_rgate, w_igate, b_igate, lam, w_branch, w_out, norm_final):
    y_prompt = trunk(x_prompt, norm_in, w_in, b_merge, q_norm, k_norm, conv_w, conv_b,
                     w_rgate, b_rgate, w_igate, b_igate, lam, w_branch, w_out, norm_final)
    y_sample = trunk(x_sample, norm_in, w_in, b_merge, q_norm, k_norm, conv_w, conv_b,
                     w_rgate, b_rgate, w_igate, b_igate, lam, w_branch, w_out, norm_final)
    return (y_prompt, y_sample)
```

```python
import functools
import math

import jax
import jax.numpy as jnp
from jax import lax
from jax.experimental import pallas as pl
from jax.experimental.pallas import tpu as pltpu

N_HEADS = 8
N_KV_HEADS = 2
GROUP = N_HEADS // N_KV_HEADS
HEAD_DIM = 128
HALF = HEAD_DIM // 2
N_LRU_BLOCKS = 8
LRU_BLOCK = 128
CONV_W = 4
RG_C = 8.0
EPS = 1e-6
GRID_W = 64
ROPE_THETA = 10000.0

SUBLANES = 8
HALO = SUBLANES
V7X_VMEM_LIMIT = 56 * 1024 * 1024

QKV_ROWS = 512
ATTN_Q = 256
ATTN_K = 512
LRU_ROWS = 256

F32 = jnp.float32
BF16 = jnp.bfloat16


def _const_spec(shape):
    nd = len(shape)
    return pl.BlockSpec(shape, lambda *_: (0,) * nd, pipeline_mode=pl.Buffered(1))


def _params(n_axes, sequential_last):
    sem = ["parallel"] * n_axes
    if sequential_last:
        sem[-1] = "arbitrary"
    return pltpu.CompilerParams(dimension_semantics=tuple(sem), vmem_limit_bytes=V7X_VMEM_LIMIT)


def _rms_scale(x):
    return lax.rsqrt(jnp.mean(x * x, axis=-1, keepdims=True) + EPS)


def _qkv_kernel(x_ref, g_ref, w_ref, cq_ref, sq_ref, ck_ref, sk_ref, qT_ref, k_ref, vT_ref):
    x = x_ref[0]
    h = (x * _rms_scale(x) * g_ref[...]).astype(BF16)
    zT = lax.dot_general(w_ref[...], h, (((1,), (1,)), ((), ())), preferred_element_type=F32)

    def norm_rope(z, c, s):
        n = z * lax.rsqrt(jnp.mean(z * z, axis=0, keepdims=True) + EPS)
        swapped = jnp.concatenate([n[HALF:], n[:HALF]], axis=0)
        return n * c + swapped * s

    cq, sq = cq_ref[...], sq_ref[...]
    for hd in range(N_HEADS):
        rows = slice(hd * HEAD_DIM, (hd + 1) * HEAD_DIM)
        qT_ref[0, rows, :] = norm_rope(zT[rows], cq, sq).astype(BF16)
    ck, sk = ck_ref[...], sk_ref[...]
    k_off = N_HEADS * HEAD_DIM
    for kv in range(N_KV_HEADS):
        rows = slice(k_off + kv * HEAD_DIM, k_off + (kv + 1) * HEAD_DIM)
        kT = norm_rope(zT[rows], ck, sk)
        k_ref[0, :, kv * HEAD_DIM:(kv + 1) * HEAD_DIM] = kT.T.astype(BF16)
    v_off = k_off + N_KV_HEADS * HEAD_DIM
    vT_ref[0] = zT[v_off:v_off + N_KV_HEADS * HEAD_DIM].astype(BF16)


def _qkv_call(x, g_in, w_qkvT, cq, sq, ck, sk):
    b, s, d = x.shape
    t = min(QKV_ROWS, s)
    n_q, n_kv = N_HEADS * HEAD_DIM, N_KV_HEADS * HEAD_DIM
    tab = pl.BlockSpec((HEAD_DIM, t), lambda bi, i: (0, i))
    return pl.pallas_call(
        _qkv_kernel,
        grid=(b, s // t),
        in_specs=[
            pl.BlockSpec((1, t, d), lambda bi, i: (bi, i, 0)),
            _const_spec((1, d)),
            _const_spec(w_qkvT.shape),
            tab, tab, tab, tab,
        ],
        out_specs=[
            pl.BlockSpec((1, n_q, t), lambda bi, i: (bi, 0, i)),
            pl.BlockSpec((1, t, n_kv), lambda bi, i: (bi, i, 0)),
            pl.BlockSpec((1, n_kv, t), lambda bi, i: (bi, 0, i)),
        ],
        out_shape=[
            jax.ShapeDtypeStruct((b, n_q, s), BF16),
            jax.ShapeDtypeStruct((b, s, n_kv), BF16),
            jax.ShapeDtypeStruct((b, n_kv, s), BF16),
        ],
        compiler_params=_params(2, False),
        name="qkv",
    )(x, g_in, w_qkvT, cq, sq, ck, sk)


def _attn_kernel(qT_ref, k_ref, vT_ref, o_ref, m_ref, l_ref, acc_ref, *, n_chunks, tk):
    m_ref[...] = jnp.full(m_ref.shape, -jnp.inf, F32)
    l_ref[...] = jnp.zeros(l_ref.shape, F32)
    acc_ref[...] = jnp.zeros(acc_ref.shape, F32)

    for kv in range(N_KV_HEADS):
        cols = slice(kv * HEAD_DIM, (kv + 1) * HEAD_DIM)

        def chunk(c, carry, cols=cols, kv=kv):
            off = pl.multiple_of(c * tk, tk)
            kc = k_ref[0, pl.ds(off, tk), cols]
            vc = vT_ref[0, cols, pl.ds(off, tk)]
            for j in range(GROUP):
                hd = kv * GROUP + j
                qT = qT_ref[0, hd * HEAD_DIM:(hd + 1) * HEAD_DIM, :]
                s = jnp.dot(kc, qT, preferred_element_type=F32)
                m_old = m_ref[hd:hd + 1, :]
                m_new = jnp.maximum(m_old, jnp.max(s, axis=0, keepdims=True))
                alpha = jnp.exp2(m_old - m_new)
                p = jnp.exp2(s - m_new)
                l_ref[hd:hd + 1, :] = alpha * l_ref[hd:hd + 1, :] + jnp.sum(p, axis=0, keepdims=True)
                pv = jnp.dot(vc, p.astype(BF16), preferred_element_type=F32)
                acc_ref[hd] = alpha * acc_ref[hd] + pv
                m_ref[hd:hd + 1, :] = m_new
            return carry

        lax.fori_loop(0, n_chunks, chunk, 0)

    for hd in range(N_HEADS):
        o = acc_ref[hd] * (1.0 / l_ref[hd:hd + 1, :])
        o_ref[0, :, hd * HEAD_DIM:(hd + 1) * HEAD_DIM] = o.T.astype(BF16)


def _attn_call(qT, k, vT):
    b, n_q, s = qT.shape
    n_kv = k.shape[2]
    tq = min(ATTN_Q, s)
    tk = min(ATTN_K, s)
    kernel = functools.partial(_attn_kernel, n_chunks=s // tk, tk=tk)
    return pl.pallas_call(
        kernel,
        grid=(b, s // tq),
        in_specs=[
            pl.BlockSpec((1, n_q, tq), lambda bi, i: (bi, 0, i)),
            pl.BlockSpec((1, s, n_kv), lambda bi, i: (bi, 0, 0), pipeline_mode=pl.Buffered(1)),
            pl.BlockSpec((1, n_kv, s), lambda bi, i: (bi, 0, 0), pipeline_mode=pl.Buffered(1)),
        ],
        out_specs=pl.BlockSpec((1, tq, n_q), lambda bi, i: (bi, i, 0)),
        out_shape=jax.ShapeDtypeStruct((b, s, n_q), BF16),
        scratch_shapes=[
            pltpu.VMEM((N_HEADS, tq), F32),
            pltpu.VMEM((N_HEADS, tq), F32),
            pltpu.VMEM((N_HEADS, HEAD_DIM, tq), F32),
        ],
        compiler_params=_params(2, True),
        name="attn",
    )(qT, k, vT)


def _lru_sweep(x_ref, xp_ref, xn_ref, g_ref, wxl_ref, cw_ref, cb_ref, wg_ref, bg_ref, cneg_ref,
               xl_scr, a_scr, u_scr, carry_ref, write_row, *, reverse, t):
    i = pl.program_id(1)
    nb = pl.num_programs(1)
    blk = nb - 1 - i if reverse else i
    d_lru = wxl_ref.shape[1]
    groups = t // SUBLANES

    @pl.when(i == 0)
    def _():
        carry_ref[...] = jnp.zeros(carry_ref.shape, F32)

    xp = xp_ref[0] * jnp.where(blk > 0, 1.0, 0.0)
    xn = xn_ref[0] * jnp.where(blk < nb - 1, 1.0, 0.0)
    xe = jnp.concatenate([xp, x_ref[0], xn], axis=0)
    he = (xe * _rms_scale(xe) * g_ref[...]).astype(BF16)
    xl_scr[...] = jnp.dot(he, wxl_ref[...], preferred_element_type=F32)

    left = CONV_W // 2
    xc = cb_ref[...] + cw_ref[0:1, :] * xl_scr[pl.ds(HALO - left, t), :]
    for j in range(1, CONV_W):
        xc = xc + cw_ref[j:j + 1, :] * xl_scr[pl.ds(HALO - left + j, t), :]

    sub = lax.broadcasted_iota(jnp.int32, (1, SUBLANES, LRU_BLOCK), 1)
    for n in range(N_LRU_BLOCKS):
        cols = slice(n * LRU_BLOCK, (n + 1) * LRU_BLOCK)
        xcn = xc[:, cols]
        gz = jnp.dot(xcn.astype(BF16), wg_ref[n], preferred_element_type=F32) + bg_ref[n]
        r = jax.nn.sigmoid(gz[:, :LRU_BLOCK])
        ig = jax.nn.sigmoid(gz[:, LRU_BLOCK:])
        log_a = cneg_ref[:, cols] * r
        a = jnp.exp(log_a)
        u = jnp.sqrt((1.0 - a) * (1.0 + a)) * (ig * xcn)
        a3 = a.reshape(groups, SUBLANES, LRU_BLOCK)
        u3 = u.reshape(groups, SUBLANES, LRU_BLOCK)
        for dist in (1, 2, 4):
            if reverse:
                shift, ok = SUBLANES - dist, sub < SUBLANES - dist
            else:
                shift, ok = dist, sub >= dist
            a_nb = pltpu.roll(a3, shift, axis=1)
            u_nb = pltpu.roll(u3, shift, axis=1)
            u3 = jnp.where(ok, a3 * u_nb + u3, u3)
            a3 = jnp.where(ok, a3 * a_nb, a3)
        a_scr[:, :, cols] = a3
        u_scr[:, :, cols] = u3

    edge = 0 if reverse else SUBLANES - 1

    def step(j, hrow):
        jj = groups - 1 - j if reverse else j
        rows = u_scr[jj] + a_scr[jj] * hrow
        write_row(jj, rows)
        return jnp.broadcast_to(rows[edge:edge + 1, :], (SUBLANES, d_lru))

    carry_ref[...] = lax.fori_loop(0, groups, step, carry_ref[...], unroll=4)
    return he[HALO:HALO + t]


def _lru_f_kernel(x_ref, xp_ref, xn_ref, g_ref, wxl_ref, cw_ref, cb_ref, wg_ref, bg_ref, cneg_ref,
                  hf_ref, xl_scr, a_scr, u_scr, carry_ref, *, t):
    def write_row(j, rows):
        hf_ref[0, j] = rows

    _lru_sweep(x_ref, xp_ref, xn_ref, g_ref, wxl_ref, cw_ref, cb_ref, wg_ref, bg_ref, cneg_ref,
               xl_scr, a_scr, u_scr, carry_ref, write_row, reverse=False, t=t)


def _merge_kernel(x_ref, xp_ref, xn_ref, g_ref, wxl_ref, cw_ref, cb_ref, wg_ref, bg_ref, cneg_ref,
                  hf_ref, att_ref, wgl_ref, wga_ref, wm_ref, bm_ref, wb0_ref, wb1_ref, wo_ref, gfin_ref,
                  y_ref, xl_scr, a_scr, u_scr, carry_ref, hb_scr, *, t, final_norm):
    def write_row(j, rows):
        hb_scr[j] = rows

    h = _lru_sweep(x_ref, xp_ref, xn_ref, g_ref, wxl_ref, cw_ref, cb_ref, wg_ref, bg_ref, cneg_ref,
                   xl_scr, a_scr, u_scr, carry_ref, write_row, reverse=True, t=t)
    d = x_ref.shape[2]
    lsum = (hf_ref[0] + hb_scr[...]).reshape(t, wxl_ref.shape[1])
    g_l = jnp.dot(h, wgl_ref[...], preferred_element_type=F32)
    l_out = (lsum * (g_l * jax.nn.sigmoid(g_l))).astype(BF16)
    g_a = jnp.dot(h, wga_ref[...], preferred_element_type=F32)
    a_out = (att_ref[0].astype(F32) * (g_a * jax.nn.sigmoid(g_a))).astype(BF16)
    a_proj = jnp.dot(a_out, wb0_ref[...], preferred_element_type=F32)
    l_proj = jnp.dot(l_out, wb1_ref[...], preferred_element_type=F32)
    gates = jax.nn.sigmoid(jnp.dot(h, wm_ref[...], preferred_element_type=F32) + bm_ref[...])
    merged = gates[:, :d] * a_proj + gates[:, d:] * l_proj
    y = x_ref[0] + jnp.dot(merged.astype(BF16), wo_ref[...], preferred_element_type=F32)
    if final_norm:
        y = y * _rms_scale(y) * gfin_ref[...]
    y_ref[0] = y


def _lru_specs(s, t, d, d_lru, reverse):
    nb = s // t
    groups = t // SUBLANES
    last_group = s // SUBLANES - 1

    def blk(i):
        return nb - 1 - i if reverse else i

    x_spec = pl.BlockSpec((1, t, d), lambda bi, i: (bi, blk(i), 0))
    xp_spec = pl.BlockSpec((1, HALO, d), lambda bi, i: (bi, jnp.maximum(blk(i) * groups - 1, 0), 0))
    xn_spec = pl.BlockSpec((1, HALO, d), lambda bi, i: (bi, jnp.minimum((blk(i) + 1) * groups, last_group), 0))
    hf_spec = pl.BlockSpec((1, groups, SUBLANES, d_lru), lambda bi, i: (bi, blk(i), 0, 0))
    scratch = [
        pltpu.VMEM((t + 2 * HALO, d_lru), F32),
        pltpu.VMEM((groups, SUBLANES, d_lru), F32),
        pltpu.VMEM((groups, SUBLANES, d_lru), F32),
        pltpu.VMEM((SUBLANES, d_lru), F32),
    ]
    return x_spec, xp_spec, xn_spec, hf_spec, scratch, blk


def _lru_f_call(x, g_in, w_xl, conv_w, conv_b, w_gate, b_gate, cneg):
    b, s, d = x.shape
    d_lru = w_xl.shape[1]
    t = min(LRU_ROWS, s)
    x_spec, xp_spec, xn_spec, hf_spec, scratch, _ = _lru_specs(s, t, d, d_lru, False)
    return pl.pallas_call(
        functools.partial(_lru_f_kernel, t=t),
        grid=(b, s // t),
        in_specs=[x_spec, xp_spec, xn_spec, _const_spec((1, d)), _const_spec(w_xl.shape),
                  _const_spec(conv_w.shape), _const_spec(conv_b.shape), _const_spec(w_gate.shape),
                  _const_spec(b_gate.shape), _const_spec(cneg.shape)],
        out_specs=hf_spec,
        out_shape=jax.ShapeDtypeStruct((b, s // SUBLANES, SUBLANES, d_lru), F32),
        scratch_shapes=scratch,
        compiler_params=_params(2, True),
        name="lru_f",
    )(x, x, x, g_in, w_xl, conv_w, conv_b, w_gate, b_gate, cneg)


def _merge_call(x, g_in, w_xl, conv_w, conv_b, w_gate, b_gate, cneg, hf, att,
                w_gl, w_ga, w_m, b_m, w_b0, w_b1, w_o, g_fin, final_norm):
    b, s, d = x.shape
    d_lru = w_xl.shape[1]
    t = min(LRU_ROWS, s)
    x_spec, xp_spec, xn_spec, hf_spec, scratch, blk = _lru_specs(s, t, d, d_lru, True)
    tok = lambda width: pl.BlockSpec((1, t, width), lambda bi, i: (bi, blk(i), 0))
    consts = [g_in, w_xl, conv_w, conv_b, w_gate, b_gate, cneg]
    tail = [w_gl, w_ga, w_m, b_m, w_b0, w_b1, w_o, g_fin]
    return pl.pallas_call(
        functools.partial(_merge_kernel, t=t, final_norm=final_norm),
        grid=(b, s // t),
        in_specs=[x_spec, xp_spec, xn_spec] + [_const_spec(c.shape) for c in consts]
        + [hf_spec, tok(att.shape[2])] + [_const_spec(c.shape) for c in tail],
        out_specs=tok(d),
        out_shape=jax.ShapeDtypeStruct((b, s, d), F32),
        scratch_shapes=scratch + [pltpu.VMEM((t // SUBLANES, SUBLANES, d_lru), F32)],
        compiler_params=_params(2, True),
        name="merge",
    )(x, x, x, *consts, hf, att, *tail)


def _rope_tables(s, gain, scale):
    rows_n = s // GRID_W
    rows = jnp.repeat(jnp.arange(rows_n, dtype=F32), GRID_W)
    cols = jnp.tile(jnp.arange(GRID_W, dtype=F32), rows_n)
    n_pair_axis = HEAD_DIM // 4
    inv_freq = ROPE_THETA ** (-jnp.arange(n_pair_axis, dtype=F32) / n_pair_axis)
    ang = jnp.concatenate([rows[:, None] * inv_freq, cols[:, None] * inv_freq], axis=-1)
    cos = jnp.cos(ang).T
    sin = jnp.sin(ang).T
    c = jnp.concatenate([cos, cos], axis=0)
    sn = jnp.concatenate([-sin, sin], axis=0)
    gain = gain.astype(F32)
    return (gain[:, None] * c) * scale, (jnp.roll(gain, HALF)[:, None] * sn) * scale


def _layer_params(w_in, q_norm, k_norm, w_rgate, b_rgate, w_igate, b_igate, lam, w_branch, w_out, d):
    d_attn = N_HEADS * HEAD_DIM
    d_kv = N_KV_HEADS * HEAD_DIM
    d_lru = N_LRU_BLOCKS * LRU_BLOCK
    splits = [d_attn, d_kv, d_kv, d_attn, d_lru, d_lru]
    offs = [0]
    for w in splits:
        offs.append(offs[-1] + w)
    w_q, w_k, w_v, w_ga, w_xl, w_gl = (w_in[:, offs[j]:offs[j + 1]] for j in range(6))
    w_m = w_in[:, offs[6]:]
    perm = jnp.concatenate([jnp.arange(0, HEAD_DIM, 2), jnp.arange(1, HEAD_DIM, 2)])
    permute = lambda w, nh: w.reshape(d, nh, HEAD_DIM)[:, :, perm].reshape(d, nh * HEAD_DIM)
    w_qkvT = jnp.concatenate([permute(w_q, N_HEADS), permute(w_k, N_KV_HEADS), w_v], axis=1).T.astype(BF16)
    w_gate = jnp.concatenate([w_rgate, w_igate], axis=-1).astype(BF16)
    b_gate = jnp.concatenate([b_rgate.reshape(2, N_LRU_BLOCKS, 1, LRU_BLOCK),
                              b_igate.reshape(2, N_LRU_BLOCKS, 1, LRU_BLOCK)], axis=-1).astype(F32)
    cneg = (-RG_C * jax.nn.softplus(-lam.astype(F32))).reshape(2, 1, d_lru)
    return dict(
        w_qkvT=w_qkvT, gq=q_norm[perm], gk=k_norm[perm],
        w_ga=w_ga.astype(BF16), w_xl=w_xl.astype(BF16), w_gl=w_gl.astype(BF16), w_m=w_m.astype(BF16),
        w_gate=w_gate, b_gate=b_gate, cneg=cneg,
        w_b0=w_branch[0].astype(BF16), w_b1=w_branch[1].astype(BF16), w_o=w_out.astype(BF16),
    )


def _layer(x, p, g_in, conv_w, conv_b, b_m, g_fin, final_norm):
    s = x.shape[1]
    q_scale = math.log2(math.e) / math.sqrt(HEAD_DIM)
    cq, sq = _rope_tables(s, p["gq"], q_scale)
    ck, sk = _rope_tables(s, p["gk"], 1.0)
    qT, k, vT = _qkv_call(x, g_in, p["w_qkvT"], cq, sq, ck, sk)
    att = _attn_call(qT, k, vT)
    lru = (g_in, p["w_xl"], conv_w, conv_b)
    hf = _lru_f_call(x, *lru, p["w_gate"][0], p["b_gate"][0], p["cneg"][0])
    return _merge_call(x, *lru, p["w_gate"][1], p["b_gate"][1], p["cneg"][1], hf, att,
                       p["w_gl"], p["w_ga"], p["w_m"], b_m, p["w_b0"], p["w_b1"], p["w_o"], g_fin, final_norm)


def kernel(x_prompt, x_sample, norm_in, w_in, b_merge, q_norm, k_norm, conv_w, conv_b, w_rgate, b_rgate,
           w_igate, b_igate, lam, w_branch, w_out, norm_final):
    depth, d = norm_in.shape
    layers = [
        _layer_params(w_in[l], q_norm[l], k_norm[l], w_rgate[l], b_rgate[l], w_igate[l], b_igate[l], lam[l],
                      w_branch[l], w_out[l], d)
        for l in range(depth)
    ]
    g_fin = norm_final.reshape(1, d).astype(F32)
    outs = []
    for x in (x_prompt, x_sample):
        for l in range(depth):
            x = _layer(x, layers[l], norm_in[l].reshape(1, d).astype(F32), conv_w[l].astype(F32),
                       conv_b[l].reshape(1, -1).astype(F32), b_merge[l].reshape(1, -1).astype(F32), g_fin,
                       l == depth - 1)
        outs.append(x)
    return tuple(outs)
```

```python
import functools
import math

import jax
import jax.numpy as jnp
from jax import lax
from jax.experimental import pallas as pl
from jax.experimental.pallas import tpu as pltpu

N_HEADS = 8
N_KV_HEADS = 2
GROUP = N_HEADS // N_KV_HEADS
HEAD_DIM = 128
HALF = HEAD_DIM // 2
N_LRU_BLOCKS = 8
LRU_BLOCK = 128
CONV_W = 4
RG_C = 8.0
EPS = 1e-6
GRID_W = 64
ROPE_THETA = 10000.0

SUBLANES = 8
HALO = SUBLANES
V7X_VMEM_LIMIT = 56 * 1024 * 1024

QKV_ROWS = 512
ATTN_Q = 256
ATTN_K = 512
LRU_ROWS = 256

F32 = jnp.float32
BF16 = jnp.bfloat16


def _const_spec(shape):
    nd = len(shape)
    return pl.BlockSpec(shape, lambda *_: (0,) * nd, pipeline_mode=pl.Buffered(1))


def _params(n_axes, sequential_last):
    sem = ["parallel"] * n_axes
    if sequential_last:
        sem[-1] = "arbitrary"
    return pltpu.CompilerParams(dimension_semantics=tuple(sem), vmem_limit_bytes=V7X_VMEM_LIMIT)


def _rms_scale(x):
    return lax.rsqrt(jnp.mean(x * x, axis=-1, keepdims=True) + EPS)


def _qkv_kernel(x_ref, g_ref, w_ref, cq_ref, sq_ref, ck_ref, sk_ref, qT_ref, k_ref, vT_ref):
    x = x_ref[0]
    h = (x * _rms_scale(x) * g_ref[...]).astype(BF16)
    zT = lax.dot_general(w_ref[...], h, (((1,), (1,)), ((), ())), preferred_element_type=F32)

    def norm_rope(z, c, s):
        n = z * lax.rsqrt(jnp.mean(z * z, axis=0, keepdims=True) + EPS)
        swapped = jnp.concatenate([n[HALF:], n[:HALF]], axis=0)
        return n * c + swapped * s

    cq, sq = cq_ref[...], sq_ref[...]
    for hd in range(N_HEADS):
        rows = slice(hd * HEAD_DIM, (hd + 1) * HEAD_DIM)
        qT_ref[0, rows, :] = norm_rope(zT[rows], cq, sq).astype(BF16)
    ck, sk = ck_ref[...], sk_ref[...]
    k_off = N_HEADS * HEAD_DIM
    for kv in range(N_KV_HEADS):
        rows = slice(k_off + kv * HEAD_DIM, k_off + (kv + 1) * HEAD_DIM)
        kT = norm_rope(zT[rows], ck, sk)
        k_ref[0, :, kv * HEAD_DIM:(kv + 1) * HEAD_DIM] = kT.T.astype(BF16)
    v_off = k_off + N_KV_HEADS * HEAD_DIM
    vT_ref[0] = zT[v_off:v_off + N_KV_HEADS * HEAD_DIM].astype(BF16)


def _qkv_call(x, g_in, w_qkvT, cq, sq, ck, sk):
    b, s, d = x.shape
    t = min(QKV_ROWS, s)
    n_q, n_kv = N_HEADS * HEAD_DIM, N_KV_HEADS * HEAD_DIM
    tab = pl.BlockSpec((HEAD_DIM, t), lambda bi, i: (0, i))
    return pl.pallas_call(
        _qkv_kernel,
        grid=(b, s // t),
        in_specs=[
            pl.BlockSpec((1, t, d), lambda bi, i: (bi, i, 0)),
            _const_spec((1, d)),
            _const_spec(w_qkvT.shape),
            tab, tab, tab, tab,
        ],
        out_specs=[
            pl.BlockSpec((1, n_q, t), lambda bi, i: (bi, 0, i)),
            pl.BlockSpec((1, t, n_kv), lambda bi, i: (bi, i, 0)),
            pl.BlockSpec((1, n_kv, t), lambda bi, i: (bi, 0, i)),
        ],
        out_shape=[
            jax.ShapeDtypeStruct((b, n_q, s), BF16),
            jax.ShapeDtypeStruct((b, s, n_kv), BF16),
            jax.ShapeDtypeStruct((b, n_kv, s), BF16),
        ],
        compiler_params=_params(2, False),
        name="qkv",
    )(x, g_in, w_qkvT, cq, sq, ck, sk)


def _attn_kernel(qT_ref, k_ref, vT_ref, o_ref, m_ref, l_ref, alpha_ref, acc_ref, s_scr, p_scr, *, n_chunks, tk):
    m_ref[...] = jnp.full(m_ref.shape, -jnp.inf, F32)
    l_ref[...] = jnp.zeros(l_ref.shape, F32)
    acc_ref[...] = jnp.zeros(acc_ref.shape, F32)

    for kv in range(N_KV_HEADS):
        cols = slice(kv * HEAD_DIM, (kv + 1) * HEAD_DIM)

        def scores(c, j, cols=cols, kv=kv):
            hd = kv * GROUP + j
            off = pl.multiple_of(c * tk, tk)
            kc = k_ref[0, pl.ds(off, tk), cols]
            qT = qT_ref[0, hd * HEAD_DIM:(hd + 1) * HEAD_DIM, :]
            s_scr[j % 2] = jnp.dot(kc, qT, preferred_element_type=F32)

        def softmax(j, kv=kv):
            hd = kv * GROUP + j
            s = s_scr[j % 2]
            m_old = m_ref[hd:hd + 1, :]
            m_new = jnp.maximum(m_old, jnp.max(s, axis=0, keepdims=True))
            alpha = jnp.exp2(m_old - m_new)
            p = jnp.exp2(s - m_new)
            l_ref[hd:hd + 1, :] = alpha * l_ref[hd:hd + 1, :] + jnp.sum(p, axis=0, keepdims=True)
            m_ref[hd:hd + 1, :] = m_new
            alpha_ref[hd:hd + 1, :] = alpha
            p_scr[j % 2] = p.astype(BF16)

        def update(c, j, cols=cols, kv=kv):
            hd = kv * GROUP + j
            off = pl.multiple_of(c * tk, tk)
            vc = vT_ref[0, cols, pl.ds(off, tk)]
            pv = jnp.dot(vc, p_scr[j % 2], preferred_element_type=F32)
            acc_ref[hd] = alpha_ref[hd:hd + 1, :] * acc_ref[hd] + pv

        last = GROUP - 1
        p_scr[last % 2] = jnp.zeros(p_scr.shape[1:], BF16)
        alpha_ref[kv * GROUP + last:kv * GROUP + last + 1, :] = jnp.ones((1, alpha_ref.shape[1]), F32)
        scores(0, 0)

        def step(c, carry, scores=scores, softmax=softmax, update=update):
            for j in range(GROUP):
                if j < last:
                    scores(c, j + 1)
                else:
                    scores(jnp.minimum(c + 1, n_chunks - 1), 0)
                softmax(j)
                if j > 0:
                    update(c, j - 1)
                else:
                    update(jnp.maximum(c - 1, 0), last)
            return carry

        lax.fori_loop(0, n_chunks, step, 0)
        update(n_chunks - 1, last)

    for hd in range(N_HEADS):
        o = acc_ref[hd] * (1.0 / l_ref[hd:hd + 1, :])
        o_ref[0, :, hd * HEAD_DIM:(hd + 1) * HEAD_DIM] = o.T.astype(BF16)


def _attn_call(qT, k, vT):
    b, n_q, s = qT.shape
    n_kv = k.shape[2]
    tq = min(ATTN_Q, s)
    tk = min(ATTN_K, s)
    kernel = functools.partial(_attn_kernel, n_chunks=s // tk, tk=tk)
    return pl.pallas_call(
        kernel,
        grid=(b, s // tq),
        in_specs=[
            pl.BlockSpec((1, n_q, tq), lambda bi, i: (bi, 0, i)),
            pl.BlockSpec((1, s, n_kv), lambda bi, i: (bi, 0, 0), pipeline_mode=pl.Buffered(1)),
            pl.BlockSpec((1, n_kv, s), lambda bi, i: (bi, 0, 0), pipeline_mode=pl.Buffered(1)),
        ],
        out_specs=pl.BlockSpec((1, tq, n_q), lambda bi, i: (bi, i, 0)),
        out_shape=jax.ShapeDtypeStruct((b, s, n_q), BF16),
        scratch_shapes=[
            pltpu.VMEM((N_HEADS, tq), F32),
            pltpu.VMEM((N_HEADS, tq), F32),
            pltpu.VMEM((N_HEADS, tq), F32),
            pltpu.VMEM((N_HEADS, HEAD_DIM, tq), F32),
            pltpu.VMEM((2, tk, tq), F32),
            pltpu.VMEM((2, tk, tq), BF16),
        ],
        compiler_params=_params(2, True),
        name="attn",
    )(qT, k, vT)


def _lru_sweep(x_ref, xp_ref, xn_ref, g_ref, wxl_ref, cw_ref, cb_ref, wg_ref, bg_ref, cneg_ref,
               xl_scr, a_scr, u_scr, carry_ref, write_row, *, reverse, t):
    i = pl.program_id(1)
    nb = pl.num_programs(1)
    blk = nb - 1 - i if reverse else i
    d_lru = wxl_ref.shape[1]
    groups = t // SUBLANES

    @pl.when(i == 0)
    def _():
        carry_ref[...] = jnp.zeros(carry_ref.shape, F32)

    xp = xp_ref[0] * jnp.where(blk > 0, 1.0, 0.0)
    xn = xn_ref[0] * jnp.where(blk < nb - 1, 1.0, 0.0)
    xe = jnp.concatenate([xp, x_ref[0], xn], axis=0)
    he = (xe * _rms_scale(xe) * g_ref[...]).astype(BF16)
    xl_scr[...] = jnp.dot(he, wxl_ref[...], preferred_element_type=F32)

    left = CONV_W // 2
    xc = cb_ref[...] + cw_ref[0:1, :] * xl_scr[pl.ds(HALO - left, t), :]
    for j in range(1, CONV_W):
        xc = xc + cw_ref[j:j + 1, :] * xl_scr[pl.ds(HALO - left + j, t), :]

    sub = lax.broadcasted_iota(jnp.int32, (1, SUBLANES, LRU_BLOCK), 1)
    for n in range(N_LRU_BLOCKS):
        cols = slice(n * LRU_BLOCK, (n + 1) * LRU_BLOCK)
        xcn = xc[:, cols]
        gz = jnp.dot(xcn.astype(BF16), wg_ref[n], preferred_element_type=F32) + bg_ref[n]
        r = jax.nn.sigmoid(gz[:, :LRU_BLOCK])
        ig = jax.nn.sigmoid(gz[:, LRU_BLOCK:])
        log_a = cneg_ref[:, cols] * r
        a = jnp.exp(log_a)
        u = jnp.sqrt((1.0 - a) * (1.0 + a)) * (ig * xcn)
        a3 = a.reshape(groups, SUBLANES, LRU_BLOCK)
        u3 = u.reshape(groups, SUBLANES, LRU_BLOCK)
        for dist in (1, 2, 4):
            if reverse:
                shift, ok = SUBLANES - dist, sub < SUBLANES - dist
            else:
                shift, ok = dist, sub >= dist
            a_nb = pltpu.roll(a3, shift, axis=1)
            u_nb = pltpu.roll(u3, shift, axis=1)
            u3 = jnp.where(ok, a3 * u_nb + u3, u3)
            a3 = jnp.where(ok, a3 * a_nb, a3)
        a_scr[:, :, cols] = a3
        u_scr[:, :, cols] = u3

    edge = 0 if reverse else SUBLANES - 1

    def step(j, hrow):
        jj = groups - 1 - j if reverse else j
        rows = u_scr[jj] + a_scr[jj] * hrow
        write_row(jj, rows)
        return jnp.broadcast_to(rows[edge:edge + 1, :], (SUBLANES, d_lru))

    carry_ref[...] = lax.fori_loop(0, groups, step, carry_ref[...], unroll=4)
    return he[HALO:HALO + t]


def _lru_f_kernel(x_ref, xp_ref, xn_ref, g_ref, wxl_ref, cw_ref, cb_ref, wg_ref, bg_ref, cneg_ref,
                  hf_ref, xl_scr, a_scr, u_scr, carry_ref, *, t):
    def write_row(j, rows):
        hf_ref[0, j] = rows

    _lru_sweep(x_ref, xp_ref, xn_ref, g_ref, wxl_ref, cw_ref, cb_ref, wg_ref, bg_ref, cneg_ref,
               xl_scr, a_scr, u_scr, carry_ref, write_row, reverse=False, t=t)


def _merge_kernel(x_ref, xp_ref, xn_ref, g_ref, wxl_ref, cw_ref, cb_ref, wg_ref, bg_ref, cneg_ref,
                  hf_ref, att_ref, wgl_ref, wga_ref, wm_ref, bm_ref, wb0_ref, wb1_ref, wo_ref, gfin_ref,
                  y_ref, xl_scr, a_scr, u_scr, carry_ref, hb_scr, *, t, final_norm):
    def write_row(j, rows):
        hb_scr[j] = rows

    h = _lru_sweep(x_ref, xp_ref, xn_ref, g_ref, wxl_ref, cw_ref, cb_ref, wg_ref, bg_ref, cneg_ref,
                   xl_scr, a_scr, u_scr, carry_ref, write_row, reverse=True, t=t)
    d = x_ref.shape[2]
    lsum = (hf_ref[0] + hb_scr[...]).reshape(t, wxl_ref.shape[1])
    g_l = jnp.dot(h, wgl_ref[...], preferred_element_type=F32)
    l_out = (lsum * (g_l * jax.nn.sigmoid(g_l))).astype(BF16)
    g_a = jnp.dot(h, wga_ref[...], preferred_element_type=F32)
    a_out = (att_ref[0].astype(F32) * (g_a * jax.nn.sigmoid(g_a))).astype(BF16)
    a_proj = jnp.dot(a_out, wb0_ref[...], preferred_element_type=F32)
    l_proj = jnp.dot(l_out, wb1_ref[...], preferred_element_type=F32)
    gates = jax.nn.sigmoid(jnp.dot(h, wm_ref[...], preferred_element_type=F32) + bm_ref[...])
    merged = gates[:, :d] * a_proj + gates[:, d:] * l_proj
    y = x_ref[0] + jnp.dot(merged.astype(BF16), wo_ref[...], preferred_element_type=F32)
    if final_norm:
        y = y * _rms_scale(y) * gfin_ref[...]
    y_ref[0] = y


def _lru_specs(s, t, d, d_lru, reverse):
    nb = s // t
    groups = t // SUBLANES
    last_group = s // SUBLANES - 1

    def blk(i):
        return nb - 1 - i if reverse else i

    x_spec = pl.BlockSpec((1, t, d), lambda bi, i: (bi, blk(i), 0))
    xp_spec = pl.BlockSpec((1, HALO, d), lambda bi, i: (bi, jnp.maximum(blk(i) * groups - 1, 0), 0))
    xn_spec = pl.BlockSpec((1, HALO, d), lambda bi, i: (bi, jnp.minimum((blk(i) + 1) * groups, last_group), 0))
    hf_spec = pl.BlockSpec((1, groups, SUBLANES, d_lru), lambda bi, i: (bi, blk(i), 0, 0))
    scratch = [
        pltpu.VMEM((t + 2 * HALO, d_lru), F32),
        pltpu.VMEM((groups, SUBLANES, d_lru), F32),
        pltpu.VMEM((groups, SUBLANES, d_lru), F32),
        pltpu.VMEM((SUBLANES, d_lru), F32),
    ]
    return x_spec, xp_spec, xn_spec, hf_spec, scratch, blk


def _lru_f_call(x, g_in, w_xl, conv_w, conv_b, w_gate, b_gate, cneg):
    b, s, d = x.shape
    d_lru = w_xl.shape[1]
    t = min(LRU_ROWS, s)
    x_spec, xp_spec, xn_spec, hf_spec, scratch, _ = _lru_specs(s, t, d, d_lru, False)
    return pl.pallas_call(
        functools.partial(_lru_f_kernel, t=t),
        grid=(b, s // t),
        in_specs=[x_spec, xp_spec, xn_spec, _const_spec((1, d)), _const_spec(w_xl.shape),
                  _const_spec(conv_w.shape), _const_spec(conv_b.shape), _const_spec(w_gate.shape),
                  _const_spec(b_gate.shape), _const_spec(cneg.shape)],
        out_specs=hf_spec,
        out_shape=jax.ShapeDtypeStruct((b, s // SUBLANES, SUBLANES, d_lru), F32),
        scratch_shapes=scratch,
        compiler_params=_params(2, True),
        name="lru_f",
    )(x, x, x, g_in, w_xl, conv_w, conv_b, w_gate, b_gate, cneg)


def _merge_call(x, g_in, w_xl, conv_w, conv_b, w_gate, b_gate, cneg, hf, att,
                w_gl, w_ga, w_m, b_m, w_b0, w_b1, w_o, g_fin, final_norm):
    b, s, d = x.shape
    d_lru = w_xl.shape[1]
    t = min(LRU_ROWS, s)
    x_spec, xp_spec, xn_spec, hf_spec, scratch, blk = _lru_specs(s, t, d, d_lru, True)
    tok = lambda width: pl.BlockSpec((1, t, width), lambda bi, i: (bi, blk(i), 0))
    consts = [g_in, w_xl, conv_w, conv_b, w_gate, b_gate, cneg]
    tail = [w_gl, w_ga, w_m, b_m, w_b0, w_b1, w_o, g_fin]
    return pl.pallas_call(
        functools.partial(_merge_kernel, t=t, final_norm=final_norm),
        grid=(b, s // t),
        in_specs=[x_spec, xp_spec, xn_spec] + [_const_spec(c.shape) for c in consts]
        + [hf_spec, tok(att.shape[2])] + [_const_spec(c.shape) for c in tail],
        out_specs=tok(d),
        out_shape=jax.ShapeDtypeStruct((b, s, d), F32),
        scratch_shapes=scratch + [pltpu.VMEM((t // SUBLANES, SUBLANES, d_lru), F32)],
        compiler_params=_params(2, True),
        name="merge",
    )(x, x, x, *consts, hf, att, *tail)


def _rope_tables(s, gain, scale):
    rows_n = s // GRID_W
    rows = jnp.repeat(jnp.arange(rows_n, dtype=F32), GRID_W)
    cols = jnp.tile(jnp.arange(GRID_W, dtype=F32), rows_n)
    n_pair_axis = HEAD_DIM // 4
    inv_freq = ROPE_THETA ** (-jnp.arange(n_pair_axis, dtype=F32) / n_pair_axis)
    ang = jnp.concatenate([rows[:, None] * inv_freq, cols[:, None] * inv_freq], axis=-1)
    cos = jnp.cos(ang).T
    sin = jnp.sin(ang).T
    c = jnp.concatenate([cos, cos], axis=0)
    sn = jnp.concatenate([-sin, sin], axis=0)
    gain = gain.astype(F32)
    return (gain[:, None] * c) * scale, (jnp.roll(gain, HALF)[:, None] * sn) * scale


def _layer_params(w_in, q_norm, k_norm, w_rgate, b_rgate, w_igate, b_igate, lam, w_branch, w_out, d):
    d_attn = N_HEADS * HEAD_DIM
    d_kv = N_KV_HEADS * HEAD_DIM
    d_lru = N_LRU_BLOCKS * LRU_BLOCK
    splits = [d_attn, d_kv, d_kv, d_attn, d_lru, d_lru]
    offs = [0]
    for w in splits:
        offs.append(offs[-1] + w)
    w_q, w_k, w_v, w_ga, w_xl, w_gl = (w_in[:, offs[j]:offs[j + 1]] for j in range(6))
    w_m = w_in[:, offs[6]:]
    perm = jnp.concatenate([jnp.arange(0, HEAD_DIM, 2), jnp.arange(1, HEAD_DIM, 2)])
    permute = lambda w, nh: w.reshape(d, nh, HEAD_DIM)[:, :, perm].reshape(d, nh * HEAD_DIM)
    w_qkvT = jnp.concatenate([permute(w_q, N_HEADS), permute(w_k, N_KV_HEADS), w_v], axis=1).T.astype(BF16)
    w_gate = jnp.concatenate([w_rgate, w_igate], axis=-1).astype(BF16)
    b_gate = jnp.concatenate([b_rgate.reshape(2, N_LRU_BLOCKS, 1, LRU_BLOCK),
                              b_igate.reshape(2, N_LRU_BLOCKS, 1, LRU_BLOCK)], axis=-1).astype(F32)
    cneg = (-RG_C * jax.nn.softplus(-lam.astype(F32))).reshape(2, 1, d_lru)
    return dict(
        w_qkvT=w_qkvT, gq=q_norm[perm], gk=k_norm[perm],
        w_ga=w_ga.astype(BF16), w_xl=w_xl.astype(BF16), w_gl=w_gl.astype(BF16), w_m=w_m.astype(BF16),
        w_gate=w_gate, b_gate=b_gate, cneg=cneg,
        w_b0=w_branch[0].astype(BF16), w_b1=w_branch[1].astype(BF16), w_o=w_out.astype(BF16),
    )


def _layer(x, p, g_in, conv_w, conv_b, b_m, g_fin, final_norm):
    s = x.shape[1]
    q_scale = math.log2(math.e) / math.sqrt(HEAD_DIM)
    cq, sq = _rope_tables(s, p["gq"], q_scale)
    ck, sk = _rope_tables(s, p["gk"], 1.0)
    qT, k, vT = _qkv_call(x, g_in, p["w_qkvT"], cq, sq, ck, sk)
    att = _attn_call(qT, k, vT)
    lru = (g_in, p["w_xl"], conv_w, conv_b)
    hf = _lru_f_call(x, *lru, p["w_gate"][0], p["b_gate"][0], p["cneg"][0])
    return _merge_call(x, *lru, p["w_gate"][1], p["b_gate"][1], p["cneg"][1], hf, att,
                       p["w_gl"], p["w_ga"], p["w_m"], b_m, p["w_b0"], p["w_b1"], p["w_o"], g_fin, final_norm)


def kernel(x_prompt, x_sample, norm_in, w_in, b_merge, q_norm, k_norm, conv_w, conv_b, w_rgate, b_rgate,
           w_igate, b_igate, lam, w_branch, w_out, norm_final):
    depth, d = norm_in.shape
    layers = [
        _layer_params(w_in[l], q_norm[l], k_norm[l], w_rgate[l], b_rgate[l], w_igate[l], b_igate[l], lam[l],
                      w_branch[l], w_out[l], d)
        for l in range(depth)
    ]
    g_fin = norm_final.reshape(1, d).astype(F32)
    outs = []
    for x in (x_prompt, x_sample):
        for l in range(depth):
            x = _layer(x, layers[l], norm_in[l].reshape(1, d).astype(F32), conv_w[l].astype(F32),
                       conv_b[l].reshape(1, -1).astype(F32), b_merge[l].reshape(1, -1).astype(F32), g_fin,
                       l == depth - 1)
        outs.append(x)
    return tuple(outs)
```

```python
import functools
import math

import jax
import jax.numpy as jnp
from jax import lax
from jax.experimental import pallas as pl
from jax.experimental.pallas import tpu as pltpu

N_HEADS = 8
N_KV_HEADS = 2
GROUP = N_HEADS // N_KV_HEADS
HEAD_DIM = 128
HALF = HEAD_DIM // 2
N_LRU_BLOCKS = 8
LRU_BLOCK = 128
CONV_W = 4
RG_C = 8.0
EPS = 1e-6
GRID_W = 64
ROPE_THETA = 10000.0

SUBLANES = 8
HALO = SUBLANES
V7X_VMEM_LIMIT = 56 * 1024 * 1024

QKV_ROWS = 512
ATTN_Q = 512
ATTN_K = 256
LRU_ROWS = 256

F32 = jnp.float32
BF16 = jnp.bfloat16


def _const_spec(shape):
    nd = len(shape)
    return pl.BlockSpec(shape, lambda *_: (0,) * nd, pipeline_mode=pl.Buffered(1))


def _params(n_axes, sequential_last):
    sem = ["parallel"] * n_axes
    if sequential_last:
        sem[-1] = "arbitrary"
    return pltpu.CompilerParams(dimension_semantics=tuple(sem), vmem_limit_bytes=V7X_VMEM_LIMIT)


def _rms_scale(x):
    return lax.rsqrt(jnp.mean(x * x, axis=-1, keepdims=True) + EPS)


def _qkv_kernel(x_ref, g_ref, w_ref, cq_ref, sq_ref, ck_ref, sk_ref, qT_ref, k_ref, vT_ref):
    x = x_ref[0]
    h = (x * _rms_scale(x) * g_ref[...]).astype(BF16)
    zT = lax.dot_general(w_ref[...], h, (((1,), (1,)), ((), ())), preferred_element_type=F32)

    def norm_rope(z, c, s):
        n = z * lax.rsqrt(jnp.mean(z * z, axis=0, keepdims=True) + EPS)
        swapped = jnp.concatenate([n[HALF:], n[:HALF]], axis=0)
        return n * c + swapped * s

    cq, sq = cq_ref[...], sq_ref[...]
    for hd in range(N_HEADS):
        rows = slice(hd * HEAD_DIM, (hd + 1) * HEAD_DIM)
        qh = norm_rope(zT[rows], cq, sq).astype(BF16)
        tq = qT_ref.shape[3]
        for qb in range(qT_ref.shape[1]):
            qT_ref[0, qb, rows, :] = qh[:, qb * tq:(qb + 1) * tq]
    ck, sk = ck_ref[...], sk_ref[...]
    k_off = N_HEADS * HEAD_DIM
    for kv in range(N_KV_HEADS):
        rows = slice(k_off + kv * HEAD_DIM, k_off + (kv + 1) * HEAD_DIM)
        kT = norm_rope(zT[rows], ck, sk)
        k_ref[0, :, kv * HEAD_DIM:(kv + 1) * HEAD_DIM] = kT.T.astype(BF16)
    v_off = k_off + N_KV_HEADS * HEAD_DIM
    vT_ref[0] = zT[v_off:v_off + N_KV_HEADS * HEAD_DIM].astype(BF16)


def _qkv_call(x, g_in, w_qkvT, cq, sq, ck, sk):
    b, s, d = x.shape
    t = min(QKV_ROWS, s)
    tq = min(ATTN_Q, t)
    n_q, n_kv = N_HEADS * HEAD_DIM, N_KV_HEADS * HEAD_DIM
    tab = pl.BlockSpec((HEAD_DIM, t), lambda bi, i: (0, i))
    return pl.pallas_call(
        _qkv_kernel,
        grid=(b, s // t),
        in_specs=[
            pl.BlockSpec((1, t, d), lambda bi, i: (bi, i, 0)),
            _const_spec((1, d)),
            _const_spec(w_qkvT.shape),
            tab, tab, tab, tab,
        ],
        out_specs=[
            pl.BlockSpec((1, t // tq, n_q, tq), lambda bi, i: (bi, i, 0, 0)),
            pl.BlockSpec((1, t, n_kv), lambda bi, i: (bi, i, 0)),
            pl.BlockSpec((1, n_kv, t), lambda bi, i: (bi, 0, i)),
        ],
        out_shape=[
            jax.ShapeDtypeStruct((b, s // tq, n_q, tq), BF16),
            jax.ShapeDtypeStruct((b, s, n_kv), BF16),
            jax.ShapeDtypeStruct((b, n_kv, s), BF16),
        ],
        compiler_params=_params(2, False),
        name="qkv",
    )(x, g_in, w_qkvT, cq, sq, ck, sk)


def _attn_kernel(qT_ref, k_ref, vT_ref, o_ref, m_ref, l_ref, alpha_ref, acc_ref, s_scr, cmax_ref, p_scr, *,
                 n_chunks, tk):
    m_ref[...] = jnp.full(m_ref.shape, -jnp.inf, F32)
    l_ref[...] = jnp.zeros(l_ref.shape, F32)
    acc_ref[...] = jnp.zeros(acc_ref.shape, F32)

    for kv in range(N_KV_HEADS):
        cols = slice(kv * HEAD_DIM, (kv + 1) * HEAD_DIM)

        def scores(c, j, cols=cols, kv=kv):
            hd = kv * GROUP + j
            off = pl.multiple_of(c * tk, tk)
            kc = k_ref[0, pl.ds(off, tk), cols]
            qT = qT_ref[0, 0, hd * HEAD_DIM:(hd + 1) * HEAD_DIM, :]
            s = jnp.dot(kc, qT, preferred_element_type=F32)
            s_scr[j % 2] = s
            cmax_ref[j % 2] = jnp.max(s, axis=0, keepdims=True)

        def softmax(j, kv=kv):
            hd = kv * GROUP + j
            s = s_scr[j % 2]
            m_old = m_ref[hd:hd + 1, :]
            m_new = jnp.maximum(m_old, cmax_ref[j % 2])
            alpha = jnp.exp2(m_old - m_new)
            p = jnp.exp2(s - m_new)
            l_ref[hd:hd + 1, :] = alpha * l_ref[hd:hd + 1, :] + jnp.sum(p, axis=0, keepdims=True)
            m_ref[hd:hd + 1, :] = m_new
            alpha_ref[hd:hd + 1, :] = alpha
            p_scr[j % 2] = p.astype(BF16)

        def update(c, j, cols=cols, kv=kv):
            hd = kv * GROUP + j
            off = pl.multiple_of(c * tk, tk)
            vc = vT_ref[0, cols, pl.ds(off, tk)]
            pv = jnp.dot(vc, p_scr[j % 2], preferred_element_type=F32)
            acc_ref[hd] = alpha_ref[hd:hd + 1, :] * acc_ref[hd] + pv

        last = GROUP - 1
        p_scr[last % 2] = jnp.zeros(p_scr.shape[1:], BF16)
        alpha_ref[kv * GROUP + last:kv * GROUP + last + 1, :] = jnp.ones((1, alpha_ref.shape[1]), F32)
        scores(0, 0)

        def step(c, carry, scores=scores, softmax=softmax, update=update):
            for j in range(GROUP):
                if j < last:
                    scores(c, j + 1)
                else:
                    scores(jnp.minimum(c + 1, n_chunks - 1), 0)
                softmax(j)
                if j > 0:
                    update(c, j - 1)
                else:
                    update(jnp.maximum(c - 1, 0), last)
            return carry

        lax.fori_loop(0, n_chunks, step, 0)
        update(n_chunks - 1, last)

    for hd in range(N_HEADS):
        o = acc_ref[hd] * (1.0 / l_ref[hd:hd + 1, :])
        o_ref[0, :, hd * HEAD_DIM:(hd + 1) * HEAD_DIM] = o.T.astype(BF16)


def _attn_call(qT, k, vT):
    b, n_qb, n_q, tq = qT.shape
    s, n_kv = k.shape[1:]
    tk = min(ATTN_K, s)
    kernel = functools.partial(_attn_kernel, n_chunks=s // tk, tk=tk)
    return pl.pallas_call(
        kernel,
        grid=(b, n_qb),
        in_specs=[
            pl.BlockSpec((1, 1, n_q, tq), lambda bi, i: (bi, i, 0, 0)),
            pl.BlockSpec((1, s, n_kv), lambda bi, i: (bi, 0, 0), pipeline_mode=pl.Buffered(1)),
            pl.BlockSpec((1, n_kv, s), lambda bi, i: (bi, 0, 0), pipeline_mode=pl.Buffered(1)),
        ],
        out_specs=pl.BlockSpec((1, tq, n_q), lambda bi, i: (bi, i, 0)),
        out_shape=jax.ShapeDtypeStruct((b, s, n_q), BF16),
        scratch_shapes=[
            pltpu.VMEM((N_HEADS, tq), F32),
            pltpu.VMEM((N_HEADS, tq), F32),
            pltpu.VMEM((N_HEADS, tq), F32),
            pltpu.VMEM((N_HEADS, HEAD_DIM, tq), F32),
            pltpu.VMEM((2, tk, tq), F32),
            pltpu.VMEM((2, 1, tq), F32),
            pltpu.VMEM((2, tk, tq), BF16),
        ],
        compiler_params=_params(2, True),
        name="attn",
    )(qT, k, vT)


def _lru_sweep(x_ref, xp_ref, xn_ref, g_ref, wxl_ref, cw_ref, cb_ref, wg_ref, bg_ref, cneg_ref,
               xl_scr, a_scr, u_scr, carry_ref, write_row, *, reverse, t):
    i = pl.program_id(1)
    nb = pl.num_programs(1)
    blk = nb - 1 - i if reverse else i
    d_lru = wxl_ref.shape[1]
    groups = t // SUBLANES

    @pl.when(i == 0)
    def _():
        carry_ref[...] = jnp.zeros(carry_ref.shape, F32)

    xp = xp_ref[0] * jnp.where(blk > 0, 1.0, 0.0)
    xn = xn_ref[0] * jnp.where(blk < nb - 1, 1.0, 0.0)
    xe = jnp.concatenate([xp, x_ref[0], xn], axis=0)
    he = (xe * _rms_scale(xe) * g_ref[...]).astype(BF16)
    xl_scr[...] = jnp.dot(he, wxl_ref[...], preferred_element_type=F32)

    left = CONV_W // 2
    xc = cb_ref[...] + cw_ref[0:1, :] * xl_scr[pl.ds(HALO - left, t), :]
    for j in range(1, CONV_W):
        xc = xc + cw_ref[j:j + 1, :] * xl_scr[pl.ds(HALO - left + j, t), :]

    sub = lax.broadcasted_iota(jnp.int32, (1, SUBLANES, LRU_BLOCK), 1)
    for n in range(N_LRU_BLOCKS):
        cols = slice(n * LRU_BLOCK, (n + 1) * LRU_BLOCK)
        xcn = xc[:, cols]
        gz = jnp.dot(xcn.astype(BF16), wg_ref[n], preferred_element_type=F32) + bg_ref[n]
        r = jax.nn.sigmoid(gz[:, :LRU_BLOCK])
        ig = jax.nn.sigmoid(gz[:, LRU_BLOCK:])
        log_a = cneg_ref[:, cols] * r
        a = jnp.exp(log_a)
        u = jnp.sqrt((1.0 - a) * (1.0 + a)) * (ig * xcn)
        a3 = a.reshape(groups, SUBLANES, LRU_BLOCK)
        u3 = u.reshape(groups, SUBLANES, LRU_BLOCK)
        for dist in (1, 2, 4):
            if reverse:
                shift, ok = SUBLANES - dist, sub < SUBLANES - dist
            else:
                shift, ok = dist, sub >= dist
            a_nb = pltpu.roll(a3, shift, axis=1)
            u_nb = pltpu.roll(u3, shift, axis=1)
            u3 = jnp.where(ok, a3 * u_nb + u3, u3)
            a3 = jnp.where(ok, a3 * a_nb, a3)
        a_scr[:, :, cols] = a3
        u_scr[:, :, cols] = u3

    edge = 0 if reverse else SUBLANES - 1

    def step(j, hrow):
        jj = groups - 1 - j if reverse else j
        rows = u_scr[jj] + a_scr[jj] * hrow
        write_row(jj, rows)
        return jnp.broadcast_to(rows[edge:edge + 1, :], (SUBLANES, d_lru))

    carry_ref[...] = lax.fori_loop(0, groups, step, carry_ref[...], unroll=4)
    return he[HALO:HALO + t]


def _lru_f_kernel(x_ref, xp_ref, xn_ref, g_ref, wxl_ref, cw_ref, cb_ref, wg_ref, bg_ref, cneg_ref,
                  hf_ref, xl_scr, a_scr, u_scr, carry_ref, *, t):
    def write_row(j, rows):
        hf_ref[0, j] = rows

    _lru_sweep(x_ref, xp_ref, xn_ref, g_ref, wxl_ref, cw_ref, cb_ref, wg_ref, bg_ref, cneg_ref,
               xl_scr, a_scr, u_scr, carry_ref, write_row, reverse=False, t=t)


def _merge_kernel(x_ref, xp_ref, xn_ref, g_ref, wxl_ref, cw_ref, cb_ref, wg_ref, bg_ref, cneg_ref,
                  hf_ref, att_ref, wgl_ref, wga_ref, wm_ref, bm_ref, wb0_ref, wb1_ref, wo_ref, gfin_ref,
                  y_ref, xl_scr, a_scr, u_scr, carry_ref, hb_scr, *, t, final_norm):
    def write_row(j, rows):
        hb_scr[j] = rows

    h = _lru_sweep(x_ref, xp_ref, xn_ref, g_ref, wxl_ref, cw_ref, cb_ref, wg_ref, bg_ref, cneg_ref,
                   xl_scr, a_scr, u_scr, carry_ref, write_row, reverse=True, t=t)
    d = x_ref.shape[2]
    lsum = (hf_ref[0] + hb_scr[...]).reshape(t, wxl_ref.shape[1])
    g_l = jnp.dot(h, wgl_ref[...], preferred_element_type=F32)
    l_out = (lsum * (g_l * jax.nn.sigmoid(g_l))).astype(BF16)
    g_a = jnp.dot(h, wga_ref[...], preferred_element_type=F32)
    a_out = (att_ref[0].astype(F32) * (g_a * jax.nn.sigmoid(g_a))).astype(BF16)
    a_proj = jnp.dot(a_out, wb0_ref[...], preferred_element_type=F32)
    l_proj = jnp.dot(l_out, wb1_ref[...], preferred_element_type=F32)
    gates = jax.nn.sigmoid(jnp.dot(h, wm_ref[...], preferred_element_type=F32) + bm_ref[...])
    merged = gates[:, :d] * a_proj + gates[:, d:] * l_proj
    y = x_ref[0] + jnp.dot(merged.astype(BF16), wo_ref[...], preferred_element_type=F32)
    if final_norm:
        y = y * _rms_scale(y) * gfin_ref[...]
    y_ref[0] = y


def _lru_specs(s, t, d, d_lru, reverse):
    nb = s // t
    groups = t // SUBLANES
    last_group = s // SUBLANES - 1

    def blk(i):
        return nb - 1 - i if reverse else i

    x_spec = pl.BlockSpec((1, t, d), lambda bi, i: (bi, blk(i), 0))
    xp_spec = pl.BlockSpec((1, HALO, d), lambda bi, i: (bi, jnp.maximum(blk(i) * groups - 1, 0), 0))
    xn_spec = pl.BlockSpec((1, HALO, d), lambda bi, i: (bi, jnp.minimum((blk(i) + 1) * groups, last_group), 0))
    hf_spec = pl.BlockSpec((1, groups, SUBLANES, d_lru), lambda bi, i: (bi, blk(i), 0, 0))
    scratch = [
        pltpu.VMEM((t + 2 * HALO, d_lru), F32),
        pltpu.VMEM((groups, SUBLANES, d_lru), F32),
        pltpu.VMEM((groups, SUBLANES, d_lru), F32),
        pltpu.VMEM((SUBLANES, d_lru), F32),
    ]
    return x_spec, xp_spec, xn_spec, hf_spec, scratch, blk


def _lru_f_call(x, g_in, w_xl, conv_w, conv_b, w_gate, b_gate, cneg):
    b, s, d = x.shape
    d_lru = w_xl.shape[1]
    t = min(LRU_ROWS, s)
    x_spec, xp_spec, xn_spec, hf_spec, scratch, _ = _lru_specs(s, t, d, d_lru, False)
    return pl.pallas_call(
        functools.partial(_lru_f_kernel, t=t),
        grid=(b, s // t),
        in_specs=[x_spec, xp_spec, xn_spec, _const_spec((1, d)), _const_spec(w_xl.shape),
                  _const_spec(conv_w.shape), _const_spec(conv_b.shape), _const_spec(w_gate.shape),
                  _const_spec(b_gate.shape), _const_spec(cneg.shape)],
        out_specs=hf_spec,
        out_shape=jax.ShapeDtypeStruct((b, s // SUBLANES, SUBLANES, d_lru), F32),
        scratch_shapes=scratch,
        compiler_params=_params(2, True),
        name="lru_f",
    )(x, x, x, g_in, w_xl, conv_w, conv_b, w_gate, b_gate, cneg)


def _merge_call(x, g_in, w_xl, conv_w, conv_b, w_gate, b_gate, cneg, hf, att,
                w_gl, w_ga, w_m, b_m, w_b0, w_b1, w_o, g_fin, final_norm):
    b, s, d = x.shape
    d_lru = w_xl.shape[1]
    t = min(LRU_ROWS, s)
    x_spec, xp_spec, xn_spec, hf_spec, scratch, blk = _lru_specs(s, t, d, d_lru, True)
    tok = lambda width: pl.BlockSpec((1, t, width), lambda bi, i: (bi, blk(i), 0))
    consts = [g_in, w_xl, conv_w, conv_b, w_gate, b_gate, cneg]
    tail = [w_gl, w_ga, w_m, b_m, w_b0, w_b1, w_o, g_fin]
    return pl.pallas_call(
        functools.partial(_merge_kernel, t=t, final_norm=final_norm),
        grid=(b, s // t),
        in_specs=[x_spec, xp_spec, xn_spec] + [_const_spec(c.shape) for c in consts]
        + [hf_spec, tok(att.shape[2])] + [_const_spec(c.shape) for c in tail],
        out_specs=tok(d),
        out_shape=jax.ShapeDtypeStruct((b, s, d), F32),
        scratch_shapes=scratch + [pltpu.VMEM((t // SUBLANES, SUBLANES, d_lru), F32)],
        compiler_params=_params(2, True),
        name="merge",
    )(x, x, x, *consts, hf, att, *tail)


def _rope_tables(s, gain, scale):
    rows_n = s // GRID_W
    rows = jnp.repeat(jnp.arange(rows_n, dtype=F32), GRID_W)
    cols = jnp.tile(jnp.arange(GRID_W, dtype=F32), rows_n)
    n_pair_axis = HEAD_DIM // 4
    inv_freq = ROPE_THETA ** (-jnp.arange(n_pair_axis, dtype=F32) / n_pair_axis)
    ang = jnp.concatenate([rows[:, None] * inv_freq, cols[:, None] * inv_freq], axis=-1)
    cos = jnp.cos(ang).T
    sin = jnp.sin(ang).T
    c = jnp.concatenate([cos, cos], axis=0)
    sn = jnp.concatenate([-sin, sin], axis=0)
    gain = gain.astype(F32)
    return (gain[:, None] * c) * scale, (jnp.roll(gain, HALF)[:, None] * sn) * scale


def _layer_params(w_in, q_norm, k_norm, w_rgate, b_rgate, w_igate, b_igate, lam, w_branch, w_out, d):
    d_attn = N_HEADS * HEAD_DIM
    d_kv = N_KV_HEADS * HEAD_DIM
    d_lru = N_LRU_BLOCKS * LRU_BLOCK
    splits = [d_attn, d_kv, d_kv, d_attn, d_lru, d_lru]
    offs = [0]
    for w in splits:
        offs.append(offs[-1] + w)
    w_q, w_k, w_v, w_ga, w_xl, w_gl = (w_in[:, offs[j]:offs[j + 1]] for j in range(6))
    w_m = w_in[:, offs[6]:]
    perm = jnp.concatenate([jnp.arange(0, HEAD_DIM, 2), jnp.arange(1, HEAD_DIM, 2)])
    permute = lambda w, nh: w.reshape(d, nh, HEAD_DIM)[:, :, perm].reshape(d, nh * HEAD_DIM)
    w_qkvT = jnp.concatenate([permute(w_q, N_HEADS), permute(w_k, N_KV_HEADS), w_v], axis=1).T.astype(BF16)
    w_gate = jnp.concatenate([w_rgate, w_igate], axis=-1).astype(BF16)
    b_gate = jnp.concatenate([b_rgate.reshape(2, N_LRU_BLOCKS, 1, LRU_BLOCK),
                              b_igate.reshape(2, N_LRU_BLOCKS, 1, LRU_BLOCK)], axis=-1).astype(F32)
    cneg = (-RG_C * jax.nn.softplus(-lam.astype(F32))).reshape(2, 1, d_lru)
    return dict(
        w_qkvT=w_qkvT, gq=q_norm[perm], gk=k_norm[perm],
        w_ga=w_ga.astype(BF16), w_xl=w_xl.astype(BF16), w_gl=w_gl.astype(BF16), w_m=w_m.astype(BF16),
        w_gate=w_gate, b_gate=b_gate, cneg=cneg,
        w_b0=w_branch[0].astype(BF16), w_b1=w_branch[1].astype(BF16), w_o=w_out.astype(BF16),
    )


def _layer(x, p, g_in, conv_w, conv_b, b_m, g_fin, final_norm):
    s = x.shape[1]
    q_scale = math.log2(math.e) / math.sqrt(HEAD_DIM)
    cq, sq = _rope_tables(s, p["gq"], q_scale)
    ck, sk = _rope_tables(s, p["gk"], 1.0)
    qT, k, vT = _qkv_call(x, g_in, p["w_qkvT"], cq, sq, ck, sk)
    att = _attn_call(qT, k, vT)
    lru = (g_in, p["w_xl"], conv_w, conv_b)
    hf = _lru_f_call(x, *lru, p["w_gate"][0], p["b_gate"][0], p["cneg"][0])
    return _merge_call(x, *lru, p["w_gate"][1], p["b_gate"][1], p["cneg"][1], hf, att,
                       p["w_gl"], p["w_ga"], p["w_m"], b_m, p["w_b0"], p["w_b1"], p["w_o"], g_fin, final_norm)


def kernel(x_prompt, x_sample, norm_in, w_in, b_merge, q_norm, k_norm, conv_w, conv_b, w_rgate, b_rgate,
           w_igate, b_igate, lam, w_branch, w_out, norm_final):
    depth, d = norm_in.shape
    layers = [
        _layer_params(w_in[l], q_norm[l], k_norm[l], w_rgate[l], b_rgate[l], w_igate[l], b_igate[l], lam[l],
                      w_branch[l], w_out[l], d)
        for l in range(depth)
    ]
    g_fin = norm_final.reshape(1, d).astype(F32)
    outs = []
    for x in (x_prompt, x_sample):
        for l in range(depth):
            x = _layer(x, layers[l], norm_in[l].reshape(1, d).astype(F32), conv_w[l].astype(F32),
                       conv_b[l].reshape(1, -1).astype(F32), b_merge[l].reshape(1, -1).astype(F32), g_fin,
                       l == depth - 1)
        outs.append(x)
    return tuple(outs)
```

```python
import functools
import math

import jax
import jax.numpy as jnp
from jax import lax
from jax.experimental import pallas as pl
from jax.experimental.pallas import tpu as pltpu

N_HEADS = 8
N_KV_HEADS = 2
GROUP = N_HEADS // N_KV_HEADS
HEAD_DIM = 128
HALF = HEAD_DIM // 2
N_LRU_BLOCKS = 8
LRU_BLOCK = 128
CONV_W = 4
RG_C = 8.0
EPS = 1e-6
GRID_W = 64
ROPE_THETA = 10000.0

SUBLANES = 8
ONES_ROWS = 2 * SUBLANES
HALO = SUBLANES
V7X_VMEM_LIMIT = 56 * 1024 * 1024

QKV_ROWS = 512
ATTN_Q = 512
ATTN_K = 256
LRU_ROWS = 256

F32 = jnp.float32
BF16 = jnp.bfloat16


def _const_spec(shape):
    nd = len(shape)
    return pl.BlockSpec(shape, lambda *_: (0,) * nd, pipeline_mode=pl.Buffered(1))


def _params(n_axes, sequential_last):
    sem = ["parallel"] * n_axes
    if sequential_last:
        sem[-1] = "arbitrary"
    return pltpu.CompilerParams(dimension_semantics=tuple(sem), vmem_limit_bytes=V7X_VMEM_LIMIT)


def _rms_scale(x):
    return lax.rsqrt(jnp.mean(x * x, axis=-1, keepdims=True) + EPS)


def _qkv_kernel(x_ref, g_ref, w_ref, cq_ref, sq_ref, ck_ref, sk_ref, qT_ref, k_ref, vT_ref):
    x = x_ref[0]
    h = (x * _rms_scale(x) * g_ref[...]).astype(BF16)
    zT = lax.dot_general(w_ref[...], h, (((1,), (1,)), ((), ())), preferred_element_type=F32)

    def norm_rope(z, c, s):
        n = z * lax.rsqrt(jnp.mean(z * z, axis=0, keepdims=True) + EPS)
        swapped = jnp.concatenate([n[HALF:], n[:HALF]], axis=0)
        return n * c + swapped * s

    cq, sq = cq_ref[...], sq_ref[...]
    for hd in range(N_HEADS):
        rows = slice(hd * HEAD_DIM, (hd + 1) * HEAD_DIM)
        qh = norm_rope(zT[rows], cq, sq).astype(BF16)
        tq = qT_ref.shape[3]
        for qb in range(qT_ref.shape[1]):
            qT_ref[0, qb, rows, :] = qh[:, qb * tq:(qb + 1) * tq]
    ck, sk = ck_ref[...], sk_ref[...]
    k_off = N_HEADS * HEAD_DIM
    for kv in range(N_KV_HEADS):
        rows = slice(k_off + kv * HEAD_DIM, k_off + (kv + 1) * HEAD_DIM)
        kT = norm_rope(zT[rows], ck, sk)
        k_ref[0, :, kv * HEAD_DIM:(kv + 1) * HEAD_DIM] = kT.T.astype(BF16)
    v_off = k_off + N_KV_HEADS * HEAD_DIM
    vT_ref[0] = zT[v_off:v_off + N_KV_HEADS * HEAD_DIM].astype(BF16)


def _qkv_call(x, g_in, w_qkvT, cq, sq, ck, sk):
    b, s, d = x.shape
    t = min(QKV_ROWS, s)
    tq = min(ATTN_Q, t)
    n_q, n_kv = N_HEADS * HEAD_DIM, N_KV_HEADS * HEAD_DIM
    tab = pl.BlockSpec((HEAD_DIM, t), lambda bi, i: (0, i))
    return pl.pallas_call(
        _qkv_kernel,
        grid=(b, s // t),
        in_specs=[
            pl.BlockSpec((1, t, d), lambda bi, i: (bi, i, 0)),
            _const_spec((1, d)),
            _const_spec(w_qkvT.shape),
            tab, tab, tab, tab,
        ],
        out_specs=[
            pl.BlockSpec((1, t // tq, n_q, tq), lambda bi, i: (bi, i, 0, 0)),
            pl.BlockSpec((1, t, n_kv), lambda bi, i: (bi, i, 0)),
            pl.BlockSpec((1, n_kv, t), lambda bi, i: (bi, 0, i)),
        ],
        out_shape=[
            jax.ShapeDtypeStruct((b, s // tq, n_q, tq), BF16),
            jax.ShapeDtypeStruct((b, s, n_kv), BF16),
            jax.ShapeDtypeStruct((b, n_kv, s), BF16),
        ],
        compiler_params=_params(2, False),
        name="qkv",
    )(x, g_in, w_qkvT, cq, sq, ck, sk)


def _attn_kernel(qT_ref, k_ref, vT_ref, o_ref, m_ref, alpha_ref, acc_ref, s_scr, cmax_ref, p_scr, *, n_chunks, tk):
    m_ref[...] = jnp.full(m_ref.shape, -jnp.inf, F32)
    acc_ref[...] = jnp.zeros(acc_ref.shape, F32)
    ones_rows = jnp.ones((ONES_ROWS, tk), BF16)
    last = N_HEADS - 1

    def kv_cols(hd):
        kv = hd // GROUP
        return slice(kv * HEAD_DIM, (kv + 1) * HEAD_DIM)

    def scores(c, hd):
        off = pl.multiple_of(c * tk, tk)
        kc = k_ref[0, pl.ds(off, tk), kv_cols(hd)]
        qT = qT_ref[0, 0, hd * HEAD_DIM:(hd + 1) * HEAD_DIM, :]
        s = jnp.dot(kc, qT, preferred_element_type=F32)
        s_scr[hd] = s
        cmax_ref[hd] = jnp.max(s, axis=0, keepdims=True)

    def softmax(hd):
        m_old = m_ref[hd:hd + 1, :]
        m_new = jnp.maximum(m_old, cmax_ref[hd])
        alpha_ref[hd:hd + 1, :] = jnp.exp2(m_old - m_new)
        m_ref[hd:hd + 1, :] = m_new
        p_scr[hd] = jnp.exp2(s_scr[hd] - m_new).astype(BF16)

    def update(c, hd):
        off = pl.multiple_of(c * tk, tk)
        vc = jnp.concatenate([vT_ref[0, kv_cols(hd), pl.ds(off, tk)], ones_rows], axis=0)
        pv = jnp.dot(vc, p_scr[hd], preferred_element_type=F32)
        acc_ref[hd] = alpha_ref[hd:hd + 1, :] * acc_ref[hd] + pv

    p_scr[last] = jnp.zeros(p_scr.shape[1:], BF16)
    alpha_ref[last:last + 1, :] = jnp.ones((1, alpha_ref.shape[1]), F32)
    scores(0, 0)

    def step(c, carry):
        for hd in range(N_HEADS):
            if hd < last:
                scores(c, hd + 1)
            else:
                scores(jnp.minimum(c + 1, n_chunks - 1), 0)
            softmax(hd)
            if hd > 0:
                update(c, hd - 1)
            else:
                update(jnp.maximum(c - 1, 0), last)
        return carry

    lax.fori_loop(0, n_chunks, step, 0)
    update(n_chunks - 1, last)

    for hd in range(N_HEADS):
        o = acc_ref[hd, :HEAD_DIM, :] * (1.0 / acc_ref[hd, HEAD_DIM:HEAD_DIM + 1, :])
        o_ref[0, :, hd * HEAD_DIM:(hd + 1) * HEAD_DIM] = o.T.astype(BF16)


def _attn_call(qT, k, vT):
    b, n_qb, n_q, tq = qT.shape
    s, n_kv = k.shape[1:]
    tk = min(ATTN_K, s)
    kernel = functools.partial(_attn_kernel, n_chunks=s // tk, tk=tk)
    return pl.pallas_call(
        kernel,
        grid=(b, n_qb),
        in_specs=[
            pl.BlockSpec((1, 1, n_q, tq), lambda bi, i: (bi, i, 0, 0)),
            pl.BlockSpec((1, s, n_kv), lambda bi, i: (bi, 0, 0), pipeline_mode=pl.Buffered(1)),
            pl.BlockSpec((1, n_kv, s), lambda bi, i: (bi, 0, 0), pipeline_mode=pl.Buffered(1)),
        ],
        out_specs=pl.BlockSpec((1, tq, n_q), lambda bi, i: (bi, i, 0)),
        out_shape=jax.ShapeDtypeStruct((b, s, n_q), BF16),
        scratch_shapes=[
            pltpu.VMEM((N_HEADS, tq), F32),
            pltpu.VMEM((N_HEADS, tq), F32),
            pltpu.VMEM((N_HEADS, HEAD_DIM + ONES_ROWS, tq), F32),
            pltpu.VMEM((N_HEADS, tk, tq), F32),
            pltpu.VMEM((N_HEADS, 1, tq), F32),
            pltpu.VMEM((N_HEADS, tk, tq), BF16),
        ],
        compiler_params=_params(2, True),
        name="attn",
    )(qT, k, vT)


def _lru_sweep(x_ref, xp_ref, xn_ref, g_ref, wxl_ref, cw_ref, cb_ref, wg_ref, bg_ref, cneg_ref,
               xl_scr, a_scr, u_scr, carry_ref, write_row, *, reverse, t):
    i = pl.program_id(1)
    nb = pl.num_programs(1)
    blk = nb - 1 - i if reverse else i
    d_lru = wxl_ref.shape[1]
    groups = t // SUBLANES

    @pl.when(i == 0)
    def _():
        carry_ref[...] = jnp.zeros(carry_ref.shape, F32)

    xp = xp_ref[0] * jnp.where(blk > 0, 1.0, 0.0)
    xn = xn_ref[0] * jnp.where(blk < nb - 1, 1.0, 0.0)
    xe = jnp.concatenate([xp, x_ref[0], xn], axis=0)
    he = (xe * _rms_scale(xe) * g_ref[...]).astype(BF16)
    xl_scr[...] = jnp.dot(he, wxl_ref[...], preferred_element_type=F32)

    left = CONV_W // 2
    xc = cb_ref[...] + cw_ref[0:1, :] * xl_scr[pl.ds(HALO - left, t), :]
    for j in range(1, CONV_W):
        xc = xc + cw_ref[j:j + 1, :] * xl_scr[pl.ds(HALO - left + j, t), :]

    sub = lax.broadcasted_iota(jnp.int32, (1, SUBLANES, LRU_BLOCK), 1)
    for n in range(N_LRU_BLOCKS):
        cols = slice(n * LRU_BLOCK, (n + 1) * LRU_BLOCK)
        xcn = xc[:, cols]
        gz = jnp.dot(xcn.astype(BF16), wg_ref[n], preferred_element_type=F32) + bg_ref[n]
        r = jax.nn.sigmoid(gz[:, :LRU_BLOCK])
        ig = jax.nn.sigmoid(gz[:, LRU_BLOCK:])
        log_a = cneg_ref[:, cols] * r
        a = jnp.exp(log_a)
        u = jnp.sqrt((1.0 - a) * (1.0 + a)) * (ig * xcn)
        a3 = a.reshape(groups, SUBLANES, LRU_BLOCK)
        u3 = u.reshape(groups, SUBLANES, LRU_BLOCK)
        for dist in (1, 2, 4):
            if reverse:
                shift, ok = SUBLANES - dist, sub < SUBLANES - dist
            else:
                shift, ok = dist, sub >= dist
            a_nb = pltpu.roll(a3, shift, axis=1)
            u_nb = pltpu.roll(u3, shift, axis=1)
            u3 = jnp.where(ok, a3 * u_nb + u3, u3)
            a3 = jnp.where(ok, a3 * a_nb, a3)
        a_scr[:, :, cols] = a3
        u_scr[:, :, cols] = u3

    edge = 0 if reverse else SUBLANES - 1

    def step(j, hrow):
        jj = groups - 1 - j if reverse else j
        rows = u_scr[jj] + a_scr[jj] * hrow
        write_row(jj, rows)
        return jnp.broadcast_to(rows[edge:edge + 1, :], (SUBLANES, d_lru))

    carry_ref[...] = lax.fori_loop(0, groups, step, carry_ref[...], unroll=4)
    return he[HALO:HALO + t]


def _lru_f_kernel(x_ref, xp_ref, xn_ref, g_ref, wxl_ref, cw_ref, cb_ref, wg_ref, bg_ref, cneg_ref,
                  hf_ref, xl_scr, a_scr, u_scr, carry_ref, *, t):
    def write_row(j, rows):
        hf_ref[0, j] = rows

    _lru_sweep(x_ref, xp_ref, xn_ref, g_ref, wxl_ref, cw_ref, cb_ref, wg_ref, bg_ref, cneg_ref,
               xl_scr, a_scr, u_scr, carry_ref, write_row, reverse=False, t=t)


def _merge_kernel(x_ref, xp_ref, xn_ref, g_ref, wxl_ref, cw_ref, cb_ref, wg_ref, bg_ref, cneg_ref,
                  hf_ref, att_ref, wgl_ref, wga_ref, wm_ref, bm_ref, wb0_ref, wb1_ref, wo_ref, gfin_ref,
                  y_ref, xl_scr, a_scr, u_scr, carry_ref, hb_scr, *, t, final_norm):
    def write_row(j, rows):
        hb_scr[j] = rows

    h = _lru_sweep(x_ref, xp_ref, xn_ref, g_ref, wxl_ref, cw_ref, cb_ref, wg_ref, bg_ref, cneg_ref,
                   xl_scr, a_scr, u_scr, carry_ref, write_row, reverse=True, t=t)
    d = x_ref.shape[2]
    lsum = (hf_ref[0] + hb_scr[...]).reshape(t, wxl_ref.shape[1])
    g_l = jnp.dot(h, wgl_ref[...], preferred_element_type=F32)
    l_out = (lsum * (g_l * jax.nn.sigmoid(g_l))).astype(BF16)
    g_a = jnp.dot(h, wga_ref[...], preferred_element_type=F32)
    a_out = (att_ref[0].astype(F32) * (g_a * jax.nn.sigmoid(g_a))).astype(BF16)
    a_proj = jnp.dot(a_out, wb0_ref[...], preferred_element_type=F32)
    l_proj = jnp.dot(l_out, wb1_ref[...], preferred_element_type=F32)
    gates = jax.nn.sigmoid(jnp.dot(h, wm_ref[...], preferred_element_type=F32) + bm_ref[...])
    merged = gates[:, :d] * a_proj + gates[:, d:] * l_proj
    y = x_ref[0] + jnp.dot(merged.astype(BF16), wo_ref[...], preferred_element_type=F32)
    if final_norm:
        y = y * _rms_scale(y) * gfin_ref[...]
    y_ref[0] = y


def _lru_specs(s, t, d, d_lru, reverse):
    nb = s // t
    groups = t // SUBLANES
    last_group = s // SUBLANES - 1

    def blk(i):
        return nb - 1 - i if reverse else i

    x_spec = pl.BlockSpec((1, t, d), lambda bi, i: (bi, blk(i), 0))
    xp_spec = pl.BlockSpec((1, HALO, d), lambda bi, i: (bi, jnp.maximum(blk(i) * groups - 1, 0), 0))
    xn_spec = pl.BlockSpec((1, HALO, d), lambda bi, i: (bi, jnp.minimum((blk(i) + 1) * groups, last_group), 0))
    hf_spec = pl.BlockSpec((1, groups, SUBLANES, d_lru), lambda bi, i: (bi, blk(i), 0, 0))
    scratch = [
        pltpu.VMEM((t + 2 * HALO, d_lru), F32),
        pltpu.VMEM((groups, SUBLANES, d_lru), F32),
        pltpu.VMEM((groups, SUBLANES, d_lru), F32),
        pltpu.VMEM((SUBLANES, d_lru), F32),
    ]
    return x_spec, xp_spec, xn_spec, hf_spec, scratch, blk


def _lru_f_call(x, g_in, w_xl, conv_w, conv_b, w_gate, b_gate, cneg):
    b, s, d = x.shape
    d_lru = w_xl.shape[1]
    t = min(LRU_ROWS, s)
    x_spec, xp_spec, xn_spec, hf_spec, scratch, _ = _lru_specs(s, t, d, d_lru, False)
    return pl.pallas_call(
        functools.partial(_lru_f_kernel, t=t),
        grid=(b, s // t),
        in_specs=[x_spec, xp_spec, xn_spec, _const_spec((1, d)), _const_spec(w_xl.shape),
                  _const_spec(conv_w.shape), _const_spec(conv_b.shape), _const_spec(w_gate.shape),
                  _const_spec(b_gate.shape), _const_spec(cneg.shape)],
        out_specs=hf_spec,
        out_shape=jax.ShapeDtypeStruct((b, s // SUBLANES, SUBLANES, d_lru), F32),
        scratch_shapes=scratch,
        compiler_params=_params(2, True),
        name="lru_f",
    )(x, x, x, g_in, w_xl, conv_w, conv_b, w_gate, b_gate, cneg)


def _merge_call(x, g_in, w_xl, conv_w, conv_b, w_gate, b_gate, cneg, hf, att,
                w_gl, w_ga, w_m, b_m, w_b0, w_b1, w_o, g_fin, final_norm):
    b, s, d = x.shape
    d_lru = w_xl.shape[1]
    t = min(LRU_ROWS, s)
    x_spec, xp_spec, xn_spec, hf_spec, scratch, blk = _lru_specs(s, t, d, d_lru, True)
    tok = lambda width: pl.BlockSpec((1, t, width), lambda bi, i: (bi, blk(i), 0))
    consts = [g_in, w_xl, conv_w, conv_b, w_gate, b_gate, cneg]
    tail = [w_gl, w_ga, w_m, b_m, w_b0, w_b1, w_o, g_fin]
    return pl.pallas_call(
        functools.partial(_merge_kernel, t=t, final_norm=final_norm),
        grid=(b, s // t),
        in_specs=[x_spec, xp_spec, xn_spec] + [_const_spec(c.shape) for c in consts]
        + [hf_spec, tok(att.shape[2])] + [_const_spec(c.shape) for c in tail],
        out_specs=tok(d),
        out_shape=jax.ShapeDtypeStruct((b, s, d), F32),
        scratch_shapes=scratch + [pltpu.VMEM((t // SUBLANES, SUBLANES, d_lru), F32)],
        compiler_params=_params(2, True),
        name="merge",
    )(x, x, x, *consts, hf, att, *tail)


def _rope_tables(s, gain, scale):
    rows_n = s // GRID_W
    rows = jnp.repeat(jnp.arange(rows_n, dtype=F32), GRID_W)
    cols = jnp.tile(jnp.arange(GRID_W, dtype=F32), rows_n)
    n_pair_axis = HEAD_DIM // 4
    inv_freq = ROPE_THETA ** (-jnp.arange(n_pair_axis, dtype=F32) / n_pair_axis)
    ang = jnp.concatenate([rows[:, None] * inv_freq, cols[:, None] * inv_freq], axis=-1)
    cos = jnp.cos(ang).T
    sin = jnp.sin(ang).T
    c = jnp.concatenate([cos, cos], axis=0)
    sn = jnp.concatenate([-sin, sin], axis=0)
    gain = gain.astype(F32)
    return (gain[:, None] * c) * scale, (jnp.roll(gain, HALF)[:, None] * sn) * scale


def _layer_params(w_in, q_norm, k_norm, w_rgate, b_rgate, w_igate, b_igate, lam, w_branch, w_out, d):
    d_attn = N_HEADS * HEAD_DIM
    d_kv = N_KV_HEADS * HEAD_DIM
    d_lru = N_LRU_BLOCKS * LRU_BLOCK
    splits = [d_attn, d_kv, d_kv, d_attn, d_lru, d_lru]
    offs = [0]
    for w in splits:
        offs.append(offs[-1] + w)
    w_q, w_k, w_v, w_ga, w_xl, w_gl = (w_in[:, offs[j]:offs[j + 1]] for j in range(6))
    w_m = w_in[:, offs[6]:]
    perm = jnp.concatenate([jnp.arange(0, HEAD_DIM, 2), jnp.arange(1, HEAD_DIM, 2)])
    permute = lambda w, nh: w.reshape(d, nh, HEAD_DIM)[:, :, perm].reshape(d, nh * HEAD_DIM)
    w_qkvT = jnp.concatenate([permute(w_q, N_HEADS), permute(w_k, N_KV_HEADS), w_v], axis=1).T.astype(BF16)
    w_gate = jnp.concatenate([w_rgate, w_igate], axis=-1).astype(BF16)
    b_gate = jnp.concatenate([b_rgate.reshape(2, N_LRU_BLOCKS, 1, LRU_BLOCK),
                              b_igate.reshape(2, N_LRU_BLOCKS, 1, LRU_BLOCK)], axis=-1).astype(F32)
    cneg = (-RG_C * jax.nn.softplus(-lam.astype(F32))).reshape(2, 1, d_lru)
    return dict(
        w_qkvT=w_qkvT, gq=q_norm[perm], gk=k_norm[perm],
        w_ga=w_ga.astype(BF16), w_xl=w_xl.astype(BF16), w_gl=w_gl.astype(BF16), w_m=w_m.astype(BF16),
        w_gate=w_gate, b_gate=b_gate, cneg=cneg,
        w_b0=w_branch[0].astype(BF16), w_b1=w_branch[1].astype(BF16), w_o=w_out.astype(BF16),
    )


def _layer(x, p, g_in, conv_w, conv_b, b_m, g_fin, final_norm):
    s = x.shape[1]
    q_scale = math.log2(math.e) / math.sqrt(HEAD_DIM)
    cq, sq = _rope_tables(s, p["gq"], q_scale)
    ck, sk = _rope_tables(s, p["gk"], 1.0)
    qT, k, vT = _qkv_call(x, g_in, p["w_qkvT"], cq, sq, ck, sk)
    att = _attn_call(qT, k, vT)
    lru = (g_in, p["w_xl"], conv_w, conv_b)
    hf = _lru_f_call(x, *lru, p["w_gate"][0], p["b_gate"][0], p["cneg"][0])
    return _merge_call(x, *lru, p["w_gate"][1], p["b_gate"][1], p["cneg"][1], hf, att,
                       p["w_gl"], p["w_ga"], p["w_m"], b_m, p["w_b0"], p["w_b1"], p["w_o"], g_fin, final_norm)


def kernel(x_prompt, x_sample, norm_in, w_in, b_merge, q_norm, k_norm, conv_w, conv_b, w_rgate, b_rgate,
           w_igate, b_igate, lam, w_branch, w_out, norm_final):
    depth, d = norm_in.shape
    layers = [
        _layer_params(w_in[l], q_norm[l], k_norm[l], w_rgate[l], b_rgate[l], w_igate[l], b_igate[l], lam[l],
                      w_branch[l], w_out[l], d)
        for l in range(depth)
    ]
    g_fin = norm_final.reshape(1, d).astype(F32)
    outs = []
    for x in (x_prompt, x_sample):
        for l in range(depth):
            x = _layer(x, layers[l], norm_in[l].reshape(1, d).astype(F32), conv_w[l].astype(F32),
                       conv_b[l].reshape(1, -1).astype(F32), b_merge[l].reshape(1, -1).astype(F32), g_fin,
                       l == depth - 1)
        outs.append(x)
    return tuple(outs)
```

```python
import functools
import math

import jax
import jax.numpy as jnp
from jax import lax
from jax.experimental import pallas as pl
from jax.experimental.pallas import tpu as pltpu

N_HEADS = 8
N_KV_HEADS = 2
GROUP = N_HEADS // N_KV_HEADS
HEAD_DIM = 128
HALF = HEAD_DIM // 2
N_LRU_BLOCKS = 8
LRU_BLOCK = 128
CONV_W = 4
RG_C = 8.0
EPS = 1e-6
GRID_W = 64
ROPE_THETA = 10000.0

SUBLANES = 8
ONES_ROWS = 2 * SUBLANES
HALO = SUBLANES
V7X_VMEM_LIMIT = 56 * 1024 * 1024

QKV_ROWS = 512
ATTN_Q = 512
ATTN_K = 256
LRU_ROWS = 256

F32 = jnp.float32
BF16 = jnp.bfloat16


def _const_spec(shape):
    nd = len(shape)
    return pl.BlockSpec(shape, lambda *_: (0,) * nd, pipeline_mode=pl.Buffered(1))


def _params(n_axes, sequential_last):
    sem = ["parallel"] * n_axes
    if sequential_last:
        sem[-1] = "arbitrary"
    return pltpu.CompilerParams(dimension_semantics=tuple(sem), vmem_limit_bytes=V7X_VMEM_LIMIT)


def _rms_scale(x):
    return lax.rsqrt(jnp.mean(x * x, axis=-1, keepdims=True) + EPS)


def _qkv_kernel(x_ref, g_ref, w_ref, cq_ref, sq_ref, ck_ref, sk_ref, qT_ref, k_ref, vT_ref):
    x = x_ref[0]
    h = (x * _rms_scale(x) * g_ref[...]).astype(BF16)
    zT = lax.dot_general(w_ref[...], h, (((1,), (1,)), ((), ())), preferred_element_type=F32)

    def norm_rope(z, c, s):
        n = z * lax.rsqrt(jnp.mean(z * z, axis=0, keepdims=True) + EPS)
        swapped = jnp.concatenate([n[HALF:], n[:HALF]], axis=0)
        return n * c + swapped * s

    cq, sq = cq_ref[...], sq_ref[...]
    for hd in range(N_HEADS):
        rows = slice(hd * HEAD_DIM, (hd + 1) * HEAD_DIM)
        qh = norm_rope(zT[rows], cq, sq).astype(BF16)
        tq = qT_ref.shape[3]
        for qb in range(qT_ref.shape[1]):
            qT_ref[0, qb, rows, :] = qh[:, qb * tq:(qb + 1) * tq]
    ck, sk = ck_ref[...], sk_ref[...]
    k_off = N_HEADS * HEAD_DIM
    for kv in range(N_KV_HEADS):
        rows = slice(k_off + kv * HEAD_DIM, k_off + (kv + 1) * HEAD_DIM)
        kT = norm_rope(zT[rows], ck, sk)
        k_ref[0, :, kv * HEAD_DIM:(kv + 1) * HEAD_DIM] = kT.T.astype(BF16)
    v_off = k_off + N_KV_HEADS * HEAD_DIM
    vT_ref[0] = zT[v_off:v_off + N_KV_HEADS * HEAD_DIM].astype(BF16)


def _qkv_call(x, g_in, w_qkvT, cq, sq, ck, sk):
    b, s, d = x.shape
    t = min(QKV_ROWS, s)
    tq = min(ATTN_Q, t)
    n_q, n_kv = N_HEADS * HEAD_DIM, N_KV_HEADS * HEAD_DIM
    tab = pl.BlockSpec((HEAD_DIM, t), lambda bi, i: (0, i))
    return pl.pallas_call(
        _qkv_kernel,
        grid=(b, s // t),
        in_specs=[
            pl.BlockSpec((1, t, d), lambda bi, i: (bi, i, 0)),
            _const_spec((1, d)),
            _const_spec(w_qkvT.shape),
            tab, tab, tab, tab,
        ],
        out_specs=[
            pl.BlockSpec((1, t // tq, n_q, tq), lambda bi, i: (bi, i, 0, 0)),
            pl.BlockSpec((1, t, n_kv), lambda bi, i: (bi, i, 0)),
            pl.BlockSpec((1, n_kv, t), lambda bi, i: (bi, 0, i)),
        ],
        out_shape=[
            jax.ShapeDtypeStruct((b, s // tq, n_q, tq), BF16),
            jax.ShapeDtypeStruct((b, s, n_kv), BF16),
            jax.ShapeDtypeStruct((b, n_kv, s), BF16),
        ],
        compiler_params=_params(2, False),
        name="qkv",
    )(x, g_in, w_qkvT, cq, sq, ck, sk)


def _attn_kernel(qT_ref, k_ref, vT_ref, o_ref, m_ref, alpha_ref, acc_ref, s_scr, cmax_ref, p_scr, *, n_chunks, tk):
    m_ref[...] = jnp.full(m_ref.shape, -jnp.inf, F32)
    acc_ref[...] = jnp.zeros(acc_ref.shape, F32)
    ones_rows = jnp.ones((ONES_ROWS, tk), BF16)
    last = N_HEADS - 1

    def kv_cols(hd):
        kv = hd // GROUP
        return slice(kv * HEAD_DIM, (kv + 1) * HEAD_DIM)

    def scores(c, hd):
        off = pl.multiple_of(c * tk, tk)
        kc = k_ref[0, pl.ds(off, tk), kv_cols(hd)]
        qT = qT_ref[0, 0, hd * HEAD_DIM:(hd + 1) * HEAD_DIM, :]
        s = jnp.dot(kc, qT, preferred_element_type=F32)
        s_scr[hd] = s
        cmax_ref[hd] = jnp.max(s, axis=0, keepdims=True)

    def softmax(hd):
        m_old = m_ref[hd:hd + 1, :]
        m_new = jnp.maximum(m_old, cmax_ref[hd])
        alpha_ref[hd:hd + 1, :] = jnp.exp2(m_old - m_new)
        m_ref[hd:hd + 1, :] = m_new
        p_scr[hd] = jnp.exp2(s_scr[hd] - m_new).astype(BF16)

    def update(c, hd):
        off = pl.multiple_of(c * tk, tk)
        vc = jnp.concatenate([vT_ref[0, kv_cols(hd), pl.ds(off, tk)], ones_rows], axis=0)
        pv = jnp.dot(vc, p_scr[hd], preferred_element_type=F32)
        acc_ref[hd] = alpha_ref[hd:hd + 1, :] * acc_ref[hd] + pv

    p_scr[last] = jnp.zeros(p_scr.shape[1:], BF16)
    alpha_ref[last:last + 1, :] = jnp.ones((1, alpha_ref.shape[1]), F32)
    scores(0, 0)

    def step(c, carry):
        for hd in range(N_HEADS):
            if hd < last:
                scores(c, hd + 1)
            else:
                scores(jnp.minimum(c + 1, n_chunks - 1), 0)
            softmax(hd)
            if hd > 0:
                update(c, hd - 1)
            else:
                update(jnp.maximum(c - 1, 0), last)
        return carry

    lax.fori_loop(0, n_chunks, step, 0, unroll=4)
    update(n_chunks - 1, last)

    for hd in range(N_HEADS):
        o = acc_ref[hd, :HEAD_DIM, :] * (1.0 / acc_ref[hd, HEAD_DIM:HEAD_DIM + 1, :])
        o_ref[0, :, hd * HEAD_DIM:(hd + 1) * HEAD_DIM] = o.T.astype(BF16)


def _attn_call(qT, k, vT):
    b, n_qb, n_q, tq = qT.shape
    s, n_kv = k.shape[1:]
    tk = min(ATTN_K, s)
    kernel = functools.partial(_attn_kernel, n_chunks=s // tk, tk=tk)
    return pl.pallas_call(
        kernel,
        grid=(b, n_qb),
        in_specs=[
            pl.BlockSpec((1, 1, n_q, tq), lambda bi, i: (bi, i, 0, 0)),
            pl.BlockSpec((1, s, n_kv), lambda bi, i: (bi, 0, 0), pipeline_mode=pl.Buffered(1)),
            pl.BlockSpec((1, n_kv, s), lambda bi, i: (bi, 0, 0), pipeline_mode=pl.Buffered(1)),
        ],
        out_specs=pl.BlockSpec((1, tq, n_q), lambda bi, i: (bi, i, 0)),
        out_shape=jax.ShapeDtypeStruct((b, s, n_q), BF16),
        scratch_shapes=[
            pltpu.VMEM((N_HEADS, tq), F32),
            pltpu.VMEM((N_HEADS, tq), F32),
            pltpu.VMEM((N_HEADS, HEAD_DIM + ONES_ROWS, tq), F32),
            pltpu.VMEM((N_HEADS, tk, tq), F32),
            pltpu.VMEM((N_HEADS, 1, tq), F32),
            pltpu.VMEM((N_HEADS, tk, tq), BF16),
        ],
        compiler_params=_params(2, True),
        name="attn",
    )(qT, k, vT)


def _lru_sweep(x_ref, xp_ref, xn_ref, g_ref, wxl_ref, cw_ref, cb_ref, wg_ref, bg_ref, cneg_ref,
               xl_scr, a_scr, u_scr, carry_ref, write_row, *, reverse, t):
    i = pl.program_id(1)
    nb = pl.num_programs(1)
    blk = nb - 1 - i if reverse else i
    d_lru = wxl_ref.shape[1]
    groups = t // SUBLANES

    @pl.when(i == 0)
    def _():
        carry_ref[...] = jnp.zeros(carry_ref.shape, F32)

    xp = xp_ref[0] * jnp.where(blk > 0, 1.0, 0.0)
    xn = xn_ref[0] * jnp.where(blk < nb - 1, 1.0, 0.0)
    xe = jnp.concatenate([xp, x_ref[0], xn], axis=0)
    he = (xe * _rms_scale(xe) * g_ref[...]).astype(BF16)
    xl_scr[...] = jnp.dot(he, wxl_ref[...], preferred_element_type=F32)

    left = CONV_W // 2
    xc = cb_ref[...] + cw_ref[0:1, :] * xl_scr[pl.ds(HALO - left, t), :]
    for j in range(1, CONV_W):
        xc = xc + cw_ref[j:j + 1, :] * xl_scr[pl.ds(HALO - left + j, t), :]

    sub = lax.broadcasted_iota(jnp.int32, (1, SUBLANES, LRU_BLOCK), 1)
    for n in range(N_LRU_BLOCKS):
        cols = slice(n * LRU_BLOCK, (n + 1) * LRU_BLOCK)
        xcn = xc[:, cols]
        gz = jnp.dot(xcn.astype(BF16), wg_ref[n], preferred_element_type=F32) + bg_ref[n]
        r = jax.nn.sigmoid(gz[:, :LRU_BLOCK])
        ig = jax.nn.sigmoid(gz[:, LRU_BLOCK:])
        log_a = cneg_ref[:, cols] * r
        a = jnp.exp(log_a)
        u = jnp.sqrt((1.0 - a) * (1.0 + a)) * (ig * xcn)
        a3 = a.reshape(groups, SUBLANES, LRU_BLOCK)
        u3 = u.reshape(groups, SUBLANES, LRU_BLOCK)
        for dist in (1, 2, 4):
            if reverse:
                shift, ok = SUBLANES - dist, sub < SUBLANES - dist
            else:
                shift, ok = dist, sub >= dist
            a_nb = pltpu.roll(a3, shift, axis=1)
            u_nb = pltpu.roll(u3, shift, axis=1)
            u3 = jnp.where(ok, a3 * u_nb + u3, u3)
            a3 = jnp.where(ok, a3 * a_nb, a3)
        a_scr[:, :, cols] = a3
        u_scr[:, :, cols] = u3

    edge = 0 if reverse else SUBLANES - 1

    def step(j, hrow):
        jj = groups - 1 - j if reverse else j
        rows = u_scr[jj] + a_scr[jj] * hrow
        write_row(jj, rows)
        return jnp.broadcast_to(rows[edge:edge + 1, :], (SUBLANES, d_lru))

    carry_ref[...] = lax.fori_loop(0, groups, step, carry_ref[...], unroll=4)
    return he[HALO:HALO + t]


def _lru_f_kernel(x_ref, xp_ref, xn_ref, g_ref, wxl_ref, cw_ref, cb_ref, wg_ref, bg_ref, cneg_ref,
                  hf_ref, xl_scr, a_scr, u_scr, carry_ref, *, t):
    def write_row(j, rows):
        hf_ref[0, j] = rows

    _lru_sweep(x_ref, xp_ref, xn_ref, g_ref, wxl_ref, cw_ref, cb_ref, wg_ref, bg_ref, cneg_ref,
               xl_scr, a_scr, u_scr, carry_ref, write_row, reverse=False, t=t)


def _merge_kernel(x_ref, xp_ref, xn_ref, g_ref, wxl_ref, cw_ref, cb_ref, wg_ref, bg_ref, cneg_ref,
                  hf_ref, att_ref, wgl_ref, wga_ref, wm_ref, bm_ref, wb0_ref, wb1_ref, wo_ref, gfin_ref,
                  y_ref, xl_scr, a_scr, u_scr, carry_ref, hb_scr, *, t, final_norm):
    def write_row(j, rows):
        hb_scr[j] = rows

    h = _lru_sweep(x_ref, xp_ref, xn_ref, g_ref, wxl_ref, cw_ref, cb_ref, wg_ref, bg_ref, cneg_ref,
                   xl_scr, a_scr, u_scr, carry_ref, write_row, reverse=True, t=t)
    d = x_ref.shape[2]
    lsum = (hf_ref[0] + hb_scr[...]).reshape(t, wxl_ref.shape[1])
    g_l = jnp.dot(h, wgl_ref[...], preferred_element_type=F32)
    l_out = (lsum * (g_l * jax.nn.sigmoid(g_l))).astype(BF16)
    g_a = jnp.dot(h, wga_ref[...], preferred_element_type=F32)
    a_out = (att_ref[0].astype(F32) * (g_a * jax.nn.sigmoid(g_a))).astype(BF16)
    a_proj = jnp.dot(a_out, wb0_ref[...], preferred_element_type=F32)
    l_proj = jnp.dot(l_out, wb1_ref[...], preferred_element_type=F32)
    gates = jax.nn.sigmoid(jnp.dot(h, wm_ref[...], preferred_element_type=F32) + bm_ref[...])
    merged = gates[:, :d] * a_proj + gates[:, d:] * l_proj
    y = x_ref[0] + jnp.dot(merged.astype(BF16), wo_ref[...], preferred_element_type=F32)
    if final_norm:
        y = y * _rms_scale(y) * gfin_ref[...]
    y_ref[0] = y


def _lru_specs(s, t, d, d_lru, reverse):
    nb = s // t
    groups = t // SUBLANES
    last_group = s // SUBLANES - 1

    def blk(i):
        return nb - 1 - i if reverse else i

    x_spec = pl.BlockSpec((1, t, d), lambda bi, i: (bi, blk(i), 0))
    xp_spec = pl.BlockSpec((1, HALO, d), lambda bi, i: (bi, jnp.maximum(blk(i) * groups - 1, 0), 0))
    xn_spec = pl.BlockSpec((1, HALO, d), lambda bi, i: (bi, jnp.minimum((blk(i) + 1) * groups, last_group), 0))
    hf_spec = pl.BlockSpec((1, groups, SUBLANES, d_lru), lambda bi, i: (bi, blk(i), 0, 0))
    scratch = [
        pltpu.VMEM((t + 2 * HALO, d_lru), F32),
        pltpu.VMEM((groups, SUBLANES, d_lru), F32),
        pltpu.VMEM((groups, SUBLANES, d_lru), F32),
        pltpu.VMEM((SUBLANES, d_lru), F32),
    ]
    return x_spec, xp_spec, xn_spec, hf_spec, scratch, blk


def _lru_f_call(x, g_in, w_xl, conv_w, conv_b, w_gate, b_gate, cneg):
    b, s, d = x.shape
    d_lru = w_xl.shape[1]
    t = min(LRU_ROWS, s)
    x_spec, xp_spec, xn_spec, hf_spec, scratch, _ = _lru_specs(s, t, d, d_lru, False)
    return pl.pallas_call(
        functools.partial(_lru_f_kernel, t=t),
        grid=(b, s // t),
        in_specs=[x_spec, xp_spec, xn_spec, _const_spec((1, d)), _const_spec(w_xl.shape),
                  _const_spec(conv_w.shape), _const_spec(conv_b.shape), _const_spec(w_gate.shape),
                  _const_spec(b_gate.shape), _const_spec(cneg.shape)],
        out_specs=hf_spec,
        out_shape=jax.ShapeDtypeStruct((b, s // SUBLANES, SUBLANES, d_lru), F32),
        scratch_shapes=scratch,
        compiler_params=_params(2, True),
        name="lru_f",
    )(x, x, x, g_in, w_xl, conv_w, conv_b, w_gate, b_gate, cneg)


def _merge_call(x, g_in, w_xl, conv_w, conv_b, w_gate, b_gate, cneg, hf, att,
                w_gl, w_ga, w_m, b_m, w_b0, w_b1, w_o, g_fin, final_norm):
    b, s, d = x.shape
    d_lru = w_xl.shape[1]
    t = min(LRU_ROWS, s)
    x_spec, xp_spec, xn_spec, hf_spec, scratch, blk = _lru_specs(s, t, d, d_lru, True)
    tok = lambda width: pl.BlockSpec((1, t, width), lambda bi, i: (bi, blk(i), 0))
    consts = [g_in, w_xl, conv_w, conv_b, w_gate, b_gate, cneg]
    tail = [w_gl, w_ga, w_m, b_m, w_b0, w_b1, w_o, g_fin]
    return pl.pallas_call(
        functools.partial(_merge_kernel, t=t, final_norm=final_norm),
        grid=(b, s // t),
        in_specs=[x_spec, xp_spec, xn_spec] + [_const_spec(c.shape) for c in consts]
        + [hf_spec, tok(att.shape[2])] + [_const_spec(c.shape) for c in tail],
        out_specs=tok(d),
        out_shape=jax.ShapeDtypeStruct((b, s, d), F32),
        scratch_shapes=scratch + [pltpu.VMEM((t // SUBLANES, SUBLANES, d_lru), F32)],
        compiler_params=_params(2, True),
        name="merge",
    )(x, x, x, *consts, hf, att, *tail)


def _rope_tables(s, gain, scale):
    rows_n = s // GRID_W
    rows = jnp.repeat(jnp.arange(rows_n, dtype=F32), GRID_W)
    cols = jnp.tile(jnp.arange(GRID_W, dtype=F32), rows_n)
    n_pair_axis = HEAD_DIM // 4
    inv_freq = ROPE_THETA ** (-jnp.arange(n_pair_axis, dtype=F32) / n_pair_axis)
    ang = jnp.concatenate([rows[:, None] * inv_freq, cols[:, None] * inv_freq], axis=-1)
    cos = jnp.cos(ang).T
    sin = jnp.sin(ang).T
    c = jnp.concatenate([cos, cos], axis=0)
    sn = jnp.concatenate([-sin, sin], axis=0)
    gain = gain.astype(F32)
    return (gain[:, None] * c) * scale, (jnp.roll(gain, HALF)[:, None] * sn) * scale


def _layer_params(w_in, q_norm, k_norm, w_rgate, b_rgate, w_igate, b_igate, lam, w_branch, w_out, d):
    d_attn = N_HEADS * HEAD_DIM
    d_kv = N_KV_HEADS * HEAD_DIM
    d_lru = N_LRU_BLOCKS * LRU_BLOCK
    splits = [d_attn, d_kv, d_kv, d_attn, d_lru, d_lru]
    offs = [0]
    for w in splits:
        offs.append(offs[-1] + w)
    w_q, w_k, w_v, w_ga, w_xl, w_gl = (w_in[:, offs[j]:offs[j + 1]] for j in range(6))
    w_m = w_in[:, offs[6]:]
    perm = jnp.concatenate([jnp.arange(0, HEAD_DIM, 2), jnp.arange(1, HEAD_DIM, 2)])
    permute = lambda w, nh: w.reshape(d, nh, HEAD_DIM)[:, :, perm].reshape(d, nh * HEAD_DIM)
    w_qkvT = jnp.concatenate([permute(w_q, N_HEADS), permute(w_k, N_KV_HEADS), w_v], axis=1).T.astype(BF16)
    w_gate = jnp.concatenate([w_rgate, w_igate], axis=-1).astype(BF16)
    b_gate = jnp.concatenate([b_rgate.reshape(2, N_LRU_BLOCKS, 1, LRU_BLOCK),
                              b_igate.reshape(2, N_LRU_BLOCKS, 1, LRU_BLOCK)], axis=-1).astype(F32)
    cneg = (-RG_C * jax.nn.softplus(-lam.astype(F32))).reshape(2, 1, d_lru)
    return dict(
        w_qkvT=w_qkvT, gq=q_norm[perm], gk=k_norm[perm],
        w_ga=w_ga.astype(BF16), w_xl=w_xl.astype(BF16), w_gl=w_gl.astype(BF16), w_m=w_m.astype(BF16),
        w_gate=w_gate, b_gate=b_gate, cneg=cneg,
        w_b0=w_branch[0].astype(BF16), w_b1=w_branch[1].astype(BF16), w_o=w_out.astype(BF16),
    )


def _layer(x, p, g_in, conv_w, conv_b, b_m, g_fin, final_norm):
    s = x.shape[1]
    q_scale = math.log2(math.e) / math.sqrt(HEAD_DIM)
    cq, sq = _rope_tables(s, p["gq"], q_scale)
    ck, sk = _rope_tables(s, p["gk"], 1.0)
    qT, k, vT = _qkv_call(x, g_in, p["w_qkvT"], cq, sq, ck, sk)
    att = _attn_call(qT, k, vT)
    lru = (g_in, p["w_xl"], conv_w, conv_b)
    hf = _lru_f_call(x, *lru, p["w_gate"][0], p["b_gate"][0], p["cneg"][0])
    return _merge_call(x, *lru, p["w_gate"][1], p["b_gate"][1], p["cneg"][1], hf, att,
                       p["w_gl"], p["w_ga"], p["w_m"], b_m, p["w_b0"], p["w_b1"], p["w_o"], g_fin, final_norm)


def kernel(x_prompt, x_sample, norm_in, w_in, b_merge, q_norm, k_norm, conv_w, conv_b, w_rgate, b_rgate,
           w_igate, b_igate, lam, w_branch, w_out, norm_final):
    depth, d = norm_in.shape
    layers = [
        _layer_params(w_in[l], q_norm[l], k_norm[l], w_rgate[l], b_rgate[l], w_igate[l], b_igate[l], lam[l],
                      w_branch[l], w_out[l], d)
        for l in range(depth)
    ]
    g_fin = norm_final.reshape(1, d).astype(F32)
    outs = []
    for x in (x_prompt, x_sample):
        for l in range(depth):
            x = _layer(x, layers[l], norm_in[l].reshape(1, d).astype(F32), conv_w[l].astype(F32),
                       conv_b[l].reshape(1, -1).astype(F32), b_merge[l].reshape(1, -1).astype(F32), g_fin,
                       l == depth - 1)
        outs.append(x)
    return tuple(outs)
```

```python
import functools
import math

import jax
import jax.numpy as jnp
from jax import lax
from jax.experimental import pallas as pl
from jax.experimental.pallas import tpu as pltpu

N_HEADS = 8
N_KV_HEADS = 2
GROUP = N_HEADS // N_KV_HEADS
HEAD_DIM = 128
HALF = HEAD_DIM // 2
N_LRU_BLOCKS = 8
LRU_BLOCK = 128
CONV_W = 4
RG_C = 8.0
EPS = 1e-6
TINY = 1e-30
GRID_W = 64
ROPE_THETA = 10000.0

SUBLANES = 8
ONES_ROWS = 2 * SUBLANES
HALO = SUBLANES
V7X_VMEM_LIMIT = 56 * 1024 * 1024

QKV_ROWS = 512
ATTN_Q = 512
ATTN_K = 256
LRU_ROWS = 256

F32 = jnp.float32
BF16 = jnp.bfloat16


def _const_spec(shape):
    nd = len(shape)
    return pl.BlockSpec(shape, lambda *_: (0,) * nd, pipeline_mode=pl.Buffered(1))


def _params(n_axes, sequential_last, flags=None):
    sem = ["parallel"] * n_axes
    if sequential_last:
        sem[-1] = "arbitrary"
    return pltpu.CompilerParams(dimension_semantics=tuple(sem), vmem_limit_bytes=V7X_VMEM_LIMIT, flags=flags)


def _rms_scale(x):
    return lax.rsqrt(jnp.mean(x * x, axis=-1, keepdims=True) + EPS)


def _qkv_kernel(x_ref, g_ref, w_ref, cq_ref, sq_ref, ck_ref, sk_ref, qT_ref, k_ref, vT_ref):
    x = x_ref[0]
    h = (x * _rms_scale(x) * g_ref[...]).astype(BF16)
    zT = lax.dot_general(w_ref[...], h, (((1,), (1,)), ((), ())), preferred_element_type=F32)

    def norm_rope(z, c, s):
        n = z * lax.rsqrt(jnp.mean(z * z, axis=0, keepdims=True) + EPS)
        swapped = jnp.concatenate([n[HALF:], n[:HALF]], axis=0)
        return n * c + swapped * s

    cq, sq = cq_ref[...], sq_ref[...]
    for hd in range(N_HEADS):
        rows = slice(hd * HEAD_DIM, (hd + 1) * HEAD_DIM)
        qh = norm_rope(zT[rows], cq, sq).astype(BF16)
        tq = qT_ref.shape[3]
        for qb in range(qT_ref.shape[1]):
            qT_ref[0, qb, rows, :] = qh[:, qb * tq:(qb + 1) * tq]
    ck, sk = ck_ref[...], sk_ref[...]
    k_off = N_HEADS * HEAD_DIM
    for kv in range(N_KV_HEADS):
        rows = slice(k_off + kv * HEAD_DIM, k_off + (kv + 1) * HEAD_DIM)
        kT = norm_rope(zT[rows], ck, sk)
        k_ref[0, :, kv * HEAD_DIM:(kv + 1) * HEAD_DIM] = kT.T.astype(BF16)
    v_off = k_off + N_KV_HEADS * HEAD_DIM
    vT_ref[0] = zT[v_off:v_off + N_KV_HEADS * HEAD_DIM].astype(BF16)


def _qkv_call(x, g_in, w_qkvT, cq, sq, ck, sk):
    b, s, d = x.shape
    t = min(QKV_ROWS, s)
    tq = min(ATTN_Q, t)
    n_q, n_kv = N_HEADS * HEAD_DIM, N_KV_HEADS * HEAD_DIM
    tab = pl.BlockSpec((HEAD_DIM, t), lambda bi, i: (0, i))
    return pl.pallas_call(
        _qkv_kernel,
        grid=(b, s // t),
        in_specs=[
            pl.BlockSpec((1, t, d), lambda bi, i: (bi, i, 0)),
            _const_spec((1, d)),
            _const_spec(w_qkvT.shape),
            tab, tab, tab, tab,
        ],
        out_specs=[
            pl.BlockSpec((1, t // tq, n_q, tq), lambda bi, i: (bi, i, 0, 0)),
            pl.BlockSpec((1, t, n_kv), lambda bi, i: (bi, i, 0)),
            pl.BlockSpec((1, n_kv, t), lambda bi, i: (bi, 0, i)),
        ],
        out_shape=[
            jax.ShapeDtypeStruct((b, s // tq, n_q, tq), BF16),
            jax.ShapeDtypeStruct((b, s, n_kv), BF16),
            jax.ShapeDtypeStruct((b, n_kv, s), BF16),
        ],
        compiler_params=_params(2, False),
        name="qkv",
    )(x, g_in, w_qkvT, cq, sq, ck, sk)


def _attn_kernel(qT_ref, k_ref, vT_ref, o_ref, m_ref, alpha_ref, acc_ref, s_scr, cmax_ref, p_scr, *, n_chunks, tk):
    m_ref[...] = jnp.full(m_ref.shape, -jnp.inf, F32)
    acc_ref[...] = jnp.zeros(acc_ref.shape, F32)
    ones_rows = jnp.ones((ONES_ROWS, tk), BF16)
    last = N_HEADS - 1

    def kv_cols(hd):
        kv = hd // GROUP
        return slice(kv * HEAD_DIM, (kv + 1) * HEAD_DIM)

    def scores(c, hd):
        off = pl.multiple_of(c * tk, tk)
        kc = k_ref[0, pl.ds(off, tk), kv_cols(hd)]
        qT = qT_ref[0, 0, hd * HEAD_DIM:(hd + 1) * HEAD_DIM, :]
        s = jnp.dot(kc, qT, preferred_element_type=F32)
        s_scr[hd] = s
        cmax_ref[hd] = jnp.max(s, axis=0, keepdims=True)

    def softmax(hd):
        m_old = m_ref[hd:hd + 1, :]
        m_new = jnp.maximum(m_old, cmax_ref[hd])
        alpha_ref[hd:hd + 1, :] = jnp.exp2(m_old - m_new)
        m_ref[hd:hd + 1, :] = m_new
        p_scr[hd] = jnp.exp2(s_scr[hd] - m_new).astype(BF16)

    def update(c, hd):
        off = pl.multiple_of(c * tk, tk)
        vc = jnp.concatenate([vT_ref[0, kv_cols(hd), pl.ds(off, tk)], ones_rows], axis=0)
        pv = jnp.dot(vc, p_scr[hd], preferred_element_type=F32)
        acc_ref[hd] = alpha_ref[hd:hd + 1, :] * acc_ref[hd] + pv

    p_scr[last] = jnp.zeros(p_scr.shape[1:], BF16)
    alpha_ref[last:last + 1, :] = jnp.ones((1, alpha_ref.shape[1]), F32)
    scores(0, 0)

    def step(c, carry):
        for hd in range(N_HEADS):
            if hd < last:
                scores(c, hd + 1)
            else:
                scores(jnp.minimum(c + 1, n_chunks - 1), 0)
            softmax(hd)
            if hd > 0:
                update(c, hd - 1)
            else:
                update(jnp.maximum(c - 1, 0), last)
        return carry

    lax.fori_loop(0, n_chunks, step, 0, unroll=4)
    update(n_chunks - 1, last)

    for hd in range(N_HEADS):
        o = acc_ref[hd, :HEAD_DIM, :] * (1.0 / acc_ref[hd, HEAD_DIM:HEAD_DIM + 1, :])
        o_ref[0, :, hd * HEAD_DIM:(hd + 1) * HEAD_DIM] = o.T.astype(BF16)


def _attn_call(qT, k, vT):
    b, n_qb, n_q, tq = qT.shape
    s, n_kv = k.shape[1:]
    tk = min(ATTN_K, s)
    kernel = functools.partial(_attn_kernel, n_chunks=s // tk, tk=tk)
    return pl.pallas_call(
        kernel,
        grid=(b, n_qb),
        in_specs=[
            pl.BlockSpec((1, 1, n_q, tq), lambda bi, i: (bi, i, 0, 0)),
            pl.BlockSpec((1, s, n_kv), lambda bi, i: (bi, 0, 0), pipeline_mode=pl.Buffered(1)),
            pl.BlockSpec((1, n_kv, s), lambda bi, i: (bi, 0, 0), pipeline_mode=pl.Buffered(1)),
        ],
        out_specs=pl.BlockSpec((1, tq, n_q), lambda bi, i: (bi, i, 0)),
        out_shape=jax.ShapeDtypeStruct((b, s, n_q), BF16),
        scratch_shapes=[
            pltpu.VMEM((N_HEADS, tq), F32),
            pltpu.VMEM((N_HEADS, tq), F32),
            pltpu.VMEM((N_HEADS, HEAD_DIM + ONES_ROWS, tq), F32),
            pltpu.VMEM((N_HEADS, tk, tq), F32),
            pltpu.VMEM((N_HEADS, 1, tq), F32),
            pltpu.VMEM((N_HEADS, tk, tq), BF16),
        ],
        compiler_params=_params(2, True),
        name="attn",
    )(qT, k, vT)


def _reset_carry_at_sequence_start(carry_ref):
    @pl.when(pl.program_id(1) == 0)
    def _():
        carry_ref[...] = jnp.zeros(carry_ref.shape, F32)


def _lru_inputs(x_ref, xp_ref, xn_ref, g_ref, wxl_ref, cw_ref, cb_ref, *, reverse, t):
    i = pl.program_id(1)
    nb = pl.num_programs(1)
    blk = nb - 1 - i if reverse else i
    d_lru = wxl_ref.shape[1]
    groups = t // SUBLANES
    xp = xp_ref[0] * jnp.where(blk > 0, 1.0, 0.0)
    xn = xn_ref[0] * jnp.where(blk < nb - 1, 1.0, 0.0)
    xe = jnp.concatenate([xp, x_ref[0], xn], axis=0)
    he = (xe * _rms_scale(xe) * g_ref[...]).astype(BF16)
    xl = jnp.dot(he, wxl_ref[...], preferred_element_type=F32)
    x3 = xl.reshape(groups + 2, SUBLANES, d_lru)
    sub = lax.broadcasted_iota(jnp.int32, (1, SUBLANES, d_lru), 1)

    def delayed(k):
        r = pltpu.roll(x3, k, axis=1)
        return jnp.where(sub >= k, r[1:groups + 1], r[0:groups])

    ahead = pltpu.roll(x3, SUBLANES - 1, axis=1)
    ahead = jnp.where(sub < SUBLANES - 1, ahead[1:groups + 1], ahead[2:groups + 2])
    cw = cw_ref[...]
    xc = (cb_ref[...] + cw[0:1] * delayed(2) + cw[1:2] * delayed(1) + cw[2:3] * x3[1:groups + 1]
          + cw[3:4] * ahead)
    return he[HALO:HALO + t], xc.reshape(t, d_lru)


def _lru_gates(xc, wg_ref, bg_ref):
    return [
        jnp.dot(xc[:, n * LRU_BLOCK:(n + 1) * LRU_BLOCK].astype(BF16), wg_ref[n], preferred_element_type=F32)
        + bg_ref[n]
        for n in range(N_LRU_BLOCKS)
    ]


def _lru_scan(xc, gate_pre, cneg_ref, a_scr, u_scr, carry_ref, write_rows, *, reverse, t):
    groups = t // SUBLANES
    sub = lax.broadcasted_iota(jnp.int32, (1, SUBLANES, LRU_BLOCK), 1)
    for n in range(N_LRU_BLOCKS):
        cols = slice(n * LRU_BLOCK, (n + 1) * LRU_BLOCK)
        xcn = xc[:, cols]
        gz = gate_pre[n]
        r = jax.nn.sigmoid(gz[:, :LRU_BLOCK])
        ig = jax.nn.sigmoid(gz[:, LRU_BLOCK:])
        a = jnp.exp(cneg_ref[:, cols] * r)
        y = (1.0 - a) * (1.0 + a)
        u = (y * lax.rsqrt(jnp.maximum(y, TINY))) * (ig * xcn)
        a3 = a.reshape(groups, SUBLANES, LRU_BLOCK)
        u3 = u.reshape(groups, SUBLANES, LRU_BLOCK)
        for dist in (1, 2, 4):
            if reverse:
                shift, ok = SUBLANES - dist, sub < SUBLANES - dist
            else:
                shift, ok = dist, sub >= dist
            a_nb = pltpu.roll(a3, shift, axis=1)
            u_nb = pltpu.roll(u3, shift, axis=1)
            u3 = jnp.where(ok, a3 * u_nb + u3, u3)
            a3 = jnp.where(ok, a3 * a_nb, a3)
        a_scr[:, :, cols] = a3
        u_scr[:, :, cols] = u3

    edge = 0 if reverse else SUBLANES - 1
    for n in range(N_LRU_BLOCKS):
        cols = slice(n * LRU_BLOCK, (n + 1) * LRU_BLOCK)
        hrow = carry_ref[:, cols]
        for j in range(groups):
            jj = groups - 1 - j if reverse else j
            rows = u_scr[jj, :, cols] + a_scr[jj, :, cols] * hrow
            write_rows(jj, cols, rows)
            hrow = jnp.broadcast_to(rows[edge:edge + 1, :], (SUBLANES, LRU_BLOCK))
        carry_ref[:, cols] = hrow


def _lru_f_kernel(x_ref, xp_ref, xn_ref, g_ref, wxl_ref, cw_ref, cb_ref, wg_ref, bg_ref, cneg_ref,
                  hf_ref, a_scr, u_scr, carry_ref, *, t):
    def write_rows(j, cols, rows):
        hf_ref[0, j, :, cols] = rows

    _reset_carry_at_sequence_start(carry_ref)
    _, xc = _lru_inputs(x_ref, xp_ref, xn_ref, g_ref, wxl_ref, cw_ref, cb_ref, reverse=False, t=t)
    gate_pre = _lru_gates(xc, wg_ref, bg_ref)
    _lru_scan(xc, gate_pre, cneg_ref, a_scr, u_scr, carry_ref, write_rows, reverse=False, t=t)


def _merge_kernel(x_ref, xp_ref, xn_ref, g_ref, wxl_ref, cw_ref, cb_ref, wg_ref, bg_ref, cneg_ref,
                  hf_ref, att_ref, wgl_ref, wga_ref, wm_ref, bm_ref, wb0_ref, wb1_ref, wo_ref, gfin_ref,
                  y_ref, a_scr, u_scr, carry_ref, hb_scr, *, t, final_norm):
    def write_rows(j, cols, rows):
        hb_scr[j, :, cols] = rows

    _reset_carry_at_sequence_start(carry_ref)
    d = x_ref.shape[2]
    h, xc = _lru_inputs(x_ref, xp_ref, xn_ref, g_ref, wxl_ref, cw_ref, cb_ref, reverse=True, t=t)
    gate_pre = _lru_gates(xc, wg_ref, bg_ref)
    _lru_scan(xc, gate_pre, cneg_ref, a_scr, u_scr, carry_ref, write_rows, reverse=True, t=t)
    g_l = jnp.dot(h, wgl_ref[...], preferred_element_type=F32)
    lsum = (hf_ref[0] + hb_scr[...]).reshape(t, wxl_ref.shape[1])
    l_out = (lsum * (g_l * jax.nn.sigmoid(g_l))).astype(BF16)
    g_a = jnp.dot(h, wga_ref[...], preferred_element_type=F32)
    a_out = (att_ref[0].astype(F32) * (g_a * jax.nn.sigmoid(g_a))).astype(BF16)
    a_proj = jnp.dot(a_out, wb0_ref[...], preferred_element_type=F32)
    l_proj = jnp.dot(l_out, wb1_ref[...], preferred_element_type=F32)
    gates = jax.nn.sigmoid(jnp.dot(h, wm_ref[...], preferred_element_type=F32) + bm_ref[...])
    merged = gates[:, :d] * a_proj + gates[:, d:] * l_proj
    y = x_ref[0] + jnp.dot(merged.astype(BF16), wo_ref[...], preferred_element_type=F32)
    if final_norm:
        y = y * _rms_scale(y) * gfin_ref[...]
    y_ref[0] = y


def _lru_specs(s, t, d, d_lru, reverse):
    nb = s // t
    groups = t // SUBLANES
    last_group = s // SUBLANES - 1

    def blk(i):
        return nb - 1 - i if reverse else i

    x_spec = pl.BlockSpec((1, t, d), lambda bi, i: (bi, blk(i), 0))
    xp_spec = pl.BlockSpec((1, HALO, d), lambda bi, i: (bi, jnp.maximum(blk(i) * groups - 1, 0), 0))
    xn_spec = pl.BlockSpec((1, HALO, d), lambda bi, i: (bi, jnp.minimum((blk(i) + 1) * groups, last_group), 0))
    hf_spec = pl.BlockSpec((1, groups, SUBLANES, d_lru), lambda bi, i: (bi, blk(i), 0, 0))
    scratch = [
        pltpu.VMEM((groups, SUBLANES, d_lru), F32),
        pltpu.VMEM((groups, SUBLANES, d_lru), F32),
        pltpu.VMEM((SUBLANES, d_lru), F32),
    ]
    return x_spec, xp_spec, xn_spec, hf_spec, scratch, blk


def _lru_f_call(x, g_in, w_xl, conv_w, conv_b, w_gate, b_gate, cneg):
    b, s, d = x.shape
    d_lru = w_xl.shape[1]
    t = min(LRU_ROWS, s)
    x_spec, xp_spec, xn_spec, hf_spec, scratch, _ = _lru_specs(s, t, d, d_lru, False)
    return pl.pallas_call(
        functools.partial(_lru_f_kernel, t=t),
        grid=(b, s // t),
        in_specs=[x_spec, xp_spec, xn_spec, _const_spec((1, d)), _const_spec(w_xl.shape),
                  _const_spec(conv_w.shape), _const_spec(conv_b.shape), _const_spec(w_gate.shape),
                  _const_spec(b_gate.shape), _const_spec(cneg.shape)],
        out_specs=hf_spec,
        out_shape=jax.ShapeDtypeStruct((b, s // SUBLANES, SUBLANES, d_lru), F32),
        scratch_shapes=scratch,
        compiler_params=_params(2, True),
        name="lru_f",
    )(x, x, x, g_in, w_xl, conv_w, conv_b, w_gate, b_gate, cneg)


def _merge_call(x, g_in, w_xl, conv_w, conv_b, w_gate, b_gate, cneg, hf, att,
                w_gl, w_ga, w_m, b_m, w_b0, w_b1, w_o, g_fin, final_norm):
    b, s, d = x.shape
    d_lru = w_xl.shape[1]
    t = min(LRU_ROWS, s)
    x_spec, xp_spec, xn_spec, hf_spec, scratch, blk = _lru_specs(s, t, d, d_lru, True)
    tok = lambda width: pl.BlockSpec((1, t, width), lambda bi, i: (bi, blk(i), 0))
    consts = [g_in, w_xl, conv_w, conv_b, w_gate, b_gate, cneg]
    tail = [w_gl, w_ga, w_m, b_m, w_b0, w_b1, w_o, g_fin]
    return pl.pallas_call(
        functools.partial(_merge_kernel, t=t, final_norm=final_norm),
        grid=(b, s // t),
        in_specs=[x_spec, xp_spec, xn_spec] + [_const_spec(c.shape) for c in consts]
        + [hf_spec, tok(att.shape[2])] + [_const_spec(c.shape) for c in tail],
        out_specs=tok(d),
        out_shape=jax.ShapeDtypeStruct((b, s, d), F32),
        scratch_shapes=scratch + [pltpu.VMEM((t // SUBLANES, SUBLANES, d_lru), F32)],
        compiler_params=_params(2, True),
        name="merge",
    )(x, x, x, *consts, hf, att, *tail)


def _rope_tables(s, gain, scale):
    rows_n = s // GRID_W
    rows = jnp.repeat(jnp.arange(rows_n, dtype=F32), GRID_W)
    cols = jnp.tile(jnp.arange(GRID_W, dtype=F32), rows_n)
    n_pair_axis = HEAD_DIM // 4
    inv_freq = ROPE_THETA ** (-jnp.arange(n_pair_axis, dtype=F32) / n_pair_axis)
    ang = jnp.concatenate([rows[:, None] * inv_freq, cols[:, None] * inv_freq], axis=-1)
    cos = jnp.cos(ang).T
    sin = jnp.sin(ang).T
    c = jnp.concatenate([cos, cos], axis=0)
    sn = jnp.concatenate([-sin, sin], axis=0)
    gain = gain.astype(F32)
    return (gain[:, None] * c) * scale, (jnp.roll(gain, HALF)[:, None] * sn) * scale


def _layer_params(w_in, q_norm, k_norm, w_rgate, b_rgate, w_igate, b_igate, lam, w_branch, w_out, d):
    d_attn = N_HEADS * HEAD_DIM
    d_kv = N_KV_HEADS * HEAD_DIM
    d_lru = N_LRU_BLOCKS * LRU_BLOCK
    splits = [d_attn, d_kv, d_kv, d_attn, d_lru, d_lru]
    offs = [0]
    for w in splits:
        offs.append(offs[-1] + w)
    w_q, w_k, w_v, w_ga, w_xl, w_gl = (w_in[:, offs[j]:offs[j + 1]] for j in range(6))
    w_m = w_in[:, offs[6]:]
    perm = jnp.concatenate([jnp.arange(0, HEAD_DIM, 2), jnp.arange(1, HEAD_DIM, 2)])
    permute = lambda w, nh: w.reshape(d, nh, HEAD_DIM)[:, :, perm].reshape(d, nh * HEAD_DIM)
    w_qkvT = jnp.concatenate([permute(w_q, N_HEADS), permute(w_k, N_KV_HEADS), w_v], axis=1).T.astype(BF16)
    w_gate = jnp.concatenate([w_rgate, w_igate], axis=-1).astype(BF16)
    b_gate = jnp.concatenate([b_rgate.reshape(2, N_LRU_BLOCKS, 1, LRU_BLOCK),
                              b_igate.reshape(2, N_LRU_BLOCKS, 1, LRU_BLOCK)], axis=-1).astype(F32)
    cneg = (-RG_C * jax.nn.softplus(-lam.astype(F32))).reshape(2, 1, d_lru)
    return dict(
        w_qkvT=w_qkvT, gq=q_norm[perm], gk=k_norm[perm],
        w_ga=w_ga.astype(BF16), w_xl=w_xl.astype(BF16), w_gl=w_gl.astype(BF16), w_m=w_m.astype(BF16),
        w_gate=w_gate, b_gate=b_gate, cneg=cneg,
        w_b0=w_branch[0].astype(BF16), w_b1=w_branch[1].astype(BF16), w_o=w_out.astype(BF16),
    )


def _layer(x, p, g_in, conv_w, conv_b, b_m, g_fin, final_norm):
    s = x.shape[1]
    q_scale = math.log2(math.e) / math.sqrt(HEAD_DIM)
    cq, sq = _rope_tables(s, p["gq"], q_scale)
    ck, sk = _rope_tables(s, p["gk"], 1.0)
    qT, k, vT = _qkv_call(x, g_in, p["w_qkvT"], cq, sq, ck, sk)
    att = _attn_call(qT, k, vT)
    lru = (g_in, p["w_xl"], conv_w, conv_b)
    hf = _lru_f_call(x, *lru, p["w_gate"][0], p["b_gate"][0], p["cneg"][0])
    return _merge_call(x, *lru, p["w_gate"][1], p["b_gate"][1], p["cneg"][1], hf, att,
                       p["w_gl"], p["w_ga"], p["w_m"], b_m, p["w_b0"], p["w_b1"], p["w_o"], g_fin, final_norm)


def kernel(x_prompt, x_sample, norm_in, w_in, b_merge, q_norm, k_norm, conv_w, conv_b, w_rgate, b_rgate,
           w_igate, b_igate, lam, w_branch, w_out, norm_final):
    depth, d = norm_in.shape
    layers = [
        _layer_params(w_in[l], q_norm[l], k_norm[l], w_rgate[l], b_rgate[l], w_igate[l], b_igate[l], lam[l],
                      w_branch[l], w_out[l], d)
        for l in range(depth)
    ]
    g_fin = norm_final.reshape(1, d).astype(F32)
    outs = []
    for x in (x_prompt, x_sample):
        for l in range(depth):
            x = _layer(x, layers[l], norm_in[l].reshape(1, d).astype(F32), conv_w[l].astype(F32),
                       conv_b[l].reshape(1, -1).astype(F32), b_merge[l].reshape(1, -1).astype(F32), g_fin,
                       l == depth - 1)
        outs.append(x)
    return tuple(outs)
```

```python
import functools
import math

import jax
import jax.numpy as jnp
from jax import lax
from jax.experimental import pallas as pl
from jax.experimental.pallas import tpu as pltpu

N_HEADS = 8
N_KV_HEADS = 2
GROUP = N_HEADS // N_KV_HEADS
HEAD_DIM = 128
HALF = HEAD_DIM // 2
N_LRU_BLOCKS = 8
LRU_BLOCK = 128
CONV_W = 4
RG_C = 8.0
EPS = 1e-6
TINY = 1e-30
GRID_W = 64
ROPE_THETA = 10000.0

SUBLANES = 8
ONES_ROWS = 2 * SUBLANES
HALO = SUBLANES
V7X_VMEM_LIMIT = 56 * 1024 * 1024

QKV_ROWS = 512
ATTN_Q = 512
ATTN_K = 256
LRU_ROWS = 256

F32 = jnp.float32
BF16 = jnp.bfloat16


def _const_spec(shape):
    nd = len(shape)
    return pl.BlockSpec(shape, lambda *_: (0,) * nd, pipeline_mode=pl.Buffered(1))


def _params(n_axes, sequential_last, flags=None):
    sem = ["parallel"] * n_axes
    if sequential_last:
        sem[-1] = "arbitrary"
    return pltpu.CompilerParams(dimension_semantics=tuple(sem), vmem_limit_bytes=V7X_VMEM_LIMIT, flags=flags)


def _rms_scale(x):
    return lax.rsqrt(jnp.mean(x * x, axis=-1, keepdims=True) + EPS)


def _qkv_kernel(x_ref, g_ref, w_ref, cq_ref, sq_ref, ck_ref, sk_ref, qT_ref, k_ref, vT_ref):
    x = x_ref[0]
    h = (x * _rms_scale(x) * g_ref[...]).astype(BF16)
    zT = lax.dot_general(w_ref[...], h, (((1,), (1,)), ((), ())), preferred_element_type=F32)

    def norm_rope(z, c, s):
        n = z * lax.rsqrt(jnp.mean(z * z, axis=0, keepdims=True) + EPS)
        swapped = jnp.concatenate([n[HALF:], n[:HALF]], axis=0)
        return n * c + swapped * s

    cq, sq = cq_ref[...], sq_ref[...]
    for hd in range(N_HEADS):
        rows = slice(hd * HEAD_DIM, (hd + 1) * HEAD_DIM)
        qh = norm_rope(zT[rows], cq, sq).astype(BF16)
        tq = qT_ref.shape[3]
        for qb in range(qT_ref.shape[1]):
            qT_ref[0, qb, rows, :] = qh[:, qb * tq:(qb + 1) * tq]
    ck, sk = ck_ref[...], sk_ref[...]
    k_off = N_HEADS * HEAD_DIM
    for kv in range(N_KV_HEADS):
        rows = slice(k_off + kv * HEAD_DIM, k_off + (kv + 1) * HEAD_DIM)
        kT = norm_rope(zT[rows], ck, sk)
        k_ref[0, :, kv * HEAD_DIM:(kv + 1) * HEAD_DIM] = kT.T.astype(BF16)
    v_off = k_off + N_KV_HEADS * HEAD_DIM
    vT_ref[0] = zT[v_off:v_off + N_KV_HEADS * HEAD_DIM].astype(BF16)


def _qkv_call(x, g_in, w_qkvT, cq, sq, ck, sk):
    b, s, d = x.shape
    t = min(QKV_ROWS, s)
    tq = min(ATTN_Q, t)
    n_q, n_kv = N_HEADS * HEAD_DIM, N_KV_HEADS * HEAD_DIM
    tab = pl.BlockSpec((HEAD_DIM, t), lambda bi, i: (0, i))
    return pl.pallas_call(
        _qkv_kernel,
        grid=(b, s // t),
        in_specs=[
            pl.BlockSpec((1, t, d), lambda bi, i: (bi, i, 0)),
            _const_spec((1, d)),
            _const_spec(w_qkvT.shape),
            tab, tab, tab, tab,
        ],
        out_specs=[
            pl.BlockSpec((1, t // tq, n_q, tq), lambda bi, i: (bi, i, 0, 0)),
            pl.BlockSpec((1, t, n_kv), lambda bi, i: (bi, i, 0)),
            pl.BlockSpec((1, n_kv, t), lambda bi, i: (bi, 0, i)),
        ],
        out_shape=[
            jax.ShapeDtypeStruct((b, s // tq, n_q, tq), BF16),
            jax.ShapeDtypeStruct((b, s, n_kv), BF16),
            jax.ShapeDtypeStruct((b, n_kv, s), BF16),
        ],
        compiler_params=_params(2, False),
        name="qkv",
    )(x, g_in, w_qkvT, cq, sq, ck, sk)


def _attn_kernel(qT_ref, k_ref, vT_ref, o_ref, m_ref, alpha_ref, acc_ref, s_scr, cmax_ref, p_scr, *, n_chunks, tk):
    m_ref[...] = jnp.full(m_ref.shape, -jnp.inf, F32)
    acc_ref[...] = jnp.zeros(acc_ref.shape, F32)
    ones_rows = jnp.ones((ONES_ROWS, tk), BF16)
    last = N_HEADS - 1

    def kv_cols(hd):
        kv = hd // GROUP
        return slice(kv * HEAD_DIM, (kv + 1) * HEAD_DIM)

    def scores(c, hd):
        off = pl.multiple_of(c * tk, tk)
        kc = k_ref[0, pl.ds(off, tk), kv_cols(hd)]
        qT = qT_ref[0, 0, hd * HEAD_DIM:(hd + 1) * HEAD_DIM, :]
        s = jnp.dot(kc, qT, preferred_element_type=F32)
        s_scr[hd] = s
        cmax_ref[hd] = jnp.max(s, axis=0, keepdims=True)

    def softmax(hd):
        m_old = m_ref[hd:hd + 1, :]
        m_new = jnp.maximum(m_old, cmax_ref[hd])
        alpha_ref[hd:hd + 1, :] = jnp.exp2(m_old - m_new)
        m_ref[hd:hd + 1, :] = m_new
        p_scr[hd] = jnp.exp2(s_scr[hd] - m_new).astype(BF16)

    def update(c, hd):
        off = pl.multiple_of(c * tk, tk)
        vc = jnp.concatenate([vT_ref[0, kv_cols(hd), pl.ds(off, tk)], ones_rows], axis=0)
        pv = jnp.dot(vc, p_scr[hd], preferred_element_type=F32)
        acc_ref[hd] = alpha_ref[hd:hd + 1, :] * acc_ref[hd] + pv

    p_scr[last] = jnp.zeros(p_scr.shape[1:], BF16)
    alpha_ref[last:last + 1, :] = jnp.ones((1, alpha_ref.shape[1]), F32)
    scores(0, 0)

    def step(c, carry):
        for hd in range(N_HEADS):
            if hd < last:
                scores(c, hd + 1)
            else:
                scores(jnp.minimum(c + 1, n_chunks - 1), 0)
            softmax(hd)
            if hd > 0:
                update(c, hd - 1)
            else:
                update(jnp.maximum(c - 1, 0), last)
        return carry

    lax.fori_loop(0, n_chunks, step, 0, unroll=4)
    update(n_chunks - 1, last)

    for hd in range(N_HEADS):
        o = acc_ref[hd, :HEAD_DIM, :] * (1.0 / acc_ref[hd, HEAD_DIM:HEAD_DIM + 1, :])
        o_ref[0, :, hd * HEAD_DIM:(hd + 1) * HEAD_DIM] = o.T.astype(BF16)


def _attn_call(qT, k, vT):
    b, n_qb, n_q, tq = qT.shape
    s, n_kv = k.shape[1:]
    tk = min(ATTN_K, s)
    kernel = functools.partial(_attn_kernel, n_chunks=s // tk, tk=tk)
    return pl.pallas_call(
        kernel,
        grid=(b, n_qb),
        in_specs=[
            pl.BlockSpec((1, 1, n_q, tq), lambda bi, i: (bi, i, 0, 0)),
            pl.BlockSpec((1, s, n_kv), lambda bi, i: (bi, 0, 0), pipeline_mode=pl.Buffered(1)),
            pl.BlockSpec((1, n_kv, s), lambda bi, i: (bi, 0, 0), pipeline_mode=pl.Buffered(1)),
        ],
        out_specs=pl.BlockSpec((1, tq, n_q), lambda bi, i: (bi, i, 0)),
        out_shape=jax.ShapeDtypeStruct((b, s, n_q), BF16),
        scratch_shapes=[
            pltpu.VMEM((N_HEADS, tq), F32),
            pltpu.VMEM((N_HEADS, tq), F32),
            pltpu.VMEM((N_HEADS, HEAD_DIM + ONES_ROWS, tq), F32),
            pltpu.VMEM((N_HEADS, tk, tq), F32),
            pltpu.VMEM((N_HEADS, 1, tq), F32),
            pltpu.VMEM((N_HEADS, tk, tq), BF16),
        ],
        compiler_params=_params(2, True),
        name="attn",
    )(qT, k, vT)


def _reset_carry_at_sequence_start(carry_ref):
    @pl.when(pl.program_id(1) == 0)
    def _():
        carry_ref[...] = jnp.zeros(carry_ref.shape, F32)


def _lru_inputs(x_ref, xp_ref, xn_ref, g_ref, wxl_ref, cw_ref, cb_ref, *, t):
    blk = pl.program_id(1)
    nb = pl.num_programs(1)
    d_lru = wxl_ref.shape[1]
    groups = t // SUBLANES
    xp = xp_ref[0] * jnp.where(blk > 0, 1.0, 0.0)
    xn = xn_ref[0] * jnp.where(blk < nb - 1, 1.0, 0.0)
    xe = jnp.concatenate([xp, x_ref[0], xn], axis=0)
    he = (xe * _rms_scale(xe) * g_ref[...]).astype(BF16)
    xl = jnp.dot(he, wxl_ref[...], preferred_element_type=F32)
    x3 = xl.reshape(groups + 2, SUBLANES, d_lru)
    sub = lax.broadcasted_iota(jnp.int32, (1, SUBLANES, d_lru), 1)

    def delayed(k):
        r = pltpu.roll(x3, k, axis=1)
        return jnp.where(sub >= k, r[1:groups + 1], r[0:groups])

    ahead = pltpu.roll(x3, SUBLANES - 1, axis=1)
    ahead = jnp.where(sub < SUBLANES - 1, ahead[1:groups + 1], ahead[2:groups + 2])
    cw = cw_ref[...]
    xc = (cb_ref[...] + cw[0:1] * delayed(2) + cw[1:2] * delayed(1) + cw[2:3] * x3[1:groups + 1]
          + cw[3:4] * ahead)
    return xc.reshape(t, d_lru)


def _lru_gates(xc, wg_ref, bg_ref):
    return [
        jnp.dot(xc[:, n * LRU_BLOCK:(n + 1) * LRU_BLOCK].astype(BF16), wg_ref[n], preferred_element_type=F32)
        + bg_ref[n]
        for n in range(N_LRU_BLOCKS)
    ]


def _lru_scan(xc, gate_pre, cneg_ref, a_scr, u_scr, carry_ref, write_rows, *, reverse, t):
    groups = t // SUBLANES
    sub = lax.broadcasted_iota(jnp.int32, (1, SUBLANES, LRU_BLOCK), 1)
    for n in range(N_LRU_BLOCKS):
        cols = slice(n * LRU_BLOCK, (n + 1) * LRU_BLOCK)
        xcn = xc[:, cols]
        gz = gate_pre[n]
        r = jax.nn.sigmoid(gz[:, :LRU_BLOCK])
        ig = jax.nn.sigmoid(gz[:, LRU_BLOCK:])
        a = jnp.exp(cneg_ref[:, cols] * r)
        y = (1.0 - a) * (1.0 + a)
        u = (y * lax.rsqrt(jnp.maximum(y, TINY))) * (ig * xcn)
        a3 = a.reshape(groups, SUBLANES, LRU_BLOCK)
        u3 = u.reshape(groups, SUBLANES, LRU_BLOCK)
        for dist in (1, 2, 4):
            if reverse:
                shift, ok = SUBLANES - dist, sub < SUBLANES - dist
            else:
                shift, ok = dist, sub >= dist
            a_nb = pltpu.roll(a3, shift, axis=1)
            u_nb = pltpu.roll(u3, shift, axis=1)
            u3 = jnp.where(ok, a3 * u_nb + u3, u3)
            a3 = jnp.where(ok, a3 * a_nb, a3)
        a_scr[:, :, cols] = a3
        u_scr[:, :, cols] = u3

    edge = 0 if reverse else SUBLANES - 1
    for n in range(N_LRU_BLOCKS):
        cols = slice(n * LRU_BLOCK, (n + 1) * LRU_BLOCK)
        hrow = carry_ref[:, cols]
        for j in range(groups):
            jj = groups - 1 - j if reverse else j
            rows = u_scr[jj, :, cols] + a_scr[jj, :, cols] * hrow
            write_rows(jj, cols, rows)
            hrow = jnp.broadcast_to(rows[edge:edge + 1, :], (SUBLANES, LRU_BLOCK))
        carry_ref[:, cols] = hrow


def _lru_f_kernel(x_ref, xp_ref, xn_ref, g_ref, wxl_ref, cw_ref, cb_ref, wg_ref, bg_ref, cneg_ref,
                  hf_ref, xc_ref, a_scr, u_scr, carry_ref, *, t):
    def write_rows(j, cols, rows):
        hf_ref[0, j, :, cols] = rows

    _reset_carry_at_sequence_start(carry_ref)
    xc = _lru_inputs(x_ref, xp_ref, xn_ref, g_ref, wxl_ref, cw_ref, cb_ref, t=t)
    xc_ref[0] = xc
    gate_pre = _lru_gates(xc, wg_ref, bg_ref)
    _lru_scan(xc, gate_pre, cneg_ref, a_scr, u_scr, carry_ref, write_rows, reverse=False, t=t)


def _merge_kernel(x_ref, g_ref, xc_ref, wg_ref, bg_ref, cneg_ref,
                  hf_ref, att_ref, wgl_ref, wga_ref, wm_ref, bm_ref, wb0_ref, wb1_ref, wo_ref, gfin_ref,
                  y_ref, a_scr, u_scr, carry_ref, hb_scr, *, t, final_norm):
    def write_rows(j, cols, rows):
        hb_scr[j, :, cols] = rows

    _reset_carry_at_sequence_start(carry_ref)
    d = x_ref.shape[2]
    x = x_ref[0]
    h = (x * _rms_scale(x) * g_ref[...]).astype(BF16)
    xc = xc_ref[0]
    gate_pre = _lru_gates(xc, wg_ref, bg_ref)
    _lru_scan(xc, gate_pre, cneg_ref, a_scr, u_scr, carry_ref, write_rows, reverse=True, t=t)
    g_l = jnp.dot(h, wgl_ref[...], preferred_element_type=F32)
    lsum = (hf_ref[0] + hb_scr[...]).reshape(t, xc.shape[1])
    l_out = (lsum * (g_l * jax.nn.sigmoid(g_l))).astype(BF16)
    g_a = jnp.dot(h, wga_ref[...], preferred_element_type=F32)
    a_out = (att_ref[0].astype(F32) * (g_a * jax.nn.sigmoid(g_a))).astype(BF16)
    a_proj = jnp.dot(a_out, wb0_ref[...], preferred_element_type=F32)
    l_proj = jnp.dot(l_out, wb1_ref[...], preferred_element_type=F32)
    gates = jax.nn.sigmoid(jnp.dot(h, wm_ref[...], preferred_element_type=F32) + bm_ref[...])
    merged = gates[:, :d] * a_proj + gates[:, d:] * l_proj
    y = x + jnp.dot(merged.astype(BF16), wo_ref[...], preferred_element_type=F32)
    if final_norm:
        y = y * _rms_scale(y) * gfin_ref[...]
    y_ref[0] = y


def _lru_specs(s, t, d, d_lru, reverse):
    nb = s // t
    groups = t // SUBLANES
    last_group = s // SUBLANES - 1

    def blk(i):
        return nb - 1 - i if reverse else i

    x_spec = pl.BlockSpec((1, t, d), lambda bi, i: (bi, blk(i), 0))
    xp_spec = pl.BlockSpec((1, HALO, d), lambda bi, i: (bi, jnp.maximum(blk(i) * groups - 1, 0), 0))
    xn_spec = pl.BlockSpec((1, HALO, d), lambda bi, i: (bi, jnp.minimum((blk(i) + 1) * groups, last_group), 0))
    hf_spec = pl.BlockSpec((1, groups, SUBLANES, d_lru), lambda bi, i: (bi, blk(i), 0, 0))
    scratch = [
        pltpu.VMEM((groups, SUBLANES, d_lru), F32),
        pltpu.VMEM((groups, SUBLANES, d_lru), F32),
        pltpu.VMEM((SUBLANES, d_lru), F32),
    ]
    return x_spec, xp_spec, xn_spec, hf_spec, scratch, blk


def _lru_f_call(x, g_in, w_xl, conv_w, conv_b, w_gate, b_gate, cneg):
    b, s, d = x.shape
    d_lru = w_xl.shape[1]
    t = min(LRU_ROWS, s)
    x_spec, xp_spec, xn_spec, hf_spec, scratch, _ = _lru_specs(s, t, d, d_lru, False)
    return pl.pallas_call(
        functools.partial(_lru_f_kernel, t=t),
        grid=(b, s // t),
        in_specs=[x_spec, xp_spec, xn_spec, _const_spec((1, d)), _const_spec(w_xl.shape),
                  _const_spec(conv_w.shape), _const_spec(conv_b.shape), _const_spec(w_gate.shape),
                  _const_spec(b_gate.shape), _const_spec(cneg.shape)],
        out_specs=[hf_spec, pl.BlockSpec((1, t, d_lru), lambda bi, i: (bi, i, 0))],
        out_shape=[jax.ShapeDtypeStruct((b, s // SUBLANES, SUBLANES, d_lru), F32),
                   jax.ShapeDtypeStruct((b, s, d_lru), F32)],
        scratch_shapes=scratch,
        compiler_params=_params(2, True),
        name="lru_f",
    )(x, x, x, g_in, w_xl, conv_w, conv_b, w_gate, b_gate, cneg)


def _merge_call(x, g_in, xc, w_gate, b_gate, cneg, hf, att, w_gl, w_ga, w_m, b_m, w_b0, w_b1, w_o, g_fin, final_norm):
    b, s, d = x.shape
    d_lru = xc.shape[2]
    t = min(LRU_ROWS, s)
    _, _, _, hf_spec, scratch, blk = _lru_specs(s, t, d, d_lru, True)
    tok = lambda width: pl.BlockSpec((1, t, width), lambda bi, i: (bi, blk(i), 0))
    consts = [w_gate, b_gate, cneg]
    tail = [w_gl, w_ga, w_m, b_m, w_b0, w_b1, w_o, g_fin]
    return pl.pallas_call(
        functools.partial(_merge_kernel, t=t, final_norm=final_norm),
        grid=(b, s // t),
        in_specs=[tok(d), _const_spec(g_in.shape), tok(d_lru)] + [_const_spec(c.shape) for c in consts]
        + [hf_spec, tok(att.shape[2])] + [_const_spec(c.shape) for c in tail],
        out_specs=tok(d),
        out_shape=jax.ShapeDtypeStruct((b, s, d), F32),
        scratch_shapes=scratch + [pltpu.VMEM((t // SUBLANES, SUBLANES, d_lru), F32)],
        compiler_params=_params(2, True),
        name="merge",
    )(x, g_in, xc, *consts, hf, att, *tail)


def _rope_tables(s, gain, scale):
    rows_n = s // GRID_W
    rows = jnp.repeat(jnp.arange(rows_n, dtype=F32), GRID_W)
    cols = jnp.tile(jnp.arange(GRID_W, dtype=F32), rows_n)
    n_pair_axis = HEAD_DIM // 4
    inv_freq = ROPE_THETA ** (-jnp.arange(n_pair_axis, dtype=F32) / n_pair_axis)
    ang = jnp.concatenate([rows[:, None] * inv_freq, cols[:, None] * inv_freq], axis=-1)
    cos = jnp.cos(ang).T
    sin = jnp.sin(ang).T
    c = jnp.concatenate([cos, cos], axis=0)
    sn = jnp.concatenate([-sin, sin], axis=0)
    gain = gain.astype(F32)
    return (gain[:, None] * c) * scale, (jnp.roll(gain, HALF)[:, None] * sn) * scale


def _layer_params(w_in, q_norm, k_norm, w_rgate, b_rgate, w_igate, b_igate, lam, w_branch, w_out, d):
    d_attn = N_HEADS * HEAD_DIM
    d_kv = N_KV_HEADS * HEAD_DIM
    d_lru = N_LRU_BLOCKS * LRU_BLOCK
    splits = [d_attn, d_kv, d_kv, d_attn, d_lru, d_lru]
    offs = [0]
    for w in splits:
        offs.append(offs[-1] + w)
    w_q, w_k, w_v, w_ga, w_xl, w_gl = (w_in[:, offs[j]:offs[j + 1]] for j in range(6))
    w_m = w_in[:, offs[6]:]
    perm = jnp.concatenate([jnp.arange(0, HEAD_DIM, 2), jnp.arange(1, HEAD_DIM, 2)])
    permute = lambda w, nh: w.reshape(d, nh, HEAD_DIM)[:, :, perm].reshape(d, nh * HEAD_DIM)
    w_qkvT = jnp.concatenate([permute(w_q, N_HEADS), permute(w_k, N_KV_HEADS), w_v], axis=1).T.astype(BF16)
    w_gate = jnp.concatenate([w_rgate, w_igate], axis=-1).astype(BF16)
    b_gate = jnp.concatenate([b_rgate.reshape(2, N_LRU_BLOCKS, 1, LRU_BLOCK),
                              b_igate.reshape(2, N_LRU_BLOCKS, 1, LRU_BLOCK)], axis=-1).astype(F32)
    cneg = (-RG_C * jax.nn.softplus(-lam.astype(F32))).reshape(2, 1, d_lru)
    return dict(
        w_qkvT=w_qkvT, gq=q_norm[perm], gk=k_norm[perm],
        w_ga=w_ga.astype(BF16), w_xl=w_xl.astype(BF16), w_gl=w_gl.astype(BF16), w_m=w_m.astype(BF16),
        w_gate=w_gate, b_gate=b_gate, cneg=cneg,
        w_b0=w_branch[0].astype(BF16), w_b1=w_branch[1].astype(BF16), w_o=w_out.astype(BF16),
    )


def _layer(x, p, g_in, conv_w, conv_b, b_m, g_fin, final_norm):
    s = x.shape[1]
    q_scale = math.log2(math.e) / math.sqrt(HEAD_DIM)
    cq, sq = _rope_tables(s, p["gq"], q_scale)
    ck, sk = _rope_tables(s, p["gk"], 1.0)
    qT, k, vT = _qkv_call(x, g_in, p["w_qkvT"], cq, sq, ck, sk)
    att = _attn_call(qT, k, vT)
    lru = (g_in, p["w_xl"], conv_w, conv_b)
    hf, xc = _lru_f_call(x, *lru, p["w_gate"][0], p["b_gate"][0], p["cneg"][0])
    return _merge_call(x, g_in, xc, p["w_gate"][1], p["b_gate"][1], p["cneg"][1], hf, att,
                       p["w_gl"], p["w_ga"], p["w_m"], b_m, p["w_b0"], p["w_b1"], p["w_o"], g_fin, final_norm)


def kernel(x_prompt, x_sample, norm_in, w_in, b_merge, q_norm, k_norm, conv_w, conv_b, w_rgate, b_rgate,
           w_igate, b_igate, lam, w_branch, w_out, norm_final):
    depth, d = norm_in.shape
    layers = [
        _layer_params(w_in[l], q_norm[l], k_norm[l], w_rgate[l], b_rgate[l], w_igate[l], b_igate[l], lam[l],
                      w_branch[l], w_out[l], d)
        for l in range(depth)
    ]
    g_fin = norm_final.reshape(1, d).astype(F32)
    outs = []
    for x in (x_prompt, x_sample):
        for l in range(depth):
            x = _layer(x, layers[l], norm_in[l].reshape(1, d).astype(F32), conv_w[l].astype(F32),
                       conv_b[l].reshape(1, -1).astype(F32), b_merge[l].reshape(1, -1).astype(F32), g_fin,
                       l == depth - 1)
        outs.append(x)
    return tuple(outs)
```

```python
import functools
import math

import jax
import jax.numpy as jnp
from jax import lax
from jax.experimental import pallas as pl
from jax.experimental.pallas import tpu as pltpu

N_HEADS = 8
N_KV_HEADS = 2
GROUP = N_HEADS // N_KV_HEADS
HEAD_DIM = 128
HALF = HEAD_DIM // 2
N_LRU_BLOCKS = 8
LRU_BLOCK = 128
CONV_W = 4
RG_C = 8.0
EPS = 1e-6
TINY = 1e-30
GRID_W = 64
ROPE_THETA = 10000.0

SUBLANES = 8
ONES_ROWS = 2 * SUBLANES
HALO = SUBLANES
V7X_VMEM_LIMIT = 56 * 1024 * 1024

QKV_ROWS = 512
ATTN_Q = 512
ATTN_K = 256
AHEAD = 2
BEHIND = 1
LRU_ROWS = 256

F32 = jnp.float32
BF16 = jnp.bfloat16


def _const_spec(shape):
    nd = len(shape)
    return pl.BlockSpec(shape, lambda *_: (0,) * nd, pipeline_mode=pl.Buffered(1))


def _params(n_axes, sequential_last, flags=None):
    sem = ["parallel"] * n_axes
    if sequential_last:
        sem[-1] = "arbitrary"
    return pltpu.CompilerParams(dimension_semantics=tuple(sem), vmem_limit_bytes=V7X_VMEM_LIMIT, flags=flags)


def _rms_scale(x):
    return lax.rsqrt(jnp.mean(x * x, axis=-1, keepdims=True) + EPS)


def _qkv_kernel(x_ref, g_ref, w_ref, cq_ref, sq_ref, ck_ref, sk_ref, qT_ref, k_ref, vT_ref):
    x = x_ref[0]
    h = (x * _rms_scale(x) * g_ref[...]).astype(BF16)
    zT = lax.dot_general(w_ref[...], h, (((1,), (1,)), ((), ())), preferred_element_type=F32)

    def norm_rope(z, c, s):
        n = z * lax.rsqrt(jnp.mean(z * z, axis=0, keepdims=True) + EPS)
        swapped = jnp.concatenate([n[HALF:], n[:HALF]], axis=0)
        return n * c + swapped * s

    cq, sq = cq_ref[...], sq_ref[...]
    for hd in range(N_HEADS):
        rows = slice(hd * HEAD_DIM, (hd + 1) * HEAD_DIM)
        qh = norm_rope(zT[rows], cq, sq).astype(BF16)
        tq = qT_ref.shape[3]
        for qb in range(qT_ref.shape[1]):
            qT_ref[0, qb, rows, :] = qh[:, qb * tq:(qb + 1) * tq]
    ck, sk = ck_ref[...], sk_ref[...]
    k_off = N_HEADS * HEAD_DIM
    for kv in range(N_KV_HEADS):
        rows = slice(k_off + kv * HEAD_DIM, k_off + (kv + 1) * HEAD_DIM)
        kT = norm_rope(zT[rows], ck, sk)
        k_ref[0, :, kv * HEAD_DIM:(kv + 1) * HEAD_DIM] = kT.T.astype(BF16)
    v_off = k_off + N_KV_HEADS * HEAD_DIM
    vT_ref[0] = zT[v_off:v_off + N_KV_HEADS * HEAD_DIM].astype(BF16)


def _qkv_call(x, g_in, w_qkvT, cq, sq, ck, sk):
    b, s, d = x.shape
    t = min(QKV_ROWS, s)
    tq = min(ATTN_Q, t)
    n_q, n_kv = N_HEADS * HEAD_DIM, N_KV_HEADS * HEAD_DIM
    tab = pl.BlockSpec((HEAD_DIM, t), lambda bi, i: (0, i))
    return pl.pallas_call(
        _qkv_kernel,
        grid=(b, s // t),
        in_specs=[
            pl.BlockSpec((1, t, d), lambda bi, i: (bi, i, 0)),
            _const_spec((1, d)),
            _const_spec(w_qkvT.shape),
            tab, tab, tab, tab,
        ],
        out_specs=[
            pl.BlockSpec((1, t // tq, n_q, tq), lambda bi, i: (bi, i, 0, 0)),
            pl.BlockSpec((1, t, n_kv), lambda bi, i: (bi, i, 0)),
            pl.BlockSpec((1, n_kv, t), lambda bi, i: (bi, 0, i)),
        ],
        out_shape=[
            jax.ShapeDtypeStruct((b, s // tq, n_q, tq), BF16),
            jax.ShapeDtypeStruct((b, s, n_kv), BF16),
            jax.ShapeDtypeStruct((b, n_kv, s), BF16),
        ],
        compiler_params=_params(2, False),
        name="qkv",
    )(x, g_in, w_qkvT, cq, sq, ck, sk)


def _attn_kernel(zero_ref, qT_ref, k_ref, vT_ref, o_ref, m_ref, alpha_ref, acc_ref, cmax_ref, *slots, n_chunks, tk):
    m_ref[...] = jnp.full(m_ref.shape, -jnp.inf, F32)
    acc_ref[...] = jnp.zeros(acc_ref.shape, F32)
    ones_rows = jnp.ones((ONES_ROWS, tk), BF16)
    s_slots, p_slots = slots[:N_HEADS], slots[N_HEADS:]
    zw, zr = zero_ref[0], zero_ref[1]

    def kv_cols(hd):
        kv = hd // GROUP
        return slice(kv * HEAD_DIM, (kv + 1) * HEAD_DIM)

    def scores(c, hd):
        off = pl.multiple_of(c * tk, tk)
        kc = k_ref[0, pl.ds(off, tk), kv_cols(hd)]
        qT = qT_ref[0, 0, hd * HEAD_DIM:(hd + 1) * HEAD_DIM, :]
        s = jnp.dot(kc, qT, preferred_element_type=F32)
        s_slots[hd][zw] = s
        cmax_ref[hd] = jnp.max(s, axis=0, keepdims=True)

    def softmax(hd):
        m_old = m_ref[hd:hd + 1, :]
        m_new = jnp.maximum(m_old, cmax_ref[hd])
        alpha_ref[hd:hd + 1, :] = jnp.exp2(m_old - m_new)
        m_ref[hd:hd + 1, :] = m_new
        p_slots[hd][zw] = jnp.exp2(s_slots[hd][zr] - m_new).astype(BF16)

    def update(c, hd):
        off = pl.multiple_of(c * tk, tk)
        vc = jnp.concatenate([vT_ref[0, kv_cols(hd), pl.ds(off, tk)], ones_rows], axis=0)
        pv = jnp.dot(vc, p_slots[hd][zr], preferred_element_type=F32)
        acc_ref[hd] = alpha_ref[hd:hd + 1, :] * acc_ref[hd] + pv

    def item(c, hd, shift):
        w = hd + shift
        return c + w // N_HEADS, w % N_HEADS

    for hd in range(N_HEADS - BEHIND, N_HEADS):
        p_slots[hd][zw] = jnp.zeros(p_slots[hd].shape[1:], BF16)
        alpha_ref[hd:hd + 1, :] = jnp.ones((1, alpha_ref.shape[1]), F32)
    for hd in range(AHEAD):
        scores(0, hd)

    def step(c, carry):
        for hd in range(N_HEADS):
            cs, hs = item(c, hd, AHEAD)
            scores(jnp.minimum(cs, n_chunks - 1), hs)
            softmax(hd)
            cu, hu = item(c, hd, -BEHIND)
            update(jnp.maximum(cu, 0), hu)
        return carry

    lax.fori_loop(0, n_chunks, step, 0, unroll=4)
    for hd in range(N_HEADS - BEHIND, N_HEADS):
        update(n_chunks - 1, hd)

    for hd in range(N_HEADS):
        o = acc_ref[hd, :HEAD_DIM, :] * (1.0 / acc_ref[hd, HEAD_DIM:HEAD_DIM + 1, :])
        o_ref[0, :, hd * HEAD_DIM:(hd + 1) * HEAD_DIM] = o.T.astype(BF16)


def _attn_call(qT, k, vT):
    b, n_qb, n_q, tq = qT.shape
    s, n_kv = k.shape[1:]
    tk = min(ATTN_K, s)
    kernel = functools.partial(_attn_kernel, n_chunks=s // tk, tk=tk)
    return pl.pallas_call(
        kernel,
        grid=(b, n_qb),
        in_specs=[
            pl.BlockSpec(memory_space=pltpu.SMEM),
            pl.BlockSpec((1, 1, n_q, tq), lambda bi, i: (bi, i, 0, 0)),
            pl.BlockSpec((1, s, n_kv), lambda bi, i: (bi, 0, 0), pipeline_mode=pl.Buffered(1)),
            pl.BlockSpec((1, n_kv, s), lambda bi, i: (bi, 0, 0), pipeline_mode=pl.Buffered(1)),
        ],
        out_specs=pl.BlockSpec((1, tq, n_q), lambda bi, i: (bi, i, 0)),
        out_shape=jax.ShapeDtypeStruct((b, s, n_q), BF16),
        scratch_shapes=[
            pltpu.VMEM((N_HEADS, tq), F32),
            pltpu.VMEM((N_HEADS, tq), F32),
            pltpu.VMEM((N_HEADS, HEAD_DIM + ONES_ROWS, tq), F32),
            pltpu.VMEM((N_HEADS, 1, tq), F32),
        ] + [pltpu.VMEM((1, tk, tq), F32)] * N_HEADS
        + [pltpu.VMEM((1, tk, tq), BF16)] * N_HEADS,
        compiler_params=_params(2, True),
        name="attn",
    )(jnp.zeros((2,), jnp.int32), qT, k, vT)


def _reset_carry_at_sequence_start(carry_ref):
    @pl.when(pl.program_id(1) == 0)
    def _():
        carry_ref[...] = jnp.zeros(carry_ref.shape, F32)


def _lru_inputs(x_ref, xp_ref, xn_ref, g_ref, wxl_ref, cw_ref, cb_ref, *, t):
    blk = pl.program_id(1)
    nb = pl.num_programs(1)
    d_lru = wxl_ref.shape[1]
    groups = t // SUBLANES
    xp = xp_ref[0] * jnp.where(blk > 0, 1.0, 0.0)
    xn = xn_ref[0] * jnp.where(blk < nb - 1, 1.0, 0.0)
    xe = jnp.concatenate([xp, x_ref[0], xn], axis=0)
    he = (xe * _rms_scale(xe) * g_ref[...]).astype(BF16)
    xl = jnp.dot(he, wxl_ref[...], preferred_element_type=F32)
    x3 = xl.reshape(groups + 2, SUBLANES, d_lru)
    sub = lax.broadcasted_iota(jnp.int32, (1, SUBLANES, d_lru), 1)

    def delayed(k):
        r = pltpu.roll(x3, k, axis=1)
        return jnp.where(sub >= k, r[1:groups + 1], r[0:groups])

    ahead = pltpu.roll(x3, SUBLANES - 1, axis=1)
    ahead = jnp.where(sub < SUBLANES - 1, ahead[1:groups + 1], ahead[2:groups + 2])
    cw = cw_ref[...]
    xc = (cb_ref[...] + cw[0:1] * delayed(2) + cw[1:2] * delayed(1) + cw[2:3] * x3[1:groups + 1]
          + cw[3:4] * ahead)
    return xc.reshape(t, d_lru)


def _lru_gates(xc, wg_ref, bg_ref):
    return [
        jnp.dot(xc[:, n * LRU_BLOCK:(n + 1) * LRU_BLOCK].astype(BF16), wg_ref[n], preferred_element_type=F32)
        + bg_ref[n]
        for n in range(N_LRU_BLOCKS)
    ]


def _lru_scan(xc, gate_pre, cneg_ref, a_scr, u_scr, carry_ref, write_rows, *, reverse, t):
    groups = t // SUBLANES
    sub = lax.broadcasted_iota(jnp.int32, (1, SUBLANES, LRU_BLOCK), 1)
    for n in range(N_LRU_BLOCKS):
        cols = slice(n * LRU_BLOCK, (n + 1) * LRU_BLOCK)
        xcn = xc[:, cols]
        gz = gate_pre[n]
        r = jax.nn.sigmoid(gz[:, :LRU_BLOCK])
        ig = jax.nn.sigmoid(gz[:, LRU_BLOCK:])
        a = jnp.exp(cneg_ref[:, cols] * r)
        y = (1.0 - a) * (1.0 + a)
        u = (y * lax.rsqrt(jnp.maximum(y, TINY))) * (ig * xcn)
        a3 = a.reshape(groups, SUBLANES, LRU_BLOCK)
        u3 = u.reshape(groups, SUBLANES, LRU_BLOCK)
        for dist in (1, 2, 4):
            if reverse:
                shift, ok = SUBLANES - dist, sub < SUBLANES - dist
            else:
                shift, ok = dist, sub >= dist
            a_nb = pltpu.roll(a3, shift, axis=1)
            u_nb = pltpu.roll(u3, shift, axis=1)
            u3 = jnp.where(ok, a3 * u_nb + u3, u3)
            a3 = jnp.where(ok, a3 * a_nb, a3)
        a_scr[:, :, cols] = a3
        u_scr[:, :, cols] = u3

    edge = 0 if reverse else SUBLANES - 1
    for n in range(N_LRU_BLOCKS):
        cols = slice(n * LRU_BLOCK, (n + 1) * LRU_BLOCK)
        hrow = carry_ref[:, cols]
        for j in range(groups):
            jj = groups - 1 - j if reverse else j
            rows = u_scr[jj, :, cols] + a_scr[jj, :, cols] * hrow
            write_rows(jj, cols, rows)
            hrow = jnp.broadcast_to(rows[edge:edge + 1, :], (SUBLANES, LRU_BLOCK))
        carry_ref[:, cols] = hrow


def _lru_f_kernel(x_ref, xp_ref, xn_ref, g_ref, wxl_ref, cw_ref, cb_ref, wg_ref, bg_ref, cneg_ref,
                  hf_ref, xc_ref, a_scr, u_scr, carry_ref, *, t):
    def write_rows(j, cols, rows):
        hf_ref[0, j, :, cols] = rows

    _reset_carry_at_sequence_start(carry_ref)
    xc = _lru_inputs(x_ref, xp_ref, xn_ref, g_ref, wxl_ref, cw_ref, cb_ref, t=t)
    xc_ref[0] = xc
    gate_pre = _lru_gates(xc, wg_ref, bg_ref)
    _lru_scan(xc, gate_pre, cneg_ref, a_scr, u_scr, carry_ref, write_rows, reverse=False, t=t)


def _merge_kernel(x_ref, g_ref, xc_ref, wg_ref, bg_ref, cneg_ref,
                  hf_ref, att_ref, wgl_ref, wga_ref, wm_ref, bm_ref, wb0_ref, wb1_ref, wo_ref, gfin_ref,
                  y_ref, a_scr, u_scr, carry_ref, hb_scr, *, t, final_norm):
    def write_rows(j, cols, rows):
        hb_scr[j, :, cols] = rows

    _reset_carry_at_sequence_start(carry_ref)
    d = x_ref.shape[2]
    x = x_ref[0]
    h = (x * _rms_scale(x) * g_ref[...]).astype(BF16)
    xc = xc_ref[0]
    gate_pre = _lru_gates(xc, wg_ref, bg_ref)
    _lru_scan(xc, gate_pre, cneg_ref, a_scr, u_scr, carry_ref, write_rows, reverse=True, t=t)
    g_l = jnp.dot(h, wgl_ref[...], preferred_element_type=F32)
    lsum = (hf_ref[0] + hb_scr[...]).reshape(t, xc.shape[1])
    l_out = (lsum * (g_l * jax.nn.sigmoid(g_l))).astype(BF16)
    g_a = jnp.dot(h, wga_ref[...], preferred_element_type=F32)
    a_out = (att_ref[0].astype(F32) * (g_a * jax.nn.sigmoid(g_a))).astype(BF16)
    a_proj = jnp.dot(a_out, wb0_ref[...], preferred_element_type=F32)
    l_proj = jnp.dot(l_out, wb1_ref[...], preferred_element_type=F32)
    gates = jax.nn.sigmoid(jnp.dot(h, wm_ref[...], preferred_element_type=F32) + bm_ref[...])
    merged = gates[:, :d] * a_proj + gates[:, d:] * l_proj
    y = x + jnp.dot(merged.astype(BF16), wo_ref[...], preferred_element_type=F32)
    if final_norm:
        y = y * _rms_scale(y) * gfin_ref[...]
    y_ref[0] = y


def _lru_specs(s, t, d, d_lru, reverse):
    nb = s // t
    groups = t // SUBLANES
    last_group = s // SUBLANES - 1

    def blk(i):
        return nb - 1 - i if reverse else i

    x_spec = pl.BlockSpec((1, t, d), lambda bi, i: (bi, blk(i), 0))
    xp_spec = pl.BlockSpec((1, HALO, d), lambda bi, i: (bi, jnp.maximum(blk(i) * groups - 1, 0), 0))
    xn_spec = pl.BlockSpec((1, HALO, d), lambda bi, i: (bi, jnp.minimum((blk(i) + 1) * groups, last_group), 0))
    hf_spec = pl.BlockSpec((1, groups, SUBLANES, d_lru), lambda bi, i: (bi, blk(i), 0, 0))
    scratch = [
        pltpu.VMEM((groups, SUBLANES, d_lru), F32),
        pltpu.VMEM((groups, SUBLANES, d_lru), F32),
        pltpu.VMEM((SUBLANES, d_lru), F32),
    ]
    return x_spec, xp_spec, xn_spec, hf_spec, scratch, blk


def _lru_f_call(x, g_in, w_xl, conv_w, conv_b, w_gate, b_gate, cneg):
    b, s, d = x.shape
    d_lru = w_xl.shape[1]
    t = min(LRU_ROWS, s)
    x_spec, xp_spec, xn_spec, hf_spec, scratch, _ = _lru_specs(s, t, d, d_lru, False)
    return pl.pallas_call(
        functools.partial(_lru_f_kernel, t=t),
        grid=(b, s // t),
        in_specs=[x_spec, xp_spec, xn_spec, _const_spec((1, d)), _const_spec(w_xl.shape),
                  _const_spec(conv_w.shape), _const_spec(conv_b.shape), _const_spec(w_gate.shape),
                  _const_spec(b_gate.shape), _const_spec(cneg.shape)],
        out_specs=[hf_spec, pl.BlockSpec((1, t, d_lru), lambda bi, i: (bi, i, 0))],
        out_shape=[jax.ShapeDtypeStruct((b, s // SUBLANES, SUBLANES, d_lru), F32),
                   jax.ShapeDtypeStruct((b, s, d_lru), F32)],
        scratch_shapes=scratch,
        compiler_params=_params(2, True),
        name="lru_f",
    )(x, x, x, g_in, w_xl, conv_w, conv_b, w_gate, b_gate, cneg)


def _merge_call(x, g_in, xc, w_gate, b_gate, cneg, hf, att, w_gl, w_ga, w_m, b_m, w_b0, w_b1, w_o, g_fin, final_norm):
    b, s, d = x.shape
    d_lru = xc.shape[2]
    t = min(LRU_ROWS, s)
    _, _, _, hf_spec, scratch, blk = _lru_specs(s, t, d, d_lru, True)
    tok = lambda width: pl.BlockSpec((1, t, width), lambda bi, i: (bi, blk(i), 0))
    consts = [w_gate, b_gate, cneg]
    tail = [w_gl, w_ga, w_m, b_m, w_b0, w_b1, w_o, g_fin]
    return pl.pallas_call(
        functools.partial(_merge_kernel, t=t, final_norm=final_norm),
        grid=(b, s // t),
        in_specs=[tok(d), _const_spec(g_in.shape), tok(d_lru)] + [_const_spec(c.shape) for c in consts]
        + [hf_spec, tok(att.shape[2])] + [_const_spec(c.shape) for c in tail],
        out_specs=tok(d),
        out_shape=jax.ShapeDtypeStruct((b, s, d), F32),
        scratch_shapes=scratch + [pltpu.VMEM((t // SUBLANES, SUBLANES, d_lru), F32)],
        compiler_params=_params(2, True),
        name="merge",
    )(x, g_in, xc, *consts, hf, att, *tail)


def _rope_tables(s, gain, scale):
    rows_n = s // GRID_W
    rows = jnp.repeat(jnp.arange(rows_n, dtype=F32), GRID_W)
    cols = jnp.tile(jnp.arange(GRID_W, dtype=F32), rows_n)
    n_pair_axis = HEAD_DIM // 4
    inv_freq = ROPE_THETA ** (-jnp.arange(n_pair_axis, dtype=F32) / n_pair_axis)
    ang = jnp.concatenate([rows[:, None] * inv_freq, cols[:, None] * inv_freq], axis=-1)
    cos = jnp.cos(ang).T
    sin = jnp.sin(ang).T
    c = jnp.concatenate([cos, cos], axis=0)
    sn = jnp.concatenate([-sin, sin], axis=0)
    gain = gain.astype(F32)
    return (gain[:, None] * c) * scale, (jnp.roll(gain, HALF)[:, None] * sn) * scale


def _layer_params(w_in, q_norm, k_norm, w_rgate, b_rgate, w_igate, b_igate, lam, w_branch, w_out, d):
    d_attn = N_HEADS * HEAD_DIM
    d_kv = N_KV_HEADS * HEAD_DIM
    d_lru = N_LRU_BLOCKS * LRU_BLOCK
    splits = [d_attn, d_kv, d_kv, d_attn, d_lru, d_lru]
    offs = [0]
    for w in splits:
        offs.append(offs[-1] + w)
    w_q, w_k, w_v, w_ga, w_xl, w_gl = (w_in[:, offs[j]:offs[j + 1]] for j in range(6))
    w_m = w_in[:, offs[6]:]
    perm = jnp.concatenate([jnp.arange(0, HEAD_DIM, 2), jnp.arange(1, HEAD_DIM, 2)])
    permute = lambda w, nh: w.reshape(d, nh, HEAD_DIM)[:, :, perm].reshape(d, nh * HEAD_DIM)
    w_qkvT = jnp.concatenate([permute(w_q, N_HEADS), permute(w_k, N_KV_HEADS), w_v], axis=1).T.astype(BF16)
    w_gate = jnp.concatenate([w_rgate, w_igate], axis=-1).astype(BF16)
    b_gate = jnp.concatenate([b_rgate.reshape(2, N_LRU_BLOCKS, 1, LRU_BLOCK),
                              b_igate.reshape(2, N_LRU_BLOCKS, 1, LRU_BLOCK)], axis=-1).astype(F32)
    cneg = (-RG_C * jax.nn.softplus(-lam.astype(F32))).reshape(2, 1, d_lru)
    return dict(
        w_qkvT=w_qkvT, gq=q_norm[perm], gk=k_norm[perm],
        w_ga=w_ga.astype(BF16), w_xl=w_xl.astype(BF16), w_gl=w_gl.astype(BF16), w_m=w_m.astype(BF16),
        w_gate=w_gate, b_gate=b_gate, cneg=cneg,
        w_b0=w_branch[0].astype(BF16), w_b1=w_branch[1].astype(BF16), w_o=w_out.astype(BF16),
    )


def _layer(x, p, g_in, conv_w, conv_b, b_m, g_fin, final_norm):
    s = x.shape[1]
    q_scale = math.log2(math.e) / math.sqrt(HEAD_DIM)
    cq, sq = _rope_tables(s, p["gq"], q_scale)
    ck, sk = _rope_tables(s, p["gk"], 1.0)
    qT, k, vT = _qkv_call(x, g_in, p["w_qkvT"], cq, sq, ck, sk)
    att = _attn_call(qT, k, vT)
    lru = (g_in, p["w_xl"], conv_w, conv_b)
    hf, xc = _lru_f_call(x, *lru, p["w_gate"][0], p["b_gate"][0], p["cneg"][0])
    return _merge_call(x, g_in, xc, p["w_gate"][1], p["b_gate"][1], p["cneg"][1], hf, att,
                       p["w_gl"], p["w_ga"], p["w_m"], b_m, p["w_b0"], p["w_b1"], p["w_o"], g_fin, final_norm)


def kernel(x_prompt, x_sample, norm_in, w_in, b_merge, q_norm, k_norm, conv_w, conv_b, w_rgate, b_rgate,
           w_igate, b_igate, lam, w_branch, w_out, norm_final):
    depth, d = norm_in.shape
    layers = [
        _layer_params(w_in[l], q_norm[l], k_norm[l], w_rgate[l], b_rgate[l], w_igate[l], b_igate[l], lam[l],
                      w_branch[l], w_out[l], d)
        for l in range(depth)
    ]
    g_fin = norm_final.reshape(1, d).astype(F32)
    outs = []
    for x in (x_prompt, x_sample):
        for l in range(depth):
            x = _layer(x, layers[l], norm_in[l].reshape(1, d).astype(F32), conv_w[l].astype(F32),
                       conv_b[l].reshape(1, -1).astype(F32), b_merge[l].reshape(1, -1).astype(F32), g_fin,
                       l == depth - 1)
        outs.append(x)
    return tuple(outs)
```

```python
import functools
import math

import jax
import jax.numpy as jnp
from jax import lax
from jax.experimental import pallas as pl
from jax.experimental.pallas import tpu as pltpu

N_HEADS = 8
N_KV_HEADS = 2
GROUP = N_HEADS // N_KV_HEADS
HEAD_DIM = 128
HALF = HEAD_DIM // 2
N_LRU_BLOCKS = 8
LRU_BLOCK = 128
CONV_W = 4
RG_C = 8.0
EPS = 1e-6
TINY = 1e-30
GRID_W = 64
ROPE_THETA = 10000.0

SUBLANES = 8
ONES_ROWS = 2 * SUBLANES
HALO = SUBLANES
V7X_VMEM_LIMIT = 56 * 1024 * 1024

QKV_ROWS = 512
ATTN_Q = 512
ATTN_K = 256
AHEAD = 2
BEHIND = 0
LRU_ROWS = 256

F32 = jnp.float32
BF16 = jnp.bfloat16


def _const_spec(shape):
    nd = len(shape)
    return pl.BlockSpec(shape, lambda *_: (0,) * nd, pipeline_mode=pl.Buffered(1))


def _params(n_axes, sequential_last, flags=None):
    sem = ["parallel"] * n_axes
    if sequential_last:
        sem[-1] = "arbitrary"
    return pltpu.CompilerParams(dimension_semantics=tuple(sem), vmem_limit_bytes=V7X_VMEM_LIMIT, flags=flags)


def _rms_scale(x):
    return lax.rsqrt(jnp.mean(x * x, axis=-1, keepdims=True) + EPS)


def _qkv_kernel(x_ref, g_ref, w_ref, cq_ref, sq_ref, ck_ref, sk_ref, qT_ref, k_ref, vT_ref):
    x = x_ref[0]
    h = (x * _rms_scale(x) * g_ref[...]).astype(BF16)
    zT = lax.dot_general(w_ref[...], h, (((1,), (1,)), ((), ())), preferred_element_type=F32)

    def norm_rope(z, c, s):
        n = z * lax.rsqrt(jnp.mean(z * z, axis=0, keepdims=True) + EPS)
        swapped = jnp.concatenate([n[HALF:], n[:HALF]], axis=0)
        return n * c + swapped * s

    cq, sq = cq_ref[...], sq_ref[...]
    for hd in range(N_HEADS):
        rows = slice(hd * HEAD_DIM, (hd + 1) * HEAD_DIM)
        qh = norm_rope(zT[rows], cq, sq).astype(BF16)
        tq = qT_ref.shape[3]
        for qb in range(qT_ref.shape[1]):
            qT_ref[0, qb, rows, :] = qh[:, qb * tq:(qb + 1) * tq]
    ck, sk = ck_ref[...], sk_ref[...]
    k_off = N_HEADS * HEAD_DIM
    for kv in range(N_KV_HEADS):
        rows = slice(k_off + kv * HEAD_DIM, k_off + (kv + 1) * HEAD_DIM)
        kT = norm_rope(zT[rows], ck, sk)
        k_ref[0, :, kv * HEAD_DIM:(kv + 1) * HEAD_DIM] = kT.T.astype(BF16)
    v_off = k_off + N_KV_HEADS * HEAD_DIM
    vT_ref[0] = zT[v_off:v_off + N_KV_HEADS * HEAD_DIM].astype(BF16)


def _qkv_call(x, g_in, w_qkvT, cq, sq, ck, sk):
    b, s, d = x.shape
    t = min(QKV_ROWS, s)
    tq = min(ATTN_Q, t)
    n_q, n_kv = N_HEADS * HEAD_DIM, N_KV_HEADS * HEAD_DIM
    tab = pl.BlockSpec((HEAD_DIM, t), lambda bi, i: (0, i))
    return pl.pallas_call(
        _qkv_kernel,
        grid=(b, s // t),
        in_specs=[
            pl.BlockSpec((1, t, d), lambda bi, i: (bi, i, 0)),
            _const_spec((1, d)),
            _const_spec(w_qkvT.shape),
            tab, tab, tab, tab,
        ],
        out_specs=[
            pl.BlockSpec((1, t // tq, n_q, tq), lambda bi, i: (bi, i, 0, 0)),
            pl.BlockSpec((1, t, n_kv), lambda bi, i: (bi, i, 0)),
            pl.BlockSpec((1, n_kv, t), lambda bi, i: (bi, 0, i)),
        ],
        out_shape=[
            jax.ShapeDtypeStruct((b, s // tq, n_q, tq), BF16),
            jax.ShapeDtypeStruct((b, s, n_kv), BF16),
            jax.ShapeDtypeStruct((b, n_kv, s), BF16),
        ],
        compiler_params=_params(2, False),
        name="qkv",
    )(x, g_in, w_qkvT, cq, sq, ck, sk)


def _attn_kernel(zero_ref, qT_ref, k_ref, vT_ref, o_ref, m_ref, alpha_ref, acc_ref, cmax_ref, *slots, n_chunks, tk):
    m_ref[...] = jnp.full(m_ref.shape, -jnp.inf, F32)
    acc_ref[...] = jnp.zeros(acc_ref.shape, F32)
    ones_rows = jnp.ones((ONES_ROWS, tk), BF16)
    s_slots, p_slots = slots[:N_HEADS], slots[N_HEADS:]
    zw, zr = zero_ref[0], zero_ref[1]

    def kv_cols(hd):
        kv = hd // GROUP
        return slice(kv * HEAD_DIM, (kv + 1) * HEAD_DIM)

    def scores(c, hd):
        off = pl.multiple_of(c * tk, tk)
        kc = k_ref[0, pl.ds(off, tk), kv_cols(hd)]
        qT = qT_ref[0, 0, hd * HEAD_DIM:(hd + 1) * HEAD_DIM, :]
        s = jnp.dot(kc, qT, preferred_element_type=F32)
        s_slots[hd][zw] = s
        cmax_ref[hd] = jnp.max(s, axis=0, keepdims=True)

    def softmax(hd):
        m_old = m_ref[hd:hd + 1, :]
        m_new = jnp.maximum(m_old, cmax_ref[hd])
        m_ref[hd:hd + 1, :] = m_new
        return jnp.exp2(m_old - m_new), jnp.exp2(s_slots[hd][zr] - m_new).astype(BF16)

    def update(c, hd, alpha, p):
        off = pl.multiple_of(c * tk, tk)
        vc = jnp.concatenate([vT_ref[0, kv_cols(hd), pl.ds(off, tk)], ones_rows], axis=0)
        acc_ref[hd] = alpha * acc_ref[hd] + jnp.dot(vc, p, preferred_element_type=F32)

    def update_from_slot(c, hd):
        update(c, hd, alpha_ref[hd:hd + 1, :], p_slots[hd][zr])

    def item(c, hd, shift):
        w = hd + shift
        return c + w // N_HEADS, w % N_HEADS

    for hd in range(N_HEADS - BEHIND, N_HEADS):
        p_slots[hd][zw] = jnp.zeros(p_slots[hd].shape[1:], BF16)
        alpha_ref[hd:hd + 1, :] = jnp.ones((1, alpha_ref.shape[1]), F32)
    for hd in range(AHEAD):
        scores(0, hd)

    def step(c, carry):
        for hd in range(N_HEADS):
            cs, hs = item(c, hd, AHEAD)
            scores(jnp.minimum(cs, n_chunks - 1), hs)
            alpha, p = softmax(hd)
            if BEHIND == 0:
                update(c, hd, alpha, p)
            else:
                alpha_ref[hd:hd + 1, :] = alpha
                p_slots[hd][zw] = p
                cu, hu = item(c, hd, -BEHIND)
                update_from_slot(jnp.maximum(cu, 0), hu)
        return carry

    lax.fori_loop(0, n_chunks, step, 0, unroll=4)
    for hd in range(N_HEADS - BEHIND, N_HEADS):
        update_from_slot(n_chunks - 1, hd)

    for hd in range(N_HEADS):
        o = acc_ref[hd, :HEAD_DIM, :] * (1.0 / acc_ref[hd, HEAD_DIM:HEAD_DIM + 1, :])
        o_ref[0, :, hd * HEAD_DIM:(hd + 1) * HEAD_DIM] = o.T.astype(BF16)


def _attn_call(qT, k, vT):
    b, n_qb, n_q, tq = qT.shape
    s, n_kv = k.shape[1:]
    tk = min(ATTN_K, s)
    kernel = functools.partial(_attn_kernel, n_chunks=s // tk, tk=tk)
    return pl.pallas_call(
        kernel,
        grid=(b, n_qb),
        in_specs=[
            pl.BlockSpec(memory_space=pltpu.SMEM),
            pl.BlockSpec((1, 1, n_q, tq), lambda bi, i: (bi, i, 0, 0)),
            pl.BlockSpec((1, s, n_kv), lambda bi, i: (bi, 0, 0), pipeline_mode=pl.Buffered(1)),
            pl.BlockSpec((1, n_kv, s), lambda bi, i: (bi, 0, 0), pipeline_mode=pl.Buffered(1)),
        ],
        out_specs=pl.BlockSpec((1, tq, n_q), lambda bi, i: (bi, i, 0)),
        out_shape=jax.ShapeDtypeStruct((b, s, n_q), BF16),
        scratch_shapes=[
            pltpu.VMEM((N_HEADS, tq), F32),
            pltpu.VMEM((N_HEADS, tq), F32),
            pltpu.VMEM((N_HEADS, HEAD_DIM + ONES_ROWS, tq), F32),
            pltpu.VMEM((N_HEADS, 1, tq), F32),
        ] + [pltpu.VMEM((1, tk, tq), F32)] * N_HEADS
        + [pltpu.VMEM((1, tk, tq), BF16)] * N_HEADS,
        compiler_params=_params(2, True),
        name="attn",
    )(jnp.zeros((2,), jnp.int32), qT, k, vT)


def _reset_carry_at_sequence_start(carry_ref):
    @pl.when(pl.program_id(1) == 0)
    def _():
        carry_ref[...] = jnp.zeros(carry_ref.shape, F32)


def _lru_inputs(x_ref, xp_ref, xn_ref, g_ref, wxl_ref, cw_ref, cb_ref, *, t):
    blk = pl.program_id(1)
    nb = pl.num_programs(1)
    d_lru = wxl_ref.shape[1]
    groups = t // SUBLANES
    xp = xp_ref[0] * jnp.where(blk > 0, 1.0, 0.0)
    xn = xn_ref[0] * jnp.where(blk < nb - 1, 1.0, 0.0)
    xe = jnp.concatenate([xp, x_ref[0], xn], axis=0)
    he = (xe * _rms_scale(xe) * g_ref[...]).astype(BF16)
    xl = jnp.dot(he, wxl_ref[...], preferred_element_type=F32)
    x3 = xl.reshape(groups + 2, SUBLANES, d_lru)
    sub = lax.broadcasted_iota(jnp.int32, (1, SUBLANES, d_lru), 1)

    def delayed(k):
        r = pltpu.roll(x3, k, axis=1)
        return jnp.where(sub >= k, r[1:groups + 1], r[0:groups])

    ahead = pltpu.roll(x3, SUBLANES - 1, axis=1)
    ahead = jnp.where(sub < SUBLANES - 1, ahead[1:groups + 1], ahead[2:groups + 2])
    cw = cw_ref[...]
    xc = (cb_ref[...] + cw[0:1] * delayed(2) + cw[1:2] * delayed(1) + cw[2:3] * x3[1:groups + 1]
          + cw[3:4] * ahead)
    return xc.reshape(t, d_lru)


def _lru_gates(xc, wg_ref, bg_ref):
    return [
        jnp.dot(xc[:, n * LRU_BLOCK:(n + 1) * LRU_BLOCK].astype(BF16), wg_ref[n], preferred_element_type=F32)
        + bg_ref[n]
        for n in range(N_LRU_BLOCKS)
    ]


def _lru_scan(xc, gate_pre, cneg_ref, a_scr, u_scr, carry_ref, write_rows, *, reverse, t):
    groups = t // SUBLANES
    sub = lax.broadcasted_iota(jnp.int32, (1, SUBLANES, LRU_BLOCK), 1)
    for n in range(N_LRU_BLOCKS):
        cols = slice(n * LRU_BLOCK, (n + 1) * LRU_BLOCK)
        xcn = xc[:, cols]
        gz = gate_pre[n]
        r = jax.nn.sigmoid(gz[:, :LRU_BLOCK])
        ig = jax.nn.sigmoid(gz[:, LRU_BLOCK:])
        a = jnp.exp(cneg_ref[:, cols] * r)
        y = (1.0 - a) * (1.0 + a)
        u = (y * lax.rsqrt(jnp.maximum(y, TINY))) * (ig * xcn)
        a3 = a.reshape(groups, SUBLANES, LRU_BLOCK)
        u3 = u.reshape(groups, SUBLANES, LRU_BLOCK)
        for dist in (1, 2, 4):
            if reverse:
                shift, ok = SUBLANES - dist, sub < SUBLANES - dist
            else:
                shift, ok = dist, sub >= dist
            a_nb = pltpu.roll(a3, shift, axis=1)
            u_nb = pltpu.roll(u3, shift, axis=1)
            u3 = jnp.where(ok, a3 * u_nb + u3, u3)
            a3 = jnp.where(ok, a3 * a_nb, a3)
        a_scr[:, :, cols] = a3
        u_scr[:, :, cols] = u3

    edge = 0 if reverse else SUBLANES - 1
    for n in range(N_LRU_BLOCKS):
        cols = slice(n * LRU_BLOCK, (n + 1) * LRU_BLOCK)
        hrow = carry_ref[:, cols]
        for j in range(groups):
            jj = groups - 1 - j if reverse else j
            rows = u_scr[jj, :, cols] + a_scr[jj, :, cols] * hrow
            write_rows(jj, cols, rows)
            hrow = jnp.broadcast_to(rows[edge:edge + 1, :], (SUBLANES, LRU_BLOCK))
        carry_ref[:, cols] = hrow


def _lru_f_kernel(x_ref, xp_ref, xn_ref, g_ref, wxl_ref, cw_ref, cb_ref, wg_ref, bg_ref, cneg_ref,
                  hf_ref, xc_ref, a_scr, u_scr, carry_ref, *, t):
    def write_rows(j, cols, rows):
        hf_ref[0, j, :, cols] = rows

    _reset_carry_at_sequence_start(carry_ref)
    xc = _lru_inputs(x_ref, xp_ref, xn_ref, g_ref, wxl_ref, cw_ref, cb_ref, t=t)
    xc_ref[0] = xc
    gate_pre = _lru_gates(xc, wg_ref, bg_ref)
    _lru_scan(xc, gate_pre, cneg_ref, a_scr, u_scr, carry_ref, write_rows, reverse=False, t=t)


def _merge_kernel(x_ref, g_ref, xc_ref, wg_ref, bg_ref, cneg_ref,
                  hf_ref, att_ref, wgl_ref, wga_ref, wm_ref, bm_ref, wb0_ref, wb1_ref, wo_ref, gfin_ref,
                  y_ref, a_scr, u_scr, carry_ref, hb_scr, *, t, final_norm):
    def write_rows(j, cols, rows):
        hb_scr[j, :, cols] = rows

    _reset_carry_at_sequence_start(carry_ref)
    d = x_ref.shape[2]
    x = x_ref[0]
    h = (x * _rms_scale(x) * g_ref[...]).astype(BF16)
    xc = xc_ref[0]
    gate_pre = _lru_gates(xc, wg_ref, bg_ref)
    _lru_scan(xc, gate_pre, cneg_ref, a_scr, u_scr, carry_ref, write_rows, reverse=True, t=t)
    g_l = jnp.dot(h, wgl_ref[...], preferred_element_type=F32)
    lsum = (hf_ref[0] + hb_scr[...]).reshape(t, xc.shape[1])
    l_out = (lsum * (g_l * jax.nn.sigmoid(g_l))).astype(BF16)
    g_a = jnp.dot(h, wga_ref[...], preferred_element_type=F32)
    a_out = (att_ref[0].astype(F32) * (g_a * jax.nn.sigmoid(g_a))).astype(BF16)
    a_proj = jnp.dot(a_out, wb0_ref[...], preferred_element_type=F32)
    l_proj = jnp.dot(l_out, wb1_ref[...], preferred_element_type=F32)
    gates = jax.nn.sigmoid(jnp.dot(h, wm_ref[...], preferred_element_type=F32) + bm_ref[...])
    merged = gates[:, :d] * a_proj + gates[:, d:] * l_proj
    y = x + jnp.dot(merged.astype(BF16), wo_ref[...], preferred_element_type=F32)
    if final_norm:
        y = y * _rms_scale(y) * gfin_ref[...]
    y_ref[0] = y


def _lru_specs(s, t, d, d_lru, reverse):
    nb = s // t
    groups = t // SUBLANES
    last_group = s // SUBLANES - 1

    def blk(i):
        return nb - 1 - i if reverse else i

    x_spec = pl.BlockSpec((1, t, d), lambda bi, i: (bi, blk(i), 0))
    xp_spec = pl.BlockSpec((1, HALO, d), lambda bi, i: (bi, jnp.maximum(blk(i) * groups - 1, 0), 0))
    xn_spec = pl.BlockSpec((1, HALO, d), lambda bi, i: (bi, jnp.minimum((blk(i) + 1) * groups, last_group), 0))
    hf_spec = pl.BlockSpec((1, groups, SUBLANES, d_lru), lambda bi, i: (bi, blk(i), 0, 0))
    scratch = [
        pltpu.VMEM((groups, SUBLANES, d_lru), F32),
        pltpu.VMEM((groups, SUBLANES, d_lru), F32),
        pltpu.VMEM((SUBLANES, d_lru), F32),
    ]
    return x_spec, xp_spec, xn_spec, hf_spec, scratch, blk


def _lru_f_call(x, g_in, w_xl, conv_w, conv_b, w_gate, b_gate, cneg):
    b, s, d = x.shape
    d_lru = w_xl.shape[1]
    t = min(LRU_ROWS, s)
    x_spec, xp_spec, xn_spec, hf_spec, scratch, _ = _lru_specs(s, t, d, d_lru, False)
    return pl.pallas_call(
        functools.partial(_lru_f_kernel, t=t),
        grid=(b, s // t),
        in_specs=[x_spec, xp_spec, xn_spec, _const_spec((1, d)), _const_spec(w_xl.shape),
                  _const_spec(conv_w.shape), _const_spec(conv_b.shape), _const_spec(w_gate.shape),
                  _const_spec(b_gate.shape), _const_spec(cneg.shape)],
        out_specs=[hf_spec, pl.BlockSpec((1, t, d_lru), lambda bi, i: (bi, i, 0))],
        out_shape=[jax.ShapeDtypeStruct((b, s // SUBLANES, SUBLANES, d_lru), F32),
                   jax.ShapeDtypeStruct((b, s, d_lru), F32)],
        scratch_shapes=scratch,
        compiler_params=_params(2, True),
        name="lru_f",
    )(x, x, x, g_in, w_xl, conv_w, conv_b, w_gate, b_gate, cneg)


def _merge_call(x, g_in, xc, w_gate, b_gate, cneg, hf, att, w_gl, w_ga, w_m, b_m, w_b0, w_b1, w_o, g_fin, final_norm):
    b, s, d = x.shape
    d_lru = xc.shape[2]
    t = min(LRU_ROWS, s)
    _, _, _, hf_spec, scratch, blk = _lru_specs(s, t, d, d_lru, True)
    tok = lambda width: pl.BlockSpec((1, t, width), lambda bi, i: (bi, blk(i), 0))
    consts = [w_gate, b_gate, cneg]
    tail = [w_gl, w_ga, w_m, b_m, w_b0, w_b1, w_o, g_fin]
    return pl.pallas_call(
        functools.partial(_merge_kernel, t=t, final_norm=final_norm),
        grid=(b, s // t),
        in_specs=[tok(d), _const_spec(g_in.shape), tok(d_lru)] + [_const_spec(c.shape) for c in consts]
        + [hf_spec, tok(att.shape[2])] + [_const_spec(c.shape) for c in tail],
        out_specs=tok(d),
        out_shape=jax.ShapeDtypeStruct((b, s, d), F32),
        scratch_shapes=scratch + [pltpu.VMEM((t // SUBLANES, SUBLANES, d_lru), F32)],
        compiler_params=_params(2, True),
        name="merge",
    )(x, g_in, xc, *consts, hf, att, *tail)


def _rope_tables(s, gain, scale):
    rows_n = s // GRID_W
    rows = jnp.repeat(jnp.arange(rows_n, dtype=F32), GRID_W)
    cols = jnp.tile(jnp.arange(GRID_W, dtype=F32), rows_n)
    n_pair_axis = HEAD_DIM // 4
    inv_freq = ROPE_THETA ** (-jnp.arange(n_pair_axis, dtype=F32) / n_pair_axis)
    ang = jnp.concatenate([rows[:, None] * inv_freq, cols[:, None] * inv_freq], axis=-1)
    cos = jnp.cos(ang).T
    sin = jnp.sin(ang).T
    c = jnp.concatenate([cos, cos], axis=0)
    sn = jnp.concatenate([-sin, sin], axis=0)
    gain = gain.astype(F32)
    return (gain[:, None] * c) * scale, (jnp.roll(gain, HALF)[:, None] * sn) * scale


def _layer_params(w_in, q_norm, k_norm, w_rgate, b_rgate, w_igate, b_igate, lam, w_branch, w_out, d):
    d_attn = N_HEADS * HEAD_DIM
    d_kv = N_KV_HEADS * HEAD_DIM
    d_lru = N_LRU_BLOCKS * LRU_BLOCK
    splits = [d_attn, d_kv, d_kv, d_attn, d_lru, d_lru]
    offs = [0]
    for w in splits:
        offs.append(offs[-1] + w)
    w_q, w_k, w_v, w_ga, w_xl, w_gl = (w_in[:, offs[j]:offs[j + 1]] for j in range(6))
    w_m = w_in[:, offs[6]:]
    perm = jnp.concatenate([jnp.arange(0, HEAD_DIM, 2), jnp.arange(1, HEAD_DIM, 2)])
    permute = lambda w, nh: w.reshape(d, nh, HEAD_DIM)[:, :, perm].reshape(d, nh * HEAD_DIM)
    w_qkvT = jnp.concatenate([permute(w_q, N_HEADS), permute(w_k, N_KV_HEADS), w_v], axis=1).T.astype(BF16)
    w_gate = jnp.concatenate([w_rgate, w_igate], axis=-1).astype(BF16)
    b_gate = jnp.concatenate([b_rgate.reshape(2, N_LRU_BLOCKS, 1, LRU_BLOCK),
                              b_igate.reshape(2, N_LRU_BLOCKS, 1, LRU_BLOCK)], axis=-1).astype(F32)
    cneg = (-RG_C * jax.nn.softplus(-lam.astype(F32))).reshape(2, 1, d_lru)
    return dict(
        w_qkvT=w_qkvT, gq=q_norm[perm], gk=k_norm[perm],
        w_ga=w_ga.astype(BF16), w_xl=w_xl.astype(BF16), w_gl=w_gl.astype(BF16), w_m=w_m.astype(BF16),
        w_gate=w_gate, b_gate=b_gate, cneg=cneg,
        w_b0=w_branch[0].astype(BF16), w_b1=w_branch[1].astype(BF16), w_o=w_out.astype(BF16),
    )


def _layer(x, p, g_in, conv_w, conv_b, b_m, g_fin, final_norm):
    s = x.shape[1]
    q_scale = math.log2(math.e) / math.sqrt(HEAD_DIM)
    cq, sq = _rope_tables(s, p["gq"], q_scale)
    ck, sk = _rope_tables(s, p["gk"], 1.0)
    qT, k, vT = _qkv_call(x, g_in, p["w_qkvT"], cq, sq, ck, sk)
    att = _attn_call(qT, k, vT)
    lru = (g_in, p["w_xl"], conv_w, conv_b)
    hf, xc = _lru_f_call(x, *lru, p["w_gate"][0], p["b_gate"][0], p["cneg"][0])
    return _merge_call(x, g_in, xc, p["w_gate"][1], p["b_gate"][1], p["cneg"][1], hf, att,
                       p["w_gl"], p["w_ga"], p["w_m"], b_m, p["w_b0"], p["w_b1"], p["w_o"], g_fin, final_norm)


def kernel(x_prompt, x_sample, norm_in, w_in, b_merge, q_norm, k_norm, conv_w, conv_b, w_rgate, b_rgate,
           w_igate, b_igate, lam, w_branch, w_out, norm_final):
    depth, d = norm_in.shape
    layers = [
        _layer_params(w_in[l], q_norm[l], k_norm[l], w_rgate[l], b_rgate[l], w_igate[l], b_igate[l], lam[l],
                      w_branch[l], w_out[l], d)
        for l in range(depth)
    ]
    g_fin = norm_final.reshape(1, d).astype(F32)
    outs = []
    for x in (x_prompt, x_sample):
        for l in range(depth):
            x = _layer(x, layers[l], norm_in[l].reshape(1, d).astype(F32), conv_w[l].astype(F32),
                       conv_b[l].reshape(1, -1).astype(F32), b_merge[l].reshape(1, -1).astype(F32), g_fin,
                       l == depth - 1)
        outs.append(x)
    return tuple(outs)
```

```python
import functools
import math

import jax
import jax.numpy as jnp
from jax import lax
from jax.experimental import pallas as pl
from jax.experimental.pallas import tpu as pltpu

N_HEADS = 8
N_KV_HEADS = 2
GROUP = N_HEADS // N_KV_HEADS
HEAD_DIM = 128
HALF = HEAD_DIM // 2
N_LRU_BLOCKS = 8
LRU_BLOCK = 128
CONV_W = 4
RG_C = 8.0
EPS = 1e-6
TINY = 1e-30
GRID_W = 64
ROPE_THETA = 10000.0

SUBLANES = 8
ONES_ROWS = 2 * SUBLANES
HALO = SUBLANES
V7X_VMEM_LIMIT = 56 * 1024 * 1024

QKV_ROWS = 512
ATTN_Q = 512
ATTN_K = 256
AHEAD = 2
LRU_ROWS = 256

F32 = jnp.float32
BF16 = jnp.bfloat16


def _const_spec(shape):
    nd = len(shape)
    return pl.BlockSpec(shape, lambda *_: (0,) * nd, pipeline_mode=pl.Buffered(1))


def _params(n_axes, sequential_last, all_sequential=False):
    sem = ["arbitrary" if all_sequential else "parallel"] * n_axes
    if sequential_last:
        sem[-1] = "arbitrary"
    return pltpu.CompilerParams(dimension_semantics=tuple(sem), vmem_limit_bytes=V7X_VMEM_LIMIT)


def _rms_scale(x):
    return lax.rsqrt(jnp.mean(x * x, axis=-1, keepdims=True) + EPS)


def _qkv_kernel(x_ref, g_ref, w_ref, cq_ref, sq_ref, ck_ref, sk_ref, qT_ref, k_ref, vT_ref):
    x = x_ref[0]
    h = (x * _rms_scale(x) * g_ref[...]).astype(BF16)
    zT = lax.dot_general(w_ref[...], h, (((1,), (1,)), ((), ())), preferred_element_type=F32)

    def norm_rope(z, c, s):
        n = z * lax.rsqrt(jnp.mean(z * z, axis=0, keepdims=True) + EPS)
        swapped = jnp.concatenate([n[HALF:], n[:HALF]], axis=0)
        return n * c + swapped * s

    cq, sq = cq_ref[...], sq_ref[...]
    for hd in range(N_HEADS):
        rows = slice(hd * HEAD_DIM, (hd + 1) * HEAD_DIM)
        qh = norm_rope(zT[rows], cq, sq).astype(BF16)
        tq = qT_ref.shape[3]
        for qb in range(qT_ref.shape[1]):
            qT_ref[0, qb, rows, :] = qh[:, qb * tq:(qb + 1) * tq]
    ck, sk = ck_ref[...], sk_ref[...]
    k_off = N_HEADS * HEAD_DIM
    for kv in range(N_KV_HEADS):
        rows = slice(k_off + kv * HEAD_DIM, k_off + (kv + 1) * HEAD_DIM)
        kT = norm_rope(zT[rows], ck, sk)
        k_ref[0, :, kv * HEAD_DIM:(kv + 1) * HEAD_DIM] = kT.T.astype(BF16)
    v_off = k_off + N_KV_HEADS * HEAD_DIM
    vT_ref[0] = zT[v_off:v_off + N_KV_HEADS * HEAD_DIM].astype(BF16)


def _qkv_call(x, g_in, w_qkvT, cq, sq, ck, sk):
    b, s, d = x.shape
    t = min(QKV_ROWS, s)
    tq = min(ATTN_Q, t)
    n_q, n_kv = N_HEADS * HEAD_DIM, N_KV_HEADS * HEAD_DIM
    tab = pl.BlockSpec((HEAD_DIM, t), lambda bi, i: (0, i))
    return pl.pallas_call(
        _qkv_kernel,
        grid=(b, s // t),
        in_specs=[
            pl.BlockSpec((1, t, d), lambda bi, i: (bi, i, 0)),
            _const_spec((1, d)),
            _const_spec(w_qkvT.shape),
            tab, tab, tab, tab,
        ],
        out_specs=[
            pl.BlockSpec((1, t // tq, n_q, tq), lambda bi, i: (bi, i, 0, 0)),
            pl.BlockSpec((1, t, n_kv), lambda bi, i: (bi, i, 0)),
            pl.BlockSpec((1, n_kv, t), lambda bi, i: (bi, 0, i)),
        ],
        out_shape=[
            jax.ShapeDtypeStruct((b, s // tq, n_q, tq), BF16),
            jax.ShapeDtypeStruct((b, s, n_kv), BF16),
            jax.ShapeDtypeStruct((b, n_kv, s), BF16),
        ],
        compiler_params=_params(2, False),
        name="qkv",
    )(x, g_in, w_qkvT, cq, sq, ck, sk)


def _attn_kernel(zero_ref, qT_ref, k_ref, vT_ref, o_ref, m_ref, acc_ref, cmax_ref, *s_slots, n_chunks, tk):
    @pl.when((pl.program_id(0) == 0) & (pl.program_id(1) == 0))
    def _():
        acc_ref[...] = jnp.zeros(acc_ref.shape, F32)

    m_ref[...] = jnp.full(m_ref.shape, -jnp.inf, F32)
    ones_rows = jnp.ones((ONES_ROWS, tk), BF16)
    zw, zr = zero_ref[0], zero_ref[1]

    def kv_cols(hd):
        kv = hd // GROUP
        return slice(kv * HEAD_DIM, (kv + 1) * HEAD_DIM)

    def scores(c, hd):
        off = pl.multiple_of(c * tk, tk)
        kc = k_ref[0, pl.ds(off, tk), kv_cols(hd)]
        qT = qT_ref[0, 0, hd * HEAD_DIM:(hd + 1) * HEAD_DIM, :]
        s = jnp.dot(kc, qT, preferred_element_type=F32)
        s_slots[hd][zw] = s
        cmax_ref[hd] = jnp.max(s, axis=0, keepdims=True)

    def accumulate(c, hd):
        off = pl.multiple_of(c * tk, tk)
        m_old = m_ref[hd:hd + 1, :]
        m_new = jnp.maximum(m_old, cmax_ref[hd])
        m_ref[hd:hd + 1, :] = m_new
        p = jnp.exp2(s_slots[hd][zr] - m_new).astype(BF16)
        vc = jnp.concatenate([vT_ref[0, kv_cols(hd), pl.ds(off, tk)], ones_rows], axis=0)
        acc_ref[hd] = jnp.exp2(m_old - m_new) * acc_ref[hd] + jnp.dot(vc, p, preferred_element_type=F32)

    for hd in range(AHEAD):
        scores(0, hd)

    def step(c, carry):
        for hd in range(N_HEADS):
            w = hd + AHEAD
            scores(jnp.minimum(c + w // N_HEADS, n_chunks - 1), w % N_HEADS)
            accumulate(c, hd)
        return carry

    lax.fori_loop(0, n_chunks, step, 0, unroll=4)

    for hd in range(N_HEADS):
        o = acc_ref[hd, :HEAD_DIM, :] * (1.0 / acc_ref[hd, HEAD_DIM:HEAD_DIM + 1, :])
        o_ref[0, :, hd * HEAD_DIM:(hd + 1) * HEAD_DIM] = o.T.astype(BF16)


def _attn_call(qT, k, vT):
    b, n_qb, n_q, tq = qT.shape
    s, n_kv = k.shape[1:]
    tk = min(ATTN_K, s)
    kernel = functools.partial(_attn_kernel, n_chunks=s // tk, tk=tk)
    return pl.pallas_call(
        kernel,
        grid=(b, n_qb),
        in_specs=[
            pl.BlockSpec(memory_space=pltpu.SMEM),
            pl.BlockSpec((1, 1, n_q, tq), lambda bi, i: (bi, i, 0, 0)),
            pl.BlockSpec((1, s, n_kv), lambda bi, i: (bi, 0, 0), pipeline_mode=pl.Buffered(1)),
            pl.BlockSpec((1, n_kv, s), lambda bi, i: (bi, 0, 0), pipeline_mode=pl.Buffered(1)),
        ],
        out_specs=pl.BlockSpec((1, tq, n_q), lambda bi, i: (bi, i, 0)),
        out_shape=jax.ShapeDtypeStruct((b, s, n_q), BF16),
        scratch_shapes=[
            pltpu.VMEM((N_HEADS, tq), F32),
            pltpu.VMEM((N_HEADS, HEAD_DIM + ONES_ROWS, tq), F32),
            pltpu.VMEM((N_HEADS, 1, tq), F32),
        ] + [pltpu.VMEM((1, tk, tq), F32)] * N_HEADS,
        compiler_params=_params(2, True, all_sequential=True),
        name="attn",
    )(jnp.zeros((2,), jnp.int32), qT, k, vT)


def _reset_carry_at_sequence_start(carry_ref):
    @pl.when(pl.program_id(1) == 0)
    def _():
        carry_ref[...] = jnp.zeros(carry_ref.shape, F32)


def _lru_inputs(x_ref, xp_ref, xn_ref, g_ref, wxl_ref, cw_ref, cb_ref, *, t):
    blk = pl.program_id(1)
    nb = pl.num_programs(1)
    d_lru = wxl_ref.shape[1]
    groups = t // SUBLANES
    xp = xp_ref[0] * jnp.where(blk > 0, 1.0, 0.0)
    xn = xn_ref[0] * jnp.where(blk < nb - 1, 1.0, 0.0)
    xe = jnp.concatenate([xp, x_ref[0], xn], axis=0)
    he = (xe * _rms_scale(xe) * g_ref[...]).astype(BF16)
    xl = jnp.dot(he, wxl_ref[...], preferred_element_type=F32)
    x3 = xl.reshape(groups + 2, SUBLANES, d_lru)
    sub = lax.broadcasted_iota(jnp.int32, (1, SUBLANES, d_lru), 1)

    def delayed(k):
        r = pltpu.roll(x3, k, axis=1)
        return jnp.where(sub >= k, r[1:groups + 1], r[0:groups])

    ahead = pltpu.roll(x3, SUBLANES - 1, axis=1)
    ahead = jnp.where(sub < SUBLANES - 1, ahead[1:groups + 1], ahead[2:groups + 2])
    cw = cw_ref[...]
    xc = (cb_ref[...] + cw[0:1] * delayed(2) + cw[1:2] * delayed(1) + cw[2:3] * x3[1:groups + 1]
          + cw[3:4] * ahead)
    return xc.reshape(t, d_lru)


def _lru_gates(xc, wg_ref, bg_ref):
    return [
        jnp.dot(xc[:, n * LRU_BLOCK:(n + 1) * LRU_BLOCK].astype(BF16), wg_ref[n], preferred_element_type=F32)
        + bg_ref[n]
        for n in range(N_LRU_BLOCKS)
    ]


def _lru_scan(xc, gate_pre, cneg_ref, a_scr, u_scr, carry_ref, write_rows, *, reverse, t):
    groups = t // SUBLANES
    sub = lax.broadcasted_iota(jnp.int32, (1, SUBLANES, LRU_BLOCK), 1)
    for n in range(N_LRU_BLOCKS):
        cols = slice(n * LRU_BLOCK, (n + 1) * LRU_BLOCK)
        xcn = xc[:, cols]
        gz = gate_pre[n]
        r = jax.nn.sigmoid(gz[:, :LRU_BLOCK])
        ig = jax.nn.sigmoid(gz[:, LRU_BLOCK:])
        a = jnp.exp(cneg_ref[:, cols] * r)
        y = (1.0 - a) * (1.0 + a)
        u = (y * lax.rsqrt(jnp.maximum(y, TINY))) * (ig * xcn)
        a3 = a.reshape(groups, SUBLANES, LRU_BLOCK)
        u3 = u.reshape(groups, SUBLANES, LRU_BLOCK)
        for dist in (1, 2, 4):
            if reverse:
                shift, ok = SUBLANES - dist, sub < SUBLANES - dist
            else:
                shift, ok = dist, sub >= dist
            a_nb = pltpu.roll(a3, shift, axis=1)
            u_nb = pltpu.roll(u3, shift, axis=1)
            u3 = jnp.where(ok, a3 * u_nb + u3, u3)
            a3 = jnp.where(ok, a3 * a_nb, a3)
        a_scr[:, :, cols] = a3
        u_scr[:, :, cols] = u3

    edge = 0 if reverse else SUBLANES - 1
    for n in range(N_LRU_BLOCKS):
        cols = slice(n * LRU_BLOCK, (n + 1) * LRU_BLOCK)
        hrow = carry_ref[:, cols]
        for j in range(groups):
            jj = groups - 1 - j if reverse else j
            rows = u_scr[jj, :, cols] + a_scr[jj, :, cols] * hrow
            write_rows(jj, cols, rows)
            hrow = jnp.broadcast_to(rows[edge:edge + 1, :], (SUBLANES, LRU_BLOCK))
        carry_ref[:, cols] = hrow


def _lru_f_kernel(x_ref, xp_ref, xn_ref, g_ref, wxl_ref, cw_ref, cb_ref, wg_ref, bg_ref, cneg_ref,
                  hf_ref, xc_ref, a_scr, u_scr, carry_ref, *, t):
    def write_rows(j, cols, rows):
        hf_ref[0, j, :, cols] = rows

    _reset_carry_at_sequence_start(carry_ref)
    xc = _lru_inputs(x_ref, xp_ref, xn_ref, g_ref, wxl_ref, cw_ref, cb_ref, t=t)
    xc_ref[0] = xc
    gate_pre = _lru_gates(xc, wg_ref, bg_ref)
    _lru_scan(xc, gate_pre, cneg_ref, a_scr, u_scr, carry_ref, write_rows, reverse=False, t=t)


def _merge_kernel(x_ref, g_ref, xc_ref, wg_ref, bg_ref, cneg_ref,
                  hf_ref, att_ref, wgl_ref, wga_ref, wm_ref, bm_ref, wb0_ref, wb1_ref, wo_ref, gfin_ref,
                  y_ref, a_scr, u_scr, carry_ref, hb_scr, *, t, final_norm):
    def write_rows(j, cols, rows):
        hb_scr[j, :, cols] = rows

    _reset_carry_at_sequence_start(carry_ref)
    d = x_ref.shape[2]
    x = x_ref[0]
    h = (x * _rms_scale(x) * g_ref[...]).astype(BF16)
    xc = xc_ref[0]
    gate_pre = _lru_gates(xc, wg_ref, bg_ref)
    _lru_scan(xc, gate_pre, cneg_ref, a_scr, u_scr, carry_ref, write_rows, reverse=True, t=t)
    g_l = jnp.dot(h, wgl_ref[...], preferred_element_type=F32)
    lsum = (hf_ref[0] + hb_scr[...]).reshape(t, xc.shape[1])
    l_out = (lsum * (g_l * jax.nn.sigmoid(g_l))).astype(BF16)
    g_a = jnp.dot(h, wga_ref[...], preferred_element_type=F32)
    a_out = (att_ref[0].astype(F32) * (g_a * jax.nn.sigmoid(g_a))).astype(BF16)
    a_proj = jnp.dot(a_out, wb0_ref[...], preferred_element_type=F32)
    l_proj = jnp.dot(l_out, wb1_ref[...], preferred_element_type=F32)
    gates = jax.nn.sigmoid(jnp.dot(h, wm_ref[...], preferred_element_type=F32) + bm_ref[...])
    merged = gates[:, :d] * a_proj + gates[:, d:] * l_proj
    y = x + jnp.dot(merged.astype(BF16), wo_ref[...], preferred_element_type=F32)
    if final_norm:
        y = y * _rms_scale(y) * gfin_ref[...]
    y_ref[0] = y


def _lru_specs(s, t, d, d_lru, reverse):
    nb = s // t
    groups = t // SUBLANES
    last_group = s // SUBLANES - 1

    def blk(i):
        return nb - 1 - i if reverse else i

    x_spec = pl.BlockSpec((1, t, d), lambda bi, i: (bi, blk(i), 0))
    xp_spec = pl.BlockSpec((1, HALO, d), lambda bi, i: (bi, jnp.maximum(blk(i) * groups - 1, 0), 0))
    xn_spec = pl.BlockSpec((1, HALO, d), lambda bi, i: (bi, jnp.minimum((blk(i) + 1) * groups, last_group), 0))
    hf_spec = pl.BlockSpec((1, groups, SUBLANES, d_lru), lambda bi, i: (bi, blk(i), 0, 0))
    scratch = [
        pltpu.VMEM((groups, SUBLANES, d_lru), F32),
        pltpu.VMEM((groups, SUBLANES, d_lru), F32),
        pltpu.VMEM((SUBLANES, d_lru), F32),
    ]
    return x_spec, xp_spec, xn_spec, hf_spec, scratch, blk


def _lru_f_call(x, g_in, w_xl, conv_w, conv_b, w_gate, b_gate, cneg):
    b, s, d = x.shape
    d_lru = w_xl.shape[1]
    t = min(LRU_ROWS, s)
    x_spec, xp_spec, xn_spec, hf_spec, scratch, _ = _lru_specs(s, t, d, d_lru, False)
    return pl.pallas_call(
        functools.partial(_lru_f_kernel, t=t),
        grid=(b, s // t),
        in_specs=[x_spec, xp_spec, xn_spec, _const_spec((1, d)), _const_spec(w_xl.shape),
                  _const_spec(conv_w.shape), _const_spec(conv_b.shape), _const_spec(w_gate.shape),
                  _const_spec(b_gate.shape), _const_spec(cneg.shape)],
        out_specs=[hf_spec, pl.BlockSpec((1, t, d_lru), lambda bi, i: (bi, i, 0))],
        out_shape=[jax.ShapeDtypeStruct((b, s // SUBLANES, SUBLANES, d_lru), F32),
                   jax.ShapeDtypeStruct((b, s, d_lru), F32)],
        scratch_shapes=scratch,
        compiler_params=_params(2, True),
        name="lru_f",
    )(x, x, x, g_in, w_xl, conv_w, conv_b, w_gate, b_gate, cneg)


def _merge_call(x, g_in, xc, w_gate, b_gate, cneg, hf, att, w_gl, w_ga, w_m, b_m, w_b0, w_b1, w_o, g_fin, final_norm):
    b, s, d = x.shape
    d_lru = xc.shape[2]
    t = min(LRU_ROWS, s)
    _, _, _, hf_spec, scratch, blk = _lru_specs(s, t, d, d_lru, True)
    tok = lambda width: pl.BlockSpec((1, t, width), lambda bi, i: (bi, blk(i), 0))
    consts = [w_gate, b_gate, cneg]
    tail = [w_gl, w_ga, w_m, b_m, w_b0, w_b1, w_o, g_fin]
    return pl.pallas_call(
        functools.partial(_merge_kernel, t=t, final_norm=final_norm),
        grid=(b, s // t),
        in_specs=[tok(d), _const_spec(g_in.shape), tok(d_lru)] + [_const_spec(c.shape) for c in consts]
        + [hf_spec, tok(att.shape[2])] + [_const_spec(c.shape) for c in tail],
        out_specs=tok(d),
        out_shape=jax.ShapeDtypeStruct((b, s, d), F32),
        scratch_shapes=scratch + [pltpu.VMEM((t // SUBLANES, SUBLANES, d_lru), F32)],
        compiler_params=_params(2, True),
        name="merge",
    )(x, g_in, xc, *consts, hf, att, *tail)


def _rope_tables(s, gain, scale):
    rows_n = s // GRID_W
    rows = jnp.repeat(jnp.arange(rows_n, dtype=F32), GRID_W)
    cols = jnp.tile(jnp.arange(GRID_W, dtype=F32), rows_n)
    n_pair_axis = HEAD_DIM // 4
    inv_freq = ROPE_THETA ** (-jnp.arange(n_pair_axis, dtype=F32) / n_pair_axis)
    ang = jnp.concatenate([rows[:, None] * inv_freq, cols[:, None] * inv_freq], axis=-1)
    cos = jnp.cos(ang).T
    sin = jnp.sin(ang).T
    c = jnp.concatenate([cos, cos], axis=0)
    sn = jnp.concatenate([-sin, sin], axis=0)
    gain = gain.astype(F32)
    return (gain[:, None] * c) * scale, (jnp.roll(gain, HALF)[:, None] * sn) * scale


def _layer_params(w_in, q_norm, k_norm, w_rgate, b_rgate, w_igate, b_igate, lam, w_branch, w_out, d):
    d_attn = N_HEADS * HEAD_DIM
    d_kv = N_KV_HEADS * HEAD_DIM
    d_lru = N_LRU_BLOCKS * LRU_BLOCK
    splits = [d_attn, d_kv, d_kv, d_attn, d_lru, d_lru]
    offs = [0]
    for w in splits:
        offs.append(offs[-1] + w)
    w_q, w_k, w_v, w_ga, w_xl, w_gl = (w_in[:, offs[j]:offs[j + 1]] for j in range(6))
    w_m = w_in[:, offs[6]:]
    perm = jnp.concatenate([jnp.arange(0, HEAD_DIM, 2), jnp.arange(1, HEAD_DIM, 2)])
    permute = lambda w, nh: w.reshape(d, nh, HEAD_DIM)[:, :, perm].reshape(d, nh * HEAD_DIM)
    w_qkvT = jnp.concatenate([permute(w_q, N_HEADS), permute(w_k, N_KV_HEADS), w_v], axis=1).T.astype(BF16)
    w_gate = jnp.concatenate([w_rgate, w_igate], axis=-1).astype(BF16)
    b_gate = jnp.concatenate([b_rgate.reshape(2, N_LRU_BLOCKS, 1, LRU_BLOCK),
                              b_igate.reshape(2, N_LRU_BLOCKS, 1, LRU_BLOCK)], axis=-1).astype(F32)
    cneg = (-RG_C * jax.nn.softplus(-lam.astype(F32))).reshape(2, 1, d_lru)
    return dict(
        w_qkvT=w_qkvT, gq=q_norm[perm], gk=k_norm[perm],
        w_ga=w_ga.astype(BF16), w_xl=w_xl.astype(BF16), w_gl=w_gl.astype(BF16), w_m=w_m.astype(BF16),
        w_gate=w_gate, b_gate=b_gate, cneg=cneg,
        w_b0=w_branch[0].astype(BF16), w_b1=w_branch[1].astype(BF16), w_o=w_out.astype(BF16),
    )


def _layer(x, p, g_in, conv_w, conv_b, b_m, g_fin, final_norm):
    s = x.shape[1]
    q_scale = math.log2(math.e) / math.sqrt(HEAD_DIM)
    cq, sq = _rope_tables(s, p["gq"], q_scale)
    ck, sk = _rope_tables(s, p["gk"], 1.0)
    qT, k, vT = _qkv_call(x, g_in, p["w_qkvT"], cq, sq, ck, sk)
    att = _attn_call(qT, k, vT)
    lru = (g_in, p["w_xl"], conv_w, conv_b)
    hf, xc = _lru_f_call(x, *lru, p["w_gate"][0], p["b_gate"][0], p["cneg"][0])
    return _merge_call(x, g_in, xc, p["w_gate"][1], p["b_gate"][1], p["cneg"][1], hf, att,
                       p["w_gl"], p["w_ga"], p["w_m"], b_m, p["w_b0"], p["w_b1"], p["w_o"], g_fin, final_norm)


def kernel(x_prompt, x_sample, norm_in, w_in, b_merge, q_norm, k_norm, conv_w, conv_b, w_rgate, b_rgate,
           w_igate, b_igate, lam, w_branch, w_out, norm_final):
    depth, d = norm_in.shape
    layers = [
        _layer_params(w_in[l], q_norm[l], k_norm[l], w_rgate[l], b_rgate[l], w_igate[l], b_igate[l], lam[l],
                      w_branch[l], w_out[l], d)
        for l in range(depth)
    ]
    g_fin = norm_final.reshape(1, d).astype(F32)
    outs = []
    for x in (x_prompt, x_sample):
        for l in range(depth):
            x = _layer(x, layers[l], norm_in[l].reshape(1, d).astype(F32), conv_w[l].astype(F32),
                       conv_b[l].reshape(1, -1).astype(F32), b_merge[l].reshape(1, -1).astype(F32), g_fin,
                       l == depth - 1)
        outs.append(x)
    return tuple(outs)
```

```python
import functools
import math

import jax
import jax.numpy as jnp
from jax import lax
from jax.experimental import pallas as pl
from jax.experimental.pallas import tpu as pltpu

N_HEADS = 8
N_KV_HEADS = 2
GROUP = N_HEADS // N_KV_HEADS
HEAD_DIM = 128
HALF = HEAD_DIM // 2
N_LRU_BLOCKS = 8
LRU_BLOCK = 128
CONV_W = 4
RG_C = 8.0
EPS = 1e-6
TINY = 1e-30
GRID_W = 64
ROPE_THETA = 10000.0

SUBLANES = 8
ONES_ROWS = 2 * SUBLANES
HALO = SUBLANES
V7X_VMEM_LIMIT = 56 * 1024 * 1024

QKV_ROWS = 1024
ATTN_Q = 512
ATTN_K = 256
AHEAD = 2
LRU_ROWS = 256

F32 = jnp.float32
BF16 = jnp.bfloat16


def _const_spec(shape):
    nd = len(shape)
    return pl.BlockSpec(shape, lambda *_: (0,) * nd, pipeline_mode=pl.Buffered(1))


def _params(n_axes, sequential_last, all_sequential=False):
    sem = ["arbitrary" if all_sequential else "parallel"] * n_axes
    if sequential_last:
        sem[-1] = "arbitrary"
    return pltpu.CompilerParams(dimension_semantics=tuple(sem), vmem_limit_bytes=V7X_VMEM_LIMIT)


def _rms_scale(x):
    return lax.rsqrt(jnp.mean(x * x, axis=-1, keepdims=True) + EPS)


def _qkv_kernel(x_ref, g_ref, w_ref, cq_ref, sq_ref, ck_ref, sk_ref, qT_ref, k_ref, vT_ref):
    x = x_ref[0]
    h = (x * _rms_scale(x) * g_ref[...]).astype(BF16)
    zT = lax.dot_general(w_ref[...], h, (((1,), (1,)), ((), ())), preferred_element_type=F32)

    def norm_rope(z, c, s):
        n = z * lax.rsqrt(jnp.mean(z * z, axis=0, keepdims=True) + EPS)
        swapped = jnp.concatenate([n[HALF:], n[:HALF]], axis=0)
        return n * c + swapped * s

    cq, sq = cq_ref[...], sq_ref[...]
    for hd in range(N_HEADS):
        rows = slice(hd * HEAD_DIM, (hd + 1) * HEAD_DIM)
        qh = norm_rope(zT[rows], cq, sq).astype(BF16)
        tq = qT_ref.shape[3]
        for qb in range(qT_ref.shape[1]):
            qT_ref[0, qb, rows, :] = qh[:, qb * tq:(qb + 1) * tq]
    ck, sk = ck_ref[...], sk_ref[...]
    k_off = N_HEADS * HEAD_DIM
    for kv in range(N_KV_HEADS):
        rows = slice(k_off + kv * HEAD_DIM, k_off + (kv + 1) * HEAD_DIM)
        kT = norm_rope(zT[rows], ck, sk)
        k_ref[0, :, kv * HEAD_DIM:(kv + 1) * HEAD_DIM] = kT.T.astype(BF16)
    v_off = k_off + N_KV_HEADS * HEAD_DIM
    vT_ref[0] = zT[v_off:v_off + N_KV_HEADS * HEAD_DIM].astype(BF16)


def _qkv_call(x, g_in, w_qkvT, cq, sq, ck, sk):
    b, s, d = x.shape
    t = min(QKV_ROWS, s)
    tq = min(ATTN_Q, t)
    n_q, n_kv = N_HEADS * HEAD_DIM, N_KV_HEADS * HEAD_DIM
    tab = pl.BlockSpec((HEAD_DIM, t), lambda bi, i: (0, i))
    return pl.pallas_call(
        _qkv_kernel,
        grid=(b, s // t),
        in_specs=[
            pl.BlockSpec((1, t, d), lambda bi, i: (bi, i, 0)),
            _const_spec((1, d)),
            _const_spec(w_qkvT.shape),
            tab, tab, tab, tab,
        ],
        out_specs=[
            pl.BlockSpec((1, t // tq, n_q, tq), lambda bi, i: (bi, i, 0, 0)),
            pl.BlockSpec((1, t, n_kv), lambda bi, i: (bi, i, 0)),
            pl.BlockSpec((1, n_kv, t), lambda bi, i: (bi, 0, i)),
        ],
        out_shape=[
            jax.ShapeDtypeStruct((b, s // tq, n_q, tq), BF16),
            jax.ShapeDtypeStruct((b, s, n_kv), BF16),
            jax.ShapeDtypeStruct((b, n_kv, s), BF16),
        ],
        compiler_params=_params(2, False),
        name="qkv",
    )(x, g_in, w_qkvT, cq, sq, ck, sk)


def _attn_kernel(zero_ref, qT_ref, k_ref, vT_ref, o_ref, m_ref, acc_ref, cmax_ref, *s_slots, n_chunks, tk):
    @pl.when((pl.program_id(0) == 0) & (pl.program_id(1) == 0))
    def _():
        acc_ref[...] = jnp.zeros(acc_ref.shape, F32)

    m_ref[...] = jnp.full(m_ref.shape, -jnp.inf, F32)
    ones_rows = jnp.ones((ONES_ROWS, tk), BF16)
    zw, zr = zero_ref[0], zero_ref[1]

    def kv_cols(hd):
        kv = hd // GROUP
        return slice(kv * HEAD_DIM, (kv + 1) * HEAD_DIM)

    def scores(c, hd):
        off = pl.multiple_of(c * tk, tk)
        kc = k_ref[0, pl.ds(off, tk), kv_cols(hd)]
        qT = qT_ref[0, 0, hd * HEAD_DIM:(hd + 1) * HEAD_DIM, :]
        s = jnp.dot(kc, qT, preferred_element_type=F32)
        s_slots[hd][zw] = s
        cmax_ref[hd] = jnp.max(s, axis=0, keepdims=True)

    def accumulate(c, hd):
        off = pl.multiple_of(c * tk, tk)
        m_old = m_ref[hd:hd + 1, :]
        m_new = jnp.maximum(m_old, cmax_ref[hd])
        m_ref[hd:hd + 1, :] = m_new
        p = jnp.exp2(s_slots[hd][zr] - m_new).astype(BF16)
        vc = jnp.concatenate([vT_ref[0, kv_cols(hd), pl.ds(off, tk)], ones_rows], axis=0)
        acc_ref[hd] = jnp.exp2(m_old - m_new) * acc_ref[hd] + jnp.dot(vc, p, preferred_element_type=F32)

    for hd in range(AHEAD):
        scores(0, hd)

    def step(c, carry):
        for hd in range(N_HEADS):
            w = hd + AHEAD
            scores(jnp.minimum(c + w // N_HEADS, n_chunks - 1), w % N_HEADS)
            accumulate(c, hd)
        return carry

    lax.fori_loop(0, n_chunks, step, 0, unroll=4)

    for hd in range(N_HEADS):
        o = acc_ref[hd, :HEAD_DIM, :] * (1.0 / acc_ref[hd, HEAD_DIM:HEAD_DIM + 1, :])
        o_ref[0, :, hd * HEAD_DIM:(hd + 1) * HEAD_DIM] = o.T.astype(BF16)


def _attn_call(qT, k, vT):
    b, n_qb, n_q, tq = qT.shape
    s, n_kv = k.shape[1:]
    tk = min(ATTN_K, s)
    kernel = functools.partial(_attn_kernel, n_chunks=s // tk, tk=tk)
    return pl.pallas_call(
        kernel,
        grid=(b, n_qb),
        in_specs=[
            pl.BlockSpec(memory_space=pltpu.SMEM),
            pl.BlockSpec((1, 1, n_q, tq), lambda bi, i: (bi, i, 0, 0)),
            pl.BlockSpec((1, s, n_kv), lambda bi, i: (bi, 0, 0), pipeline_mode=pl.Buffered(1)),
            pl.BlockSpec((1, n_kv, s), lambda bi, i: (bi, 0, 0), pipeline_mode=pl.Buffered(1)),
        ],
        out_specs=pl.BlockSpec((1, tq, n_q), lambda bi, i: (bi, i, 0)),
        out_shape=jax.ShapeDtypeStruct((b, s, n_q), BF16),
        scratch_shapes=[
            pltpu.VMEM((N_HEADS, tq), F32),
            pltpu.VMEM((N_HEADS, HEAD_DIM + ONES_ROWS, tq), F32),
            pltpu.VMEM((N_HEADS, 1, tq), F32),
        ] + [pltpu.VMEM((1, tk, tq), F32)] * N_HEADS,
        compiler_params=_params(2, True, all_sequential=True),
        name="attn",
    )(jnp.zeros((2,), jnp.int32), qT, k, vT)


def _reset_carry_at_sequence_start(carry_ref):
    @pl.when(pl.program_id(1) == 0)
    def _():
        carry_ref[...] = jnp.zeros(carry_ref.shape, F32)


def _lru_inputs(x_ref, xp_ref, xn_ref, g_ref, wxl_ref, cw_ref, cb_ref, *, t):
    blk = pl.program_id(1)
    nb = pl.num_programs(1)
    d_lru = wxl_ref.shape[1]
    groups = t // SUBLANES
    xp = xp_ref[0] * jnp.where(blk > 0, 1.0, 0.0)
    xn = xn_ref[0] * jnp.where(blk < nb - 1, 1.0, 0.0)
    xe = jnp.concatenate([xp, x_ref[0], xn], axis=0)
    he = (xe * _rms_scale(xe) * g_ref[...]).astype(BF16)
    xl = jnp.dot(he, wxl_ref[...], preferred_element_type=F32)
    x3 = xl.reshape(groups + 2, SUBLANES, d_lru)
    sub = lax.broadcasted_iota(jnp.int32, (1, SUBLANES, d_lru), 1)

    def delayed(k):
        r = pltpu.roll(x3, k, axis=1)
        return jnp.where(sub >= k, r[1:groups + 1], r[0:groups])

    ahead = pltpu.roll(x3, SUBLANES - 1, axis=1)
    ahead = jnp.where(sub < SUBLANES - 1, ahead[1:groups + 1], ahead[2:groups + 2])
    cw = cw_ref[...]
    xc = (cb_ref[...] + cw[0:1] * delayed(2) + cw[1:2] * delayed(1) + cw[2:3] * x3[1:groups + 1]
          + cw[3:4] * ahead)
    return xc.reshape(t, d_lru)


def _lru_gates(xc, wg_ref, bg_ref):
    return [
        jnp.dot(xc[:, n * LRU_BLOCK:(n + 1) * LRU_BLOCK].astype(BF16), wg_ref[n], preferred_element_type=F32)
        + bg_ref[n]
        for n in range(N_LRU_BLOCKS)
    ]


def _lru_scan(xc, gate_pre, cneg_ref, a_scr, u_scr, carry_ref, write_rows, *, reverse, t):
    groups = t // SUBLANES
    sub = lax.broadcasted_iota(jnp.int32, (1, SUBLANES, LRU_BLOCK), 1)
    for n in range(N_LRU_BLOCKS):
        cols = slice(n * LRU_BLOCK, (n + 1) * LRU_BLOCK)
        xcn = xc[:, cols]
        gz = gate_pre[n]
        r = jax.nn.sigmoid(gz[:, :LRU_BLOCK])
        ig = jax.nn.sigmoid(gz[:, LRU_BLOCK:])
        a = jnp.exp2(cneg_ref[:, cols] * r)
        y = (1.0 - a) * (1.0 + a)
        u = (y * lax.rsqrt(jnp.maximum(y, TINY))) * (ig * xcn)
        a3 = a.reshape(groups, SUBLANES, LRU_BLOCK)
        u3 = u.reshape(groups, SUBLANES, LRU_BLOCK)
        for dist in (1, 2, 4):
            if reverse:
                shift, ok = SUBLANES - dist, sub < SUBLANES - dist
            else:
                shift, ok = dist, sub >= dist
            a_nb = pltpu.roll(a3, shift, axis=1)
            u_nb = pltpu.roll(u3, shift, axis=1)
            u3 = jnp.where(ok, a3 * u_nb + u3, u3)
            a3 = jnp.where(ok, a3 * a_nb, a3)
        a_scr[:, :, cols] = a3
        u_scr[:, :, cols] = u3

    edge = 0 if reverse else SUBLANES - 1
    for n in range(N_LRU_BLOCKS):
        cols = slice(n * LRU_BLOCK, (n + 1) * LRU_BLOCK)
        hrow = carry_ref[:, cols]
        for j in range(groups):
            jj = groups - 1 - j if reverse else j
            rows = u_scr[jj, :, cols] + a_scr[jj, :, cols] * hrow
            write_rows(jj, cols, rows)
            hrow = jnp.broadcast_to(rows[edge:edge + 1, :], (SUBLANES, LRU_BLOCK))
        carry_ref[:, cols] = hrow


def _lru_f_kernel(x_ref, xp_ref, xn_ref, g_ref, wxl_ref, cw_ref, cb_ref, wg_ref, bg_ref, cneg_ref,
                  hf_ref, xc_ref, a_scr, u_scr, carry_ref, *, t):
    def write_rows(j, cols, rows):
        hf_ref[0, j, :, cols] = rows

    _reset_carry_at_sequence_start(carry_ref)
    xc = _lru_inputs(x_ref, xp_ref, xn_ref, g_ref, wxl_ref, cw_ref, cb_ref, t=t)
    xc_ref[0] = xc
    gate_pre = _lru_gates(xc, wg_ref, bg_ref)
    _lru_scan(xc, gate_pre, cneg_ref, a_scr, u_scr, carry_ref, write_rows, reverse=False, t=t)


def _merge_kernel(x_ref, g_ref, xc_ref, wg_ref, bg_ref, cneg_ref,
                  hf_ref, att_ref, wgl_ref, wga_ref, wm_ref, bm_ref, wb0_ref, wb1_ref, wo_ref, gfin_ref,
                  y_ref, a_scr, u_scr, carry_ref, hb_scr, *, t, final_norm):
    def write_rows(j, cols, rows):
        hb_scr[j, :, cols] = rows

    _reset_carry_at_sequence_start(carry_ref)
    d = x_ref.shape[2]
    x = x_ref[0]
    h = (x * _rms_scale(x) * g_ref[...]).astype(BF16)
    xc = xc_ref[0]
    gate_pre = _lru_gates(xc, wg_ref, bg_ref)
    _lru_scan(xc, gate_pre, cneg_ref, a_scr, u_scr, carry_ref, write_rows, reverse=True, t=t)
    g_l = jnp.dot(h, wgl_ref[...], preferred_element_type=F32)
    lsum = (hf_ref[0] + hb_scr[...]).reshape(t, xc.shape[1])
    l_out = (lsum * (g_l * jax.nn.sigmoid(g_l))).astype(BF16)
    g_a = jnp.dot(h, wga_ref[...], preferred_element_type=F32)
    a_out = (att_ref[0].astype(F32) * (g_a * jax.nn.sigmoid(g_a))).astype(BF16)
    a_proj = jnp.dot(a_out, wb0_ref[...], preferred_element_type=F32)
    l_proj = jnp.dot(l_out, wb1_ref[...], preferred_element_type=F32)
    gates = jax.nn.sigmoid(jnp.dot(h, wm_ref[...], preferred_element_type=F32) + bm_ref[...])
    merged = gates[:, :d] * a_proj + gates[:, d:] * l_proj
    y = x + jnp.dot(merged.astype(BF16), wo_ref[...], preferred_element_type=F32)
    if final_norm:
        y = y * _rms_scale(y) * gfin_ref[...]
    y_ref[0] = y


def _lru_specs(s, t, d, d_lru, reverse):
    nb = s // t
    groups = t // SUBLANES
    last_group = s // SUBLANES - 1

    def blk(i):
        return nb - 1 - i if reverse else i

    x_spec = pl.BlockSpec((1, t, d), lambda bi, i: (bi, blk(i), 0))
    xp_spec = pl.BlockSpec((1, HALO, d), lambda bi, i: (bi, jnp.maximum(blk(i) * groups - 1, 0), 0))
    xn_spec = pl.BlockSpec((1, HALO, d), lambda bi, i: (bi, jnp.minimum((blk(i) + 1) * groups, last_group), 0))
    hf_spec = pl.BlockSpec((1, groups, SUBLANES, d_lru), lambda bi, i: (bi, blk(i), 0, 0))
    scratch = [
        pltpu.VMEM((groups, SUBLANES, d_lru), F32),
        pltpu.VMEM((groups, SUBLANES, d_lru), F32),
        pltpu.VMEM((SUBLANES, d_lru), F32),
    ]
    return x_spec, xp_spec, xn_spec, hf_spec, scratch, blk


def _lru_f_call(x, g_in, w_xl, conv_w, conv_b, w_gate, b_gate, cneg):
    b, s, d = x.shape
    d_lru = w_xl.shape[1]
    t = min(LRU_ROWS, s)
    x_spec, xp_spec, xn_spec, hf_spec, scratch, _ = _lru_specs(s, t, d, d_lru, False)
    return pl.pallas_call(
        functools.partial(_lru_f_kernel, t=t),
        grid=(b, s // t),
        in_specs=[x_spec, xp_spec, xn_spec, _const_spec((1, d)), _const_spec(w_xl.shape),
                  _const_spec(conv_w.shape), _const_spec(conv_b.shape), _const_spec(w_gate.shape),
                  _const_spec(b_gate.shape), _const_spec(cneg.shape)],
        out_specs=[hf_spec, pl.BlockSpec((1, t, d_lru), lambda bi, i: (bi, i, 0))],
        out_shape=[jax.ShapeDtypeStruct((b, s // SUBLANES, SUBLANES, d_lru), F32),
                   jax.ShapeDtypeStruct((b, s, d_lru), F32)],
        scratch_shapes=scratch,
        compiler_params=_params(2, True),
        name="lru_f",
    )(x, x, x, g_in, w_xl, conv_w, conv_b, w_gate, b_gate, cneg)


def _merge_call(x, g_in, xc, w_gate, b_gate, cneg, hf, att, w_gl, w_ga, w_m, b_m, w_b0, w_b1, w_o, g_fin, final_norm):
    b, s, d = x.shape
    d_lru = xc.shape[2]
    t = min(LRU_ROWS, s)
    _, _, _, hf_spec, scratch, blk = _lru_specs(s, t, d, d_lru, True)
    tok = lambda width: pl.BlockSpec((1, t, width), lambda bi, i: (bi, blk(i), 0))
    consts = [w_gate, b_gate, cneg]
    tail = [w_gl, w_ga, w_m, b_m, w_b0, w_b1, w_o, g_fin]
    return pl.pallas_call(
        functools.partial(_merge_kernel, t=t, final_norm=final_norm),
        grid=(b, s // t),
        in_specs=[tok(d), _const_spec(g_in.shape), tok(d_lru)] + [_const_spec(c.shape) for c in consts]
        + [hf_spec, tok(att.shape[2])] + [_const_spec(c.shape) for c in tail],
        out_specs=tok(d),
        out_shape=jax.ShapeDtypeStruct((b, s, d), F32),
        scratch_shapes=scratch + [pltpu.VMEM((t // SUBLANES, SUBLANES, d_lru), F32)],
        compiler_params=_params(2, True),
        name="merge",
    )(x, g_in, xc, *consts, hf, att, *tail)


def _rope_tables(s, gain, scale):
    rows_n = s // GRID_W
    rows = jnp.repeat(jnp.arange(rows_n, dtype=F32), GRID_W)
    cols = jnp.tile(jnp.arange(GRID_W, dtype=F32), rows_n)
    n_pair_axis = HEAD_DIM // 4
    inv_freq = ROPE_THETA ** (-jnp.arange(n_pair_axis, dtype=F32) / n_pair_axis)
    ang = jnp.concatenate([rows[:, None] * inv_freq, cols[:, None] * inv_freq], axis=-1)
    cos = jnp.cos(ang).T
    sin = jnp.sin(ang).T
    c = jnp.concatenate([cos, cos], axis=0)
    sn = jnp.concatenate([-sin, sin], axis=0)
    gain = gain.astype(F32)
    return (gain[:, None] * c) * scale, (jnp.roll(gain, HALF)[:, None] * sn) * scale


def _layer_params(w_in, q_norm, k_norm, w_rgate, b_rgate, w_igate, b_igate, lam, w_branch, w_out, d):
    d_attn = N_HEADS * HEAD_DIM
    d_kv = N_KV_HEADS * HEAD_DIM
    d_lru = N_LRU_BLOCKS * LRU_BLOCK
    splits = [d_attn, d_kv, d_kv, d_attn, d_lru, d_lru]
    offs = [0]
    for w in splits:
        offs.append(offs[-1] + w)
    w_q, w_k, w_v, w_ga, w_xl, w_gl = (w_in[:, offs[j]:offs[j + 1]] for j in range(6))
    w_m = w_in[:, offs[6]:]
    perm = jnp.concatenate([jnp.arange(0, HEAD_DIM, 2), jnp.arange(1, HEAD_DIM, 2)])
    permute = lambda w, nh: w.reshape(d, nh, HEAD_DIM)[:, :, perm].reshape(d, nh * HEAD_DIM)
    w_qkvT = jnp.concatenate([permute(w_q, N_HEADS), permute(w_k, N_KV_HEADS), w_v], axis=1).T.astype(BF16)
    w_gate = jnp.concatenate([w_rgate, w_igate], axis=-1).astype(BF16)
    b_gate = jnp.concatenate([b_rgate.reshape(2, N_LRU_BLOCKS, 1, LRU_BLOCK),
                              b_igate.reshape(2, N_LRU_BLOCKS, 1, LRU_BLOCK)], axis=-1).astype(F32)
    cneg = (-RG_C * math.log2(math.e) * jax.nn.softplus(-lam.astype(F32))).reshape(2, 1, d_lru)
    return dict(
        w_qkvT=w_qkvT, gq=q_norm[perm], gk=k_norm[perm],
        w_ga=w_ga.astype(BF16), w_xl=w_xl.astype(BF16), w_gl=w_gl.astype(BF16), w_m=w_m.astype(BF16),
        w_gate=w_gate, b_gate=b_gate, cneg=cneg,
        w_b0=w_branch[0].astype(BF16), w_b1=w_branch[1].astype(BF16), w_o=w_out.astype(BF16),
    )


def _layer(x, p, g_in, conv_w, conv_b, b_m, g_fin, final_norm):
    s = x.shape[1]
    q_scale = math.log2(math.e) / math.sqrt(HEAD_DIM)
    cq, sq = _rope_tables(s, p["gq"], q_scale)
    ck, sk = _rope_tables(s, p["gk"], 1.0)
    qT, k, vT = _qkv_call(x, g_in, p["w_qkvT"], cq, sq, ck, sk)
    att = _attn_call(qT, k, vT)
    lru = (g_in, p["w_xl"], conv_w, conv_b)
    hf, xc = _lru_f_call(x, *lru, p["w_gate"][0], p["b_gate"][0], p["cneg"][0])
    return _merge_call(x, g_in, xc, p["w_gate"][1], p["b_gate"][1], p["cneg"][1], hf, att,
                       p["w_gl"], p["w_ga"], p["w_m"], b_m, p["w_b0"], p["w_b1"], p["w_o"], g_fin, final_norm)


def kernel(x_prompt, x_sample, norm_in, w_in, b_merge, q_norm, k_norm, conv_w, conv_b, w_rgate, b_rgate,
           w_igate, b_igate, lam, w_branch, w_out, norm_final):
    depth, d = norm_in.shape
    layers = [
        _layer_params(w_in[l], q_norm[l], k_norm[l], w_rgate[l], b_rgate[l], w_igate[l], b_igate[l], lam[l],
                      w_branch[l], w_out[l], d)
        for l in range(depth)
    ]
    g_fin = norm_final.reshape(1, d).astype(F32)
    outs = []
    for x in (x_prompt, x_sample):
        for l in range(depth):
            x = _layer(x, layers[l], norm_in[l].reshape(1, d).astype(F32), conv_w[l].astype(F32),
                       conv_b[l].reshape(1, -1).astype(F32), b_merge[l].reshape(1, -1).astype(F32), g_fin,
                       l == depth - 1)
        outs.append(x)
    return tuple(outs)
```

```python
import functools
import math

import jax
import jax.numpy as jnp
from jax import lax
from jax.experimental import pallas as pl
from jax.experimental.pallas import tpu as pltpu

N_HEADS = 8
N_KV_HEADS = 2
GROUP = N_HEADS // N_KV_HEADS
HEAD_DIM = 128
HALF = HEAD_DIM // 2
N_LRU_BLOCKS = 8
LRU_BLOCK = 128
CONV_W = 4
RG_C = 8.0
EPS = 1e-6
TINY = 1e-30
GRID_W = 64
ROPE_THETA = 10000.0

SUBLANES = 8
ONES_ROWS = 2 * SUBLANES
HALO = SUBLANES
V7X_VMEM_LIMIT = 56 * 1024 * 1024

QKV_ROWS = 1024
ATTN_Q = 512
ATTN_K = 256
AHEAD = 2
LRU_ROWS = 256

F32 = jnp.float32
BF16 = jnp.bfloat16


def _const_spec(shape):
    nd = len(shape)
    return pl.BlockSpec(shape, lambda *_: (0,) * nd, pipeline_mode=pl.Buffered(1))


def _params(n_axes, sequential_last, all_sequential=False):
    sem = ["arbitrary" if all_sequential else "parallel"] * n_axes
    if sequential_last:
        sem[-1] = "arbitrary"
    return pltpu.CompilerParams(dimension_semantics=tuple(sem), vmem_limit_bytes=V7X_VMEM_LIMIT)


def _rms_scale(x):
    return lax.rsqrt(jnp.mean(x * x, axis=-1, keepdims=True) + EPS)


def _qkv_kernel(x_ref, g_ref, w_ref, cq_ref, sq_ref, ck_ref, sk_ref, qT_ref, k_ref, vT_ref):
    x = x_ref[0]
    h = (x * _rms_scale(x) * g_ref[...]).astype(BF16)
    zT = lax.dot_general(w_ref[...], h, (((1,), (1,)), ((), ())), preferred_element_type=F32)

    def norm_rope(z, c, s):
        n = z * lax.rsqrt(jnp.mean(z * z, axis=0, keepdims=True) + EPS)
        swapped = jnp.concatenate([n[HALF:], n[:HALF]], axis=0)
        return n * c + swapped * s

    cq, sq = cq_ref[...], sq_ref[...]
    for hd in range(N_HEADS):
        rows = slice(hd * HEAD_DIM, (hd + 1) * HEAD_DIM)
        qh = norm_rope(zT[rows], cq, sq).astype(BF16)
        tq = qT_ref.shape[3]
        for qb in range(qT_ref.shape[1]):
            qT_ref[0, qb, rows, :] = qh[:, qb * tq:(qb + 1) * tq]
    ck, sk = ck_ref[...], sk_ref[...]
    k_off = N_HEADS * HEAD_DIM
    for kv in range(N_KV_HEADS):
        rows = slice(k_off + kv * HEAD_DIM, k_off + (kv + 1) * HEAD_DIM)
        kT = norm_rope(zT[rows], ck, sk)
        k_ref[0, :, kv * HEAD_DIM:(kv + 1) * HEAD_DIM] = kT.T.astype(BF16)
    v_off = k_off + N_KV_HEADS * HEAD_DIM
    vT_ref[0] = zT[v_off:v_off + N_KV_HEADS * HEAD_DIM].astype(BF16)


def _qkv_call(x, g_in, w_qkvT, cq, sq, ck, sk):
    b, s, d = x.shape
    t = min(QKV_ROWS, s)
    tq = min(ATTN_Q, t)
    n_q, n_kv = N_HEADS * HEAD_DIM, N_KV_HEADS * HEAD_DIM
    tab = pl.BlockSpec((HEAD_DIM, t), lambda bi, i: (0, i))
    return pl.pallas_call(
        _qkv_kernel,
        grid=(b, s // t),
        in_specs=[
            pl.BlockSpec((1, t, d), lambda bi, i: (bi, i, 0)),
            _const_spec((1, d)),
            _const_spec(w_qkvT.shape),
            tab, tab, tab, tab,
        ],
        out_specs=[
            pl.BlockSpec((1, t // tq, n_q, tq), lambda bi, i: (bi, i, 0, 0)),
            pl.BlockSpec((1, t, n_kv), lambda bi, i: (bi, i, 0)),
            pl.BlockSpec((1, n_kv, t), lambda bi, i: (bi, 0, i)),
        ],
        out_shape=[
            jax.ShapeDtypeStruct((b, s // tq, n_q, tq), BF16),
            jax.ShapeDtypeStruct((b, s, n_kv), BF16),
            jax.ShapeDtypeStruct((b, n_kv, s), BF16),
        ],
        compiler_params=_params(2, False),
        name="qkv",
    )(x, g_in, w_qkvT, cq, sq, ck, sk)


def _attn_kernel(zero_ref, qT_ref, k_ref, vT_ref, o_ref, m_ref, acc_ref, cmax_ref, *s_slots, n_chunks, tk):
    @pl.when((pl.program_id(0) == 0) & (pl.program_id(1) == 0))
    def _():
        acc_ref[...] = jnp.zeros(acc_ref.shape, F32)

    m_ref[...] = jnp.full(m_ref.shape, -jnp.inf, F32)
    ones_rows = jnp.ones((ONES_ROWS, tk), BF16)
    zw, zr = zero_ref[0], zero_ref[1]

    def kv_cols(hd):
        kv = hd // GROUP
        return slice(kv * HEAD_DIM, (kv + 1) * HEAD_DIM)

    def scores(c, hd):
        off = pl.multiple_of(c * tk, tk)
        kc = k_ref[0, pl.ds(off, tk), kv_cols(hd)]
        qT = qT_ref[0, 0, hd * HEAD_DIM:(hd + 1) * HEAD_DIM, :]
        s = jnp.dot(kc, qT, preferred_element_type=F32)
        s_slots[hd][zw] = s
        cmax_ref[hd] = jnp.max(s, axis=0, keepdims=True)

    def accumulate(c, hd):
        off = pl.multiple_of(c * tk, tk)
        m_old = m_ref[hd:hd + 1, :]
        m_new = jnp.maximum(m_old, cmax_ref[hd])
        m_ref[hd:hd + 1, :] = m_new
        p = jnp.exp2(s_slots[hd][zr] - m_new).astype(BF16)
        vc = jnp.concatenate([vT_ref[0, kv_cols(hd), pl.ds(off, tk)], ones_rows], axis=0)
        acc_ref[hd] = jnp.exp2(m_old - m_new) * acc_ref[hd] + jnp.dot(vc, p, preferred_element_type=F32)

    for hd in range(AHEAD):
        scores(0, hd)

    def step(c, carry):
        for hd in range(N_HEADS):
            w = hd + AHEAD
            scores(jnp.minimum(c + w // N_HEADS, n_chunks - 1), w % N_HEADS)
            accumulate(c, hd)
        return carry

    lax.fori_loop(0, n_chunks, step, 0, unroll=8)

    for hd in range(N_HEADS):
        o = acc_ref[hd, :HEAD_DIM, :] * (1.0 / acc_ref[hd, HEAD_DIM:HEAD_DIM + 1, :])
        o_ref[0, :, hd * HEAD_DIM:(hd + 1) * HEAD_DIM] = o.T.astype(BF16)


def _attn_call(qT, k, vT):
    b, n_qb, n_q, tq = qT.shape
    s, n_kv = k.shape[1:]
    tk = min(ATTN_K, s)
    kernel = functools.partial(_attn_kernel, n_chunks=s // tk, tk=tk)
    return pl.pallas_call(
        kernel,
        grid=(b, n_qb),
        in_specs=[
            pl.BlockSpec(memory_space=pltpu.SMEM),
            pl.BlockSpec((1, 1, n_q, tq), lambda bi, i: (bi, i, 0, 0)),
            pl.BlockSpec((1, s, n_kv), lambda bi, i: (bi, 0, 0), pipeline_mode=pl.Buffered(1)),
            pl.BlockSpec((1, n_kv, s), lambda bi, i: (bi, 0, 0), pipeline_mode=pl.Buffered(1)),
        ],
        out_specs=pl.BlockSpec((1, tq, n_q), lambda bi, i: (bi, i, 0)),
        out_shape=jax.ShapeDtypeStruct((b, s, n_q), BF16),
        scratch_shapes=[
            pltpu.VMEM((N_HEADS, tq), F32),
            pltpu.VMEM((N_HEADS, HEAD_DIM + ONES_ROWS, tq), F32),
            pltpu.VMEM((N_HEADS, 1, tq), F32),
        ] + [pltpu.VMEM((1, tk, tq), F32)] * N_HEADS,
        compiler_params=_params(2, True, all_sequential=True),
        name="attn",
    )(jnp.zeros((2,), jnp.int32), qT, k, vT)


def _reset_carry_at_sequence_start(carry_ref):
    @pl.when(pl.program_id(1) == 0)
    def _():
        carry_ref[...] = jnp.zeros(carry_ref.shape, F32)


def _lru_inputs(x_ref, xp_ref, xn_ref, g_ref, wxl_ref, cw_ref, cb_ref, *, t):
    blk = pl.program_id(1)
    nb = pl.num_programs(1)
    d_lru = wxl_ref.shape[1]
    groups = t // SUBLANES
    xp = xp_ref[0] * jnp.where(blk > 0, 1.0, 0.0)
    xn = xn_ref[0] * jnp.where(blk < nb - 1, 1.0, 0.0)
    xe = jnp.concatenate([xp, x_ref[0], xn], axis=0)
    he = (xe * _rms_scale(xe) * g_ref[...]).astype(BF16)
    xl = jnp.dot(he, wxl_ref[...], preferred_element_type=F32)
    x3 = xl.reshape(groups + 2, SUBLANES, d_lru)
    sub = lax.broadcasted_iota(jnp.int32, (1, SUBLANES, d_lru), 1)

    def delayed(k):
        r = pltpu.roll(x3, k, axis=1)
        return jnp.where(sub >= k, r[1:groups + 1], r[0:groups])

    ahead = pltpu.roll(x3, SUBLANES - 1, axis=1)
    ahead = jnp.where(sub < SUBLANES - 1, ahead[1:groups + 1], ahead[2:groups + 2])
    cw = cw_ref[...]
    xc = (cb_ref[...] + cw[0:1] * delayed(2) + cw[1:2] * delayed(1) + cw[2:3] * x3[1:groups + 1]
          + cw[3:4] * ahead)
    return xc.reshape(t, d_lru)


def _lru_gates(xc, wg_ref, bg_ref):
    return [
        jnp.dot(xc[:, n * LRU_BLOCK:(n + 1) * LRU_BLOCK].astype(BF16), wg_ref[n], preferred_element_type=F32)
        + bg_ref[n]
        for n in range(N_LRU_BLOCKS)
    ]


def _lru_scan(xc, gate_pre, cneg_ref, a_scr, u_scr, carry_ref, write_rows, *, reverse, t):
    groups = t // SUBLANES
    sub = lax.broadcasted_iota(jnp.int32, (1, SUBLANES, LRU_BLOCK), 1)
    for n in range(N_LRU_BLOCKS):
        cols = slice(n * LRU_BLOCK, (n + 1) * LRU_BLOCK)
        xcn = xc[:, cols]
        gz = gate_pre[n]
        r = jax.nn.sigmoid(gz[:, :LRU_BLOCK])
        ig = jax.nn.sigmoid(gz[:, LRU_BLOCK:])
        a = jnp.exp2(cneg_ref[:, cols] * r)
        y = (1.0 - a) * (1.0 + a)
        u = (y * lax.rsqrt(jnp.maximum(y, TINY))) * (ig * xcn)
        a3 = a.reshape(groups, SUBLANES, LRU_BLOCK)
        u3 = u.reshape(groups, SUBLANES, LRU_BLOCK)
        for dist in (1, 2, 4):
            if reverse:
                shift, ok = SUBLANES - dist, sub < SUBLANES - dist
            else:
                shift, ok = dist, sub >= dist
            a_nb = pltpu.roll(a3, shift, axis=1)
            u_nb = pltpu.roll(u3, shift, axis=1)
            u3 = jnp.where(ok, a3 * u_nb + u3, u3)
            a3 = jnp.where(ok, a3 * a_nb, a3)
        a_scr[:, :, cols] = a3
        u_scr[:, :, cols] = u3

    edge = 0 if reverse else SUBLANES - 1
    for n in range(N_LRU_BLOCKS):
        cols = slice(n * LRU_BLOCK, (n + 1) * LRU_BLOCK)
        hrow = carry_ref[:, cols]
        for j in range(groups):
            jj = groups - 1 - j if reverse else j
            rows = u_scr[jj, :, cols] + a_scr[jj, :, cols] * hrow
            write_rows(jj, cols, rows)
            hrow = jnp.broadcast_to(rows[edge:edge + 1, :], (SUBLANES, LRU_BLOCK))
        carry_ref[:, cols] = hrow


def _lru_f_kernel(x_ref, xp_ref, xn_ref, g_ref, wxl_ref, cw_ref, cb_ref, wg_ref, bg_ref, cneg_ref,
                  hf_ref, xc_ref, a_scr, u_scr, carry_ref, *, t):
    def write_rows(j, cols, rows):
        hf_ref[0, j, :, cols] = rows

    _reset_carry_at_sequence_start(carry_ref)
    xc = _lru_inputs(x_ref, xp_ref, xn_ref, g_ref, wxl_ref, cw_ref, cb_ref, t=t)
    xc_ref[0] = xc
    gate_pre = _lru_gates(xc, wg_ref, bg_ref)
    _lru_scan(xc, gate_pre, cneg_ref, a_scr, u_scr, carry_ref, write_rows, reverse=False, t=t)


def _merge_kernel(x_ref, g_ref, xc_ref, wg_ref, bg_ref, cneg_ref,
                  hf_ref, att_ref, wgl_ref, wga_ref, wm_ref, bm_ref, wb0_ref, wb1_ref, wo_ref, gfin_ref,
                  y_ref, a_scr, u_scr, carry_ref, hb_scr, *, t, final_norm):
    def write_rows(j, cols, rows):
        hb_scr[j, :, cols] = rows

    _reset_carry_at_sequence_start(carry_ref)
    d = x_ref.shape[2]
    x = x_ref[0]
    h = (x * _rms_scale(x) * g_ref[...]).astype(BF16)
    xc = xc_ref[0]
    gate_pre = _lru_gates(xc, wg_ref, bg_ref)
    _lru_scan(xc, gate_pre, cneg_ref, a_scr, u_scr, carry_ref, write_rows, reverse=True, t=t)
    g_l = jnp.dot(h, wgl_ref[...], preferred_element_type=F32)
    lsum = (hf_ref[0] + hb_scr[...]).reshape(t, xc.shape[1])
    l_out = (lsum * (g_l * jax.nn.sigmoid(g_l))).astype(BF16)
    g_a = jnp.dot(h, wga_ref[...], preferred_element_type=F32)
    a_out = (att_ref[0].astype(F32) * (g_a * jax.nn.sigmoid(g_a))).astype(BF16)
    a_proj = jnp.dot(a_out, wb0_ref[...], preferred_element_type=F32)
    l_proj = jnp.dot(l_out, wb1_ref[...], preferred_element_type=F32)
    gates = jax.nn.sigmoid(jnp.dot(h, wm_ref[...], preferred_element_type=F32) + bm_ref[...])
    merged = gates[:, :d] * a_proj + gates[:, d:] * l_proj
    y = x + jnp.dot(merged.astype(BF16), wo_ref[...], preferred_element_type=F32)
    if final_norm:
        y = y * _rms_scale(y) * gfin_ref[...]
    y_ref[0] = y


def _lru_specs(s, t, d, d_lru, reverse):
    nb = s // t
    groups = t // SUBLANES
    last_group = s // SUBLANES - 1

    def blk(i):
        return nb - 1 - i if reverse else i

    x_spec = pl.BlockSpec((1, t, d), lambda bi, i: (bi, blk(i), 0))
    xp_spec = pl.BlockSpec((1, HALO, d), lambda bi, i: (bi, jnp.maximum(blk(i) * groups - 1, 0), 0))
    xn_spec = pl.BlockSpec((1, HALO, d), lambda bi, i: (bi, jnp.minimum((blk(i) + 1) * groups, last_group), 0))
    hf_spec = pl.BlockSpec((1, groups, SUBLANES, d_lru), lambda bi, i: (bi, blk(i), 0, 0))
    scratch = [
        pltpu.VMEM((groups, SUBLANES, d_lru), F32),
        pltpu.VMEM((groups, SUBLANES, d_lru), F32),
        pltpu.VMEM((SUBLANES, d_lru), F32),
    ]
    return x_spec, xp_spec, xn_spec, hf_spec, scratch, blk


def _lru_f_call(x, g_in, w_xl, conv_w, conv_b, w_gate, b_gate, cneg):
    b, s, d = x.shape
    d_lru = w_xl.shape[1]
    t = min(LRU_ROWS, s)
    x_spec, xp_spec, xn_spec, hf_spec, scratch, _ = _lru_specs(s, t, d, d_lru, False)
    return pl.pallas_call(
        functools.partial(_lru_f_kernel, t=t),
        grid=(b, s // t),
        in_specs=[x_spec, xp_spec, xn_spec, _const_spec((1, d)), _const_spec(w_xl.shape),
                  _const_spec(conv_w.shape), _const_spec(conv_b.shape), _const_spec(w_gate.shape),
                  _const_spec(b_gate.shape), _const_spec(cneg.shape)],
        out_specs=[hf_spec, pl.BlockSpec((1, t, d_lru), lambda bi, i: (bi, i, 0))],
        out_shape=[jax.ShapeDtypeStruct((b, s // SUBLANES, SUBLANES, d_lru), F32),
                   jax.ShapeDtypeStruct((b, s, d_lru), F32)],
        scratch_shapes=scratch,
        compiler_params=_params(2, True),
        name="lru_f",
    )(x, x, x, g_in, w_xl, conv_w, conv_b, w_gate, b_gate, cneg)


def _merge_call(x, g_in, xc, w_gate, b_gate, cneg, hf, att, w_gl, w_ga, w_m, b_m, w_b0, w_b1, w_o, g_fin, final_norm):
    b, s, d = x.shape
    d_lru = xc.shape[2]
    t = min(LRU_ROWS, s)
    _, _, _, hf_spec, scratch, blk = _lru_specs(s, t, d, d_lru, True)
    tok = lambda width: pl.BlockSpec((1, t, width), lambda bi, i: (bi, blk(i), 0))
    consts = [w_gate, b_gate, cneg]
    tail = [w_gl, w_ga, w_m, b_m, w_b0, w_b1, w_o, g_fin]
    return pl.pallas_call(
        functools.partial(_merge_kernel, t=t, final_norm=final_norm),
        grid=(b, s // t),
        in_specs=[tok(d), _const_spec(g_in.shape), tok(d_lru)] + [_const_spec(c.shape) for c in consts]
        + [hf_spec, tok(att.shape[2])] + [_const_spec(c.shape) for c in tail],
        out_specs=tok(d),
        out_shape=jax.ShapeDtypeStruct((b, s, d), F32),
        scratch_shapes=scratch + [pltpu.VMEM((t // SUBLANES, SUBLANES, d_lru), F32)],
        compiler_params=_params(2, True),
        name="merge",
    )(x, g_in, xc, *consts, hf, att, *tail)


def _rope_tables(s, gain, scale):
    rows_n = s // GRID_W
    rows = jnp.repeat(jnp.arange(rows_n, dtype=F32), GRID_W)
    cols = jnp.tile(jnp.arange(GRID_W, dtype=F32), rows_n)
    n_pair_axis = HEAD_DIM // 4
    inv_freq = ROPE_THETA ** (-jnp.arange(n_pair_axis, dtype=F32) / n_pair_axis)
    ang = jnp.concatenate([rows[:, None] * inv_freq, cols[:, None] * inv_freq], axis=-1)
    cos = jnp.cos(ang).T
    sin = jnp.sin(ang).T
    c = jnp.concatenate([cos, cos], axis=0)
    sn = jnp.concatenate([-sin, sin], axis=0)
    gain = gain.astype(F32)
    return (gain[:, None] * c) * scale, (jnp.roll(gain, HALF)[:, None] * sn) * scale


def _layer_params(w_in, q_norm, k_norm, w_rgate, b_rgate, w_igate, b_igate, lam, w_branch, w_out, d):
    d_attn = N_HEADS * HEAD_DIM
    d_kv = N_KV_HEADS * HEAD_DIM
    d_lru = N_LRU_BLOCKS * LRU_BLOCK
    splits = [d_attn, d_kv, d_kv, d_attn, d_lru, d_lru]
    offs = [0]
    for w in splits:
        offs.append(offs[-1] + w)
    w_q, w_k, w_v, w_ga, w_xl, w_gl = (w_in[:, offs[j]:offs[j + 1]] for j in range(6))
    w_m = w_in[:, offs[6]:]
    perm = jnp.concatenate([jnp.arange(0, HEAD_DIM, 2), jnp.arange(1, HEAD_DIM, 2)])
    permute = lambda w, nh: w.reshape(d, nh, HEAD_DIM)[:, :, perm].reshape(d, nh * HEAD_DIM)
    w_qkvT = jnp.concatenate([permute(w_q, N_HEADS), permute(w_k, N_KV_HEADS), w_v], axis=1).T.astype(BF16)
    w_gate = jnp.concatenate([w_rgate, w_igate], axis=-1).astype(BF16)
    b_gate = jnp.concatenate([b_rgate.reshape(2, N_LRU_BLOCKS, 1, LRU_BLOCK),
                              b_igate.reshape(2, N_LRU_BLOCKS, 1, LRU_BLOCK)], axis=-1).astype(F32)
    cneg = (-RG_C * math.log2(math.e) * jax.nn.softplus(-lam.astype(F32))).reshape(2, 1, d_lru)
    return dict(
        w_qkvT=w_qkvT, gq=q_norm[perm], gk=k_norm[perm],
        w_ga=w_ga.astype(BF16), w_xl=w_xl.astype(BF16), w_gl=w_gl.astype(BF16), w_m=w_m.astype(BF16),
        w_gate=w_gate, b_gate=b_gate, cneg=cneg,
        w_b0=w_branch[0].astype(BF16), w_b1=w_branch[1].astype(BF16), w_o=w_out.astype(BF16),
    )


def _layer(x, p, g_in, conv_w, conv_b, b_m, g_fin, final_norm):
    s = x.shape[1]
    q_scale = math.log2(math.e) / math.sqrt(HEAD_DIM)
    cq, sq = _rope_tables(s, p["gq"], q_scale)
    ck, sk = _rope_tables(s, p["gk"], 1.0)
    qT, k, vT = _qkv_call(x, g_in, p["w_qkvT"], cq, sq, ck, sk)
    att = _attn_call(qT, k, vT)
    lru = (g_in, p["w_xl"], conv_w, conv_b)
    hf, xc = _lru_f_call(x, *lru, p["w_gate"][0], p["b_gate"][0], p["cneg"][0])
    return _merge_call(x, g_in, xc, p["w_gate"][1], p["b_gate"][1], p["cneg"][1], hf, att,
                       p["w_gl"], p["w_ga"], p["w_m"], b_m, p["w_b0"], p["w_b1"], p["w_o"], g_fin, final_norm)


def kernel(x_prompt, x_sample, norm_in, w_in, b_merge, q_norm, k_norm, conv_w, conv_b, w_rgate, b_rgate,
           w_igate, b_igate, lam, w_branch, w_out, norm_final):
    depth, d = norm_in.shape
    layers = [
        _layer_params(w_in[l], q_norm[l], k_norm[l], w_rgate[l], b_rgate[l], w_igate[l], b_igate[l], lam[l],
                      w_branch[l], w_out[l], d)
        for l in range(depth)
    ]
    g_fin = norm_final.reshape(1, d).astype(F32)
    outs = []
    for x in (x_prompt, x_sample):
        for l in range(depth):
            x = _layer(x, layers[l], norm_in[l].reshape(1, d).astype(F32), conv_w[l].astype(F32),
                       conv_b[l].reshape(1, -1).astype(F32), b_merge[l].reshape(1, -1).astype(F32), g_fin,
                       l == depth - 1)
        outs.append(x)
    return tuple(outs)
```

```python
import functools
import math

import jax
import jax.numpy as jnp
from jax import lax
from jax.experimental import pallas as pl
from jax.experimental.pallas import tpu as pltpu

N_HEADS = 8
N_KV_HEADS = 2
GROUP = N_HEADS // N_KV_HEADS
HEAD_DIM = 128
HALF = HEAD_DIM // 2
N_LRU_BLOCKS = 8
LRU_BLOCK = 128
CONV_W = 4
RG_C = 8.0
EPS = 1e-6
TINY = 1e-30
GRID_W = 64
ROPE_THETA = 10000.0

SUBLANES = 8
ONES_ROWS = 2 * SUBLANES
HALO = SUBLANES
V7X_VMEM_LIMIT = 56 * 1024 * 1024

QKV_ROWS = 1024
ATTN_Q = 512
ATTN_K = 256
AHEAD = 2
LRU_ROWS = 256

F32 = jnp.float32
BF16 = jnp.bfloat16


def _const_spec(shape):
    nd = len(shape)
    return pl.BlockSpec(shape, lambda *_: (0,) * nd, pipeline_mode=pl.Buffered(1))


def _params(n_axes, sequential_last, all_sequential=False):
    sem = ["arbitrary" if all_sequential else "parallel"] * n_axes
    if sequential_last:
        sem[-1] = "arbitrary"
    return pltpu.CompilerParams(dimension_semantics=tuple(sem), vmem_limit_bytes=V7X_VMEM_LIMIT)


def _rms_scale(x):
    return lax.rsqrt(jnp.mean(x * x, axis=-1, keepdims=True) + EPS)


def _qkv_kernel(x_ref, g_ref, w_ref, cq_ref, sq_ref, ck_ref, sk_ref, qT_ref, k_ref, vT_ref):
    x = x_ref[0]
    h = (x * _rms_scale(x) * g_ref[...]).astype(BF16)
    zT = lax.dot_general(w_ref[...], h, (((1,), (1,)), ((), ())), preferred_element_type=F32)

    def norm_rope(z, c, s):
        n = z * lax.rsqrt(jnp.mean(z * z, axis=0, keepdims=True) + EPS)
        swapped = jnp.concatenate([n[HALF:], n[:HALF]], axis=0)
        return n * c + swapped * s

    cq, sq = cq_ref[...], sq_ref[...]
    for hd in range(N_HEADS):
        rows = slice(hd * HEAD_DIM, (hd + 1) * HEAD_DIM)
        qh = norm_rope(zT[rows], cq, sq).astype(BF16)
        tq = qT_ref.shape[3]
        for qb in range(qT_ref.shape[1]):
            qT_ref[0, qb, rows, :] = qh[:, qb * tq:(qb + 1) * tq]
    ck, sk = ck_ref[...], sk_ref[...]
    k_off = N_HEADS * HEAD_DIM
    for kv in range(N_KV_HEADS):
        rows = slice(k_off + kv * HEAD_DIM, k_off + (kv + 1) * HEAD_DIM)
        kT = norm_rope(zT[rows], ck, sk)
        k_ref[0, :, kv * HEAD_DIM:(kv + 1) * HEAD_DIM] = kT.T.astype(BF16)
    v_off = k_off + N_KV_HEADS * HEAD_DIM
    vT_ref[0] = zT[v_off:v_off + N_KV_HEADS * HEAD_DIM].astype(BF16)


def _qkv_call(x, g_in, w_qkvT, cq, sq, ck, sk):
    b, s, d = x.shape
    t = min(QKV_ROWS, s)
    tq = min(ATTN_Q, t)
    n_q, n_kv = N_HEADS * HEAD_DIM, N_KV_HEADS * HEAD_DIM
    tab = pl.BlockSpec((HEAD_DIM, t), lambda bi, i: (0, i))
    return pl.pallas_call(
        _qkv_kernel,
        grid=(b, s // t),
        in_specs=[
            pl.BlockSpec((1, t, d), lambda bi, i: (bi, i, 0)),
            _const_spec((1, d)),
            _const_spec(w_qkvT.shape),
            tab, tab, tab, tab,
        ],
        out_specs=[
            pl.BlockSpec((1, t // tq, n_q, tq), lambda bi, i: (bi, i, 0, 0)),
            pl.BlockSpec((1, t, n_kv), lambda bi, i: (bi, i, 0)),
            pl.BlockSpec((1, n_kv, t), lambda bi, i: (bi, 0, i)),
        ],
        out_shape=[
            jax.ShapeDtypeStruct((b, s // tq, n_q, tq), BF16),
            jax.ShapeDtypeStruct((b, s, n_kv), BF16),
            jax.ShapeDtypeStruct((b, n_kv, s), BF16),
        ],
        compiler_params=_params(2, False),
        name="qkv",
    )(x, g_in, w_qkvT, cq, sq, ck, sk)


def _attn_kernel(zero_ref, qT_ref, k_ref, vT_ref, o_ref, m_ref, acc_ref, cmax_ref, *s_slots, n_chunks, tk):
    @pl.when((pl.program_id(0) == 0) & (pl.program_id(1) == 0))
    def _():
        acc_ref[...] = jnp.zeros(acc_ref.shape, F32)

    m_ref[...] = jnp.full(m_ref.shape, -jnp.inf, F32)
    ones_rows = jnp.ones((ONES_ROWS, tk), BF16)
    zw, zr = zero_ref[0], zero_ref[1]

    def kv_cols(hd):
        kv = hd // GROUP
        return slice(kv * HEAD_DIM, (kv + 1) * HEAD_DIM)

    def scores(c, hd):
        off = pl.multiple_of(c * tk, tk)
        kc = k_ref[0, pl.ds(off, tk), kv_cols(hd)]
        qT = qT_ref[0, 0, hd * HEAD_DIM:(hd + 1) * HEAD_DIM, :]
        s = jnp.dot(kc, qT, preferred_element_type=F32)
        s_slots[hd][zw] = s
        cmax_ref[hd] = jnp.max(s, axis=0, keepdims=True)

    def accumulate(c, hd):
        off = pl.multiple_of(c * tk, tk)
        m_old = m_ref[hd:hd + 1, :]
        m_new = jnp.maximum(m_old, cmax_ref[hd])
        m_ref[hd:hd + 1, :] = m_new
        p = jnp.exp2(s_slots[hd][zr] - m_new).astype(BF16)
        vc = jnp.concatenate([vT_ref[0, kv_cols(hd), pl.ds(off, tk)], ones_rows], axis=0)
        acc_ref[hd] = jnp.exp2(m_old - m_new) * acc_ref[hd] + jnp.dot(vc, p, preferred_element_type=F32)

    for hd in range(AHEAD):
        scores(0, hd)

    def step(c, carry):
        for hd in range(N_HEADS):
            w = hd + AHEAD
            scores(jnp.minimum(c + w // N_HEADS, n_chunks - 1), w % N_HEADS)
            accumulate(c, hd)
        return carry

    lax.fori_loop(0, n_chunks, step, 0, unroll=8)

    for hd in range(N_HEADS):
        o = acc_ref[hd, :HEAD_DIM, :] * (1.0 / acc_ref[hd, HEAD_DIM:HEAD_DIM + 1, :])
        o_ref[0, :, hd * HEAD_DIM:(hd + 1) * HEAD_DIM] = o.T.astype(BF16)


def _attn_call(qT, k, vT):
    b, n_qb, n_q, tq = qT.shape
    s, n_kv = k.shape[1:]
    tk = min(ATTN_K, s)
    kernel = functools.partial(_attn_kernel, n_chunks=s // tk, tk=tk)
    kv_buffers = pl.Buffered(2 if b > 1 else 1)
    return pl.pallas_call(
        kernel,
        grid=(b, n_qb),
        in_specs=[
            pl.BlockSpec(memory_space=pltpu.SMEM),
            pl.BlockSpec((1, 1, n_q, tq), lambda bi, i: (bi, i, 0, 0)),
            pl.BlockSpec((1, s, n_kv), lambda bi, i: (bi, 0, 0), pipeline_mode=kv_buffers),
            pl.BlockSpec((1, n_kv, s), lambda bi, i: (bi, 0, 0), pipeline_mode=kv_buffers),
        ],
        out_specs=pl.BlockSpec((1, tq, n_q), lambda bi, i: (bi, i, 0)),
        out_shape=jax.ShapeDtypeStruct((b, s, n_q), BF16),
        scratch_shapes=[
            pltpu.VMEM((N_HEADS, tq), F32),
            pltpu.VMEM((N_HEADS, HEAD_DIM + ONES_ROWS, tq), F32),
            pltpu.VMEM((N_HEADS, 1, tq), F32),
        ] + [pltpu.VMEM((1, tk, tq), F32)] * N_HEADS,
        compiler_params=_params(2, True, all_sequential=True),
        name="attn",
    )(jnp.zeros((2,), jnp.int32), qT, k, vT)


def _reset_carry_at_sequence_start(carry_ref):
    @pl.when(pl.program_id(1) == 0)
    def _():
        carry_ref[...] = jnp.zeros(carry_ref.shape, F32)


def _lru_inputs(x_ref, xp_ref, xn_ref, g_ref, wxl_ref, cw_ref, cb_ref, *, t):
    blk = pl.program_id(1)
    nb = pl.num_programs(1)
    d_lru = wxl_ref.shape[1]
    groups = t // SUBLANES
    xp = xp_ref[0] * jnp.where(blk > 0, 1.0, 0.0)
    xn = xn_ref[0] * jnp.where(blk < nb - 1, 1.0, 0.0)
    xe = jnp.concatenate([xp, x_ref[0], xn], axis=0)
    he = (xe * _rms_scale(xe) * g_ref[...]).astype(BF16)
    xl = jnp.dot(he, wxl_ref[...], preferred_element_type=F32)
    x3 = xl.reshape(groups + 2, SUBLANES, d_lru)
    sub = lax.broadcasted_iota(jnp.int32, (1, SUBLANES, d_lru), 1)

    def delayed(k):
        r = pltpu.roll(x3, k, axis=1)
        return jnp.where(sub >= k, r[1:groups + 1], r[0:groups])

    ahead = pltpu.roll(x3, SUBLANES - 1, axis=1)
    ahead = jnp.where(sub < SUBLANES - 1, ahead[1:groups + 1], ahead[2:groups + 2])
    cw = cw_ref[...]
    xc = (cb_ref[...] + cw[0:1] * delayed(2) + cw[1:2] * delayed(1) + cw[2:3] * x3[1:groups + 1]
          + cw[3:4] * ahead)
    return xc.reshape(t, d_lru)


def _lru_gates(xc, wg_ref, bg_ref):
    return [
        jnp.dot(xc[:, n * LRU_BLOCK:(n + 1) * LRU_BLOCK].astype(BF16), wg_ref[n], preferred_element_type=F32)
        + bg_ref[n]
        for n in range(N_LRU_BLOCKS)
    ]


def _lru_scan(xc, gate_pre, cneg_ref, a_scr, u_scr, carry_ref, write_rows, *, reverse, t):
    groups = t // SUBLANES
    sub = lax.broadcasted_iota(jnp.int32, (1, SUBLANES, LRU_BLOCK), 1)
    for n in range(N_LRU_BLOCKS):
        cols = slice(n * LRU_BLOCK, (n + 1) * LRU_BLOCK)
        xcn = xc[:, cols]
        gz = gate_pre[n]
        r = jax.nn.sigmoid(gz[:, :LRU_BLOCK])
        ig = jax.nn.sigmoid(gz[:, LRU_BLOCK:])
        a = jnp.exp2(cneg_ref[:, cols] * r)
        y = (1.0 - a) * (1.0 + a)
        u = (y * lax.rsqrt(jnp.maximum(y, TINY))) * (ig * xcn)
        a3 = a.reshape(groups, SUBLANES, LRU_BLOCK)
        u3 = u.reshape(groups, SUBLANES, LRU_BLOCK)
        for dist in (1, 2, 4):
            if reverse:
                shift, ok = SUBLANES - dist, sub < SUBLANES - dist
            else:
                shift, ok = dist, sub >= dist
            a_nb = pltpu.roll(a3, shift, axis=1)
            u_nb = pltpu.roll(u3, shift, axis=1)
            u3 = jnp.where(ok, a3 * u_nb + u3, u3)
            a3 = jnp.where(ok, a3 * a_nb, a3)
        a_scr[:, :, cols] = a3
        u_scr[:, :, cols] = u3

    edge = 0 if reverse else SUBLANES - 1
    for n in range(N_LRU_BLOCKS):
        cols = slice(n * LRU_BLOCK, (n + 1) * LRU_BLOCK)
        hrow = carry_ref[:, cols]
        for j in range(groups):
            jj = groups - 1 - j if reverse else j
            rows = u_scr[jj, :, cols] + a_scr[jj, :, cols] * hrow
            write_rows(jj, cols, rows)
            hrow = jnp.broadcast_to(rows[edge:edge + 1, :], (SUBLANES, LRU_BLOCK))
        carry_ref[:, cols] = hrow


def _lru_f_kernel(x_ref, xp_ref, xn_ref, g_ref, wxl_ref, cw_ref, cb_ref, wg_ref, bg_ref, cneg_ref,
                  hf_ref, xc_ref, a_scr, u_scr, carry_ref, *, t):
    def write_rows(j, cols, rows):
        hf_ref[0, j, :, cols] = rows

    _reset_carry_at_sequence_start(carry_ref)
    xc = _lru_inputs(x_ref, xp_ref, xn_ref, g_ref, wxl_ref, cw_ref, cb_ref, t=t)
    xc_ref[0] = xc
    gate_pre = _lru_gates(xc, wg_ref, bg_ref)
    _lru_scan(xc, gate_pre, cneg_ref, a_scr, u_scr, carry_ref, write_rows, reverse=False, t=t)


def _merge_kernel(x_ref, g_ref, xc_ref, wg_ref, bg_ref, cneg_ref,
                  hf_ref, att_ref, wgl_ref, wga_ref, wm_ref, bm_ref, wb0_ref, wb1_ref, wo_ref, gfin_ref,
                  y_ref, a_scr, u_scr, carry_ref, hb_scr, *, t, final_norm):
    def write_rows(j, cols, rows):
        hb_scr[j, :, cols] = rows

    _reset_carry_at_sequence_start(carry_ref)
    d = x_ref.shape[2]
    x = x_ref[0]
    h = (x * _rms_scale(x) * g_ref[...]).astype(BF16)
    xc = xc_ref[0]
    gate_pre = _lru_gates(xc, wg_ref, bg_ref)
    _lru_scan(xc, gate_pre, cneg_ref, a_scr, u_scr, carry_ref, write_rows, reverse=True, t=t)
    g_l = jnp.dot(h, wgl_ref[...], preferred_element_type=F32)
    lsum = (hf_ref[0] + hb_scr[...]).reshape(t, xc.shape[1])
    l_out = (lsum * (g_l * jax.nn.sigmoid(g_l))).astype(BF16)
    g_a = jnp.dot(h, wga_ref[...], preferred_element_type=F32)
    a_out = (att_ref[0].astype(F32) * (g_a * jax.nn.sigmoid(g_a))).astype(BF16)
    a_proj = jnp.dot(a_out, wb0_ref[...], preferred_element_type=F32)
    l_proj = jnp.dot(l_out, wb1_ref[...], preferred_element_type=F32)
    gates = jax.nn.sigmoid(jnp.dot(h, wm_ref[...], preferred_element_type=F32) + bm_ref[...])
    merged = gates[:, :d] * a_proj + gates[:, d:] * l_proj
    y = x + jnp.dot(merged.astype(BF16), wo_ref[...], preferred_element_type=F32)
    if final_norm:
        y = y * _rms_scale(y) * gfin_ref[...]
    y_ref[0] = y


def _lru_specs(s, t, d, d_lru, reverse):
    nb = s // t
    groups = t // SUBLANES
    last_group = s // SUBLANES - 1

    def blk(i):
        return nb - 1 - i if reverse else i

    x_spec = pl.BlockSpec((1, t, d), lambda bi, i: (bi, blk(i), 0))
    xp_spec = pl.BlockSpec((1, HALO, d), lambda bi, i: (bi, jnp.maximum(blk(i) * groups - 1, 0), 0))
    xn_spec = pl.BlockSpec((1, HALO, d), lambda bi, i: (bi, jnp.minimum((blk(i) + 1) * groups, last_group), 0))
    hf_spec = pl.BlockSpec((1, groups, SUBLANES, d_lru), lambda bi, i: (bi, blk(i), 0, 0))
    scratch = [
        pltpu.VMEM((groups, SUBLANES, d_lru), F32),
        pltpu.VMEM((groups, SUBLANES, d_lru), F32),
        pltpu.VMEM((SUBLANES, d_lru), F32),
    ]
    return x_spec, xp_spec, xn_spec, hf_spec, scratch, blk


def _lru_f_call(x, g_in, w_xl, conv_w, conv_b, w_gate, b_gate, cneg):
    b, s, d = x.shape
    d_lru = w_xl.shape[1]
    t = min(LRU_ROWS, s)
    x_spec, xp_spec, xn_spec, hf_spec, scratch, _ = _lru_specs(s, t, d, d_lru, False)
    return pl.pallas_call(
        functools.partial(_lru_f_kernel, t=t),
        grid=(b, s // t),
        in_specs=[x_spec, xp_spec, xn_spec, _const_spec((1, d)), _const_spec(w_xl.shape),
                  _const_spec(conv_w.shape), _const_spec(conv_b.shape), _const_spec(w_gate.shape),
                  _const_spec(b_gate.shape), _const_spec(cneg.shape)],
        out_specs=[hf_spec, pl.BlockSpec((1, t, d_lru), lambda bi, i: (bi, i, 0))],
        out_shape=[jax.ShapeDtypeStruct((b, s // SUBLANES, SUBLANES, d_lru), F32),
                   jax.ShapeDtypeStruct((b, s, d_lru), F32)],
        scratch_shapes=scratch,
        compiler_params=_params(2, True),
        name="lru_f",
    )(x, x, x, g_in, w_xl, conv_w, conv_b, w_gate, b_gate, cneg)


def _merge_call(x, g_in, xc, w_gate, b_gate, cneg, hf, att, w_gl, w_ga, w_m, b_m, w_b0, w_b1, w_o, g_fin, final_norm):
    b, s, d = x.shape
    d_lru = xc.shape[2]
    t = min(LRU_ROWS, s)
    _, _, _, hf_spec, scratch, blk = _lru_specs(s, t, d, d_lru, True)
    tok = lambda width: pl.BlockSpec((1, t, width), lambda bi, i: (bi, blk(i), 0))
    consts = [w_gate, b_gate, cneg]
    tail = [w_gl, w_ga, w_m, b_m, w_b0, w_b1, w_o, g_fin]
    return pl.pallas_call(
        functools.partial(_merge_kernel, t=t, final_norm=final_norm),
        grid=(b, s // t),
        in_specs=[tok(d), _const_spec(g_in.shape), tok(d_lru)] + [_const_spec(c.shape) for c in consts]
        + [hf_spec, tok(att.shape[2])] + [_const_spec(c.shape) for c in tail],
        out_specs=tok(d),
        out_shape=jax.ShapeDtypeStruct((b, s, d), F32),
        scratch_shapes=scratch + [pltpu.VMEM((t // SUBLANES, SUBLANES, d_lru), F32)],
        compiler_params=_params(2, True),
        name="merge",
    )(x, g_in, xc, *consts, hf, att, *tail)


def _rope_tables(s, gain, scale):
    rows_n = s // GRID_W
    rows = jnp.repeat(jnp.arange(rows_n, dtype=F32), GRID_W)
    cols = jnp.tile(jnp.arange(GRID_W, dtype=F32), rows_n)
    n_pair_axis = HEAD_DIM // 4
    inv_freq = ROPE_THETA ** (-jnp.arange(n_pair_axis, dtype=F32) / n_pair_axis)
    ang = jnp.concatenate([rows[:, None] * inv_freq, cols[:, None] * inv_freq], axis=-1)
    cos = jnp.cos(ang).T
    sin = jnp.sin(ang).T
    c = jnp.concatenate([cos, cos], axis=0)
    sn = jnp.concatenate([-sin, sin], axis=0)
    gain = gain.astype(F32)
    return (gain[:, None] * c) * scale, (jnp.roll(gain, HALF)[:, None] * sn) * scale


def _layer_params(w_in, q_norm, k_norm, w_rgate, b_rgate, w_igate, b_igate, lam, w_branch, w_out, d):
    d_attn = N_HEADS * HEAD_DIM
    d_kv = N_KV_HEADS * HEAD_DIM
    d_lru = N_LRU_BLOCKS * LRU_BLOCK
    splits = [d_attn, d_kv, d_kv, d_attn, d_lru, d_lru]
    offs = [0]
    for w in splits:
        offs.append(offs[-1] + w)
    w_q, w_k, w_v, w_ga, w_xl, w_gl = (w_in[:, offs[j]:offs[j + 1]] for j in range(6))
    w_m = w_in[:, offs[6]:]
    perm = jnp.concatenate([jnp.arange(0, HEAD_DIM, 2), jnp.arange(1, HEAD_DIM, 2)])
    permute = lambda w, nh: w.reshape(d, nh, HEAD_DIM)[:, :, perm].reshape(d, nh * HEAD_DIM)
    w_qkvT = jnp.concatenate([permute(w_q, N_HEADS), permute(w_k, N_KV_HEADS), w_v], axis=1).T.astype(BF16)
    w_gate = jnp.concatenate([w_rgate, w_igate], axis=-1).astype(BF16)
    b_gate = jnp.concatenate([b_rgate.reshape(2, N_LRU_BLOCKS, 1, LRU_BLOCK),
                              b_igate.reshape(2, N_LRU_BLOCKS, 1, LRU_BLOCK)], axis=-1).astype(F32)
    cneg = (-RG_C * math.log2(math.e) * jax.nn.softplus(-lam.astype(F32))).reshape(2, 1, d_lru)
    return dict(
        w_qkvT=w_qkvT, gq=q_norm[perm], gk=k_norm[perm],
        w_ga=w_ga.astype(BF16), w_xl=w_xl.astype(BF16), w_gl=w_gl.astype(BF16), w_m=w_m.astype(BF16),
        w_gate=w_gate, b_gate=b_gate, cneg=cneg,
        w_b0=w_branch[0].astype(BF16), w_b1=w_branch[1].astype(BF16), w_o=w_out.astype(BF16),
    )


def _layer(x, p, g_in, conv_w, conv_b, b_m, g_fin, final_norm):
    s = x.shape[1]
    q_scale = math.log2(math.e) / math.sqrt(HEAD_DIM)
    cq, sq = _rope_tables(s, p["gq"], q_scale)
    ck, sk = _rope_tables(s, p["gk"], 1.0)
    qT, k, vT = _qkv_call(x, g_in, p["w_qkvT"], cq, sq, ck, sk)
    att = _attn_call(qT, k, vT)
    lru = (g_in, p["w_xl"], conv_w, conv_b)
    hf, xc = _lru_f_call(x, *lru, p["w_gate"][0], p["b_gate"][0], p["cneg"][0])
    return _merge_call(x, g_in, xc, p["w_gate"][1], p["b_gate"][1], p["cneg"][1], hf, att,
                       p["w_gl"], p["w_ga"], p["w_m"], b_m, p["w_b0"], p["w_b1"], p["w_o"], g_fin, final_norm)


def kernel(x_prompt, x_sample, norm_in, w_in, b_merge, q_norm, k_norm, conv_w, conv_b, w_rgate, b_rgate,
           w_igate, b_igate, lam, w_branch, w_out, norm_final):
    depth, d = norm_in.shape
    layers = [
        _layer_params(w_in[l], q_norm[l], k_norm[l], w_rgate[l], b_rgate[l], w_igate[l], b_igate[l], lam[l],
                      w_branch[l], w_out[l], d)
        for l in range(depth)
    ]
    g_fin = norm_final.reshape(1, d).astype(F32)
    outs = []
    for x in (x_prompt, x_sample):
        for l in range(depth):
            x = _layer(x, layers[l], norm_in[l].reshape(1, d).astype(F32), conv_w[l].astype(F32),
                       conv_b[l].reshape(1, -1).astype(F32), b_merge[l].reshape(1, -1).astype(F32), g_fin,
                       l == depth - 1)
        outs.append(x)
    return tuple(outs)
```

```python
import functools
import math

import jax
import jax.numpy as jnp
from jax import lax
from jax.experimental import pallas as pl
from jax.experimental.pallas import tpu as pltpu

N_HEADS = 8
N_KV_HEADS = 2
GROUP = N_HEADS // N_KV_HEADS
HEAD_DIM = 128
HALF = HEAD_DIM // 2
N_LRU_BLOCKS = 8
LRU_BLOCK = 128
CONV_W = 4
RG_C = 8.0
EPS = 1e-6
TINY = 1e-30
GRID_W = 64
ROPE_THETA = 10000.0

SUBLANES = 8
ONES_ROWS = 2 * SUBLANES
HALO = SUBLANES
V7X_VMEM_LIMIT = 56 * 1024 * 1024

QKV_ROWS = 1024
ATTN_Q = 512
ATTN_K = 256
AHEAD = 2
LRU_ROWS = 256

F32 = jnp.float32
BF16 = jnp.bfloat16


def _const_spec(shape):
    nd = len(shape)
    return pl.BlockSpec(shape, lambda *_: (0,) * nd, pipeline_mode=pl.Buffered(1))


def _params(n_axes, sequential_last, all_sequential=False):
    sem = ["arbitrary" if all_sequential else "parallel"] * n_axes
    if sequential_last:
        sem[-1] = "arbitrary"
    return pltpu.CompilerParams(dimension_semantics=tuple(sem), vmem_limit_bytes=V7X_VMEM_LIMIT)


def _rms_scale(x):
    return lax.rsqrt(jnp.mean(x * x, axis=-1, keepdims=True) + EPS)


def _qkv_kernel(x_ref, g_ref, w_ref, cq_ref, sq_ref, ck_ref, sk_ref, qT_ref, k_ref, vT_ref):
    x = x_ref[0]
    h = (x * _rms_scale(x) * g_ref[...]).astype(BF16)
    zT = lax.dot_general(w_ref[...], h, (((1,), (1,)), ((), ())), preferred_element_type=F32)

    def norm_rope(z, c, s):
        n = z * lax.rsqrt(jnp.mean(z * z, axis=0, keepdims=True) + EPS)
        swapped = jnp.concatenate([n[HALF:], n[:HALF]], axis=0)
        return n * c + swapped * s

    cq, sq = cq_ref[...], sq_ref[...]
    for hd in range(N_HEADS):
        rows = slice(hd * HEAD_DIM, (hd + 1) * HEAD_DIM)
        qh = norm_rope(zT[rows], cq, sq).astype(BF16)
        tq = qT_ref.shape[3]
        for qb in range(qT_ref.shape[1]):
            qT_ref[0, qb, rows, :] = qh[:, qb * tq:(qb + 1) * tq]
    ck, sk = ck_ref[...], sk_ref[...]
    k_off = N_HEADS * HEAD_DIM
    for kv in range(N_KV_HEADS):
        rows = slice(k_off + kv * HEAD_DIM, k_off + (kv + 1) * HEAD_DIM)
        kT = norm_rope(zT[rows], ck, sk)
        k_ref[0, :, kv * HEAD_DIM:(kv + 1) * HEAD_DIM] = kT.T.astype(BF16)
    v_off = k_off + N_KV_HEADS * HEAD_DIM
    vT_ref[0] = zT[v_off:v_off + N_KV_HEADS * HEAD_DIM].astype(BF16)


def _qkv_call(x, g_in, w_qkvT, cq, sq, ck, sk):
    b, s, d = x.shape
    t = min(QKV_ROWS, s)
    tq = min(ATTN_Q, t)
    n_q, n_kv = N_HEADS * HEAD_DIM, N_KV_HEADS * HEAD_DIM
    tab = pl.BlockSpec((HEAD_DIM, t), lambda bi, i: (0, i))
    return pl.pallas_call(
        _qkv_kernel,
        grid=(b, s // t),
        in_specs=[
            pl.BlockSpec((1, t, d), lambda bi, i: (bi, i, 0)),
            _const_spec((1, d)),
            _const_spec(w_qkvT.shape),
            tab, tab, tab, tab,
        ],
        out_specs=[
            pl.BlockSpec((1, t // tq, n_q, tq), lambda bi, i: (bi, i, 0, 0)),
            pl.BlockSpec((1, t, n_kv), lambda bi, i: (bi, i, 0)),
            pl.BlockSpec((1, n_kv, t), lambda bi, i: (bi, 0, i)),
        ],
        out_shape=[
            jax.ShapeDtypeStruct((b, s // tq, n_q, tq), BF16),
            jax.ShapeDtypeStruct((b, s, n_kv), BF16),
            jax.ShapeDtypeStruct((b, n_kv, s), BF16),
        ],
        compiler_params=_params(2, False),
        name="qkv",
    )(x, g_in, w_qkvT, cq, sq, ck, sk)


def _attn_kernel(zero_ref, qT_ref, k_ref, vT_ref, o_ref, m_ref, acc_ref, cmax_ref, *s_slots, n_chunks, tk):
    @pl.when((pl.program_id(0) == 0) & (pl.program_id(1) == 0))
    def _():
        acc_ref[...] = jnp.zeros(acc_ref.shape, F32)

    m_ref[...] = jnp.full(m_ref.shape, -jnp.inf, F32)
    ones_rows = jnp.ones((ONES_ROWS, tk), BF16)
    zw, zr = zero_ref[0], zero_ref[1]

    def kv_cols(hd):
        kv = hd // GROUP
        return slice(kv * HEAD_DIM, (kv + 1) * HEAD_DIM)

    def scores(c, hd):
        off = pl.multiple_of(c * tk, tk)
        kc = k_ref[0, pl.ds(off, tk), kv_cols(hd)]
        qT = qT_ref[0, 0, hd * HEAD_DIM:(hd + 1) * HEAD_DIM, :]
        s = jnp.dot(kc, qT, preferred_element_type=F32)
        s_slots[hd][zw] = s
        cmax_ref[hd] = jnp.max(s, axis=0, keepdims=True)

    def accumulate(c, hd):
        off = pl.multiple_of(c * tk, tk)
        m_old = m_ref[hd:hd + 1, :]
        m_new = jnp.maximum(m_old, cmax_ref[hd])
        m_ref[hd:hd + 1, :] = m_new
        p = jnp.exp2(s_slots[hd][zr] - m_new).astype(BF16)
        vc = jnp.concatenate([vT_ref[0, kv_cols(hd), pl.ds(off, tk)], ones_rows], axis=0)
        acc_ref[hd] = jnp.exp2(m_old - m_new) * acc_ref[hd] + jnp.dot(vc, p, preferred_element_type=F32)

    for hd in range(AHEAD):
        scores(0, hd)

    def step(c, carry):
        for hd in range(N_HEADS):
            w = hd + AHEAD
            scores(jnp.minimum(c + w // N_HEADS, n_chunks - 1), w % N_HEADS)
            accumulate(c, hd)
        return carry

    lax.fori_loop(0, n_chunks, step, 0, unroll=16)

    for hd in range(N_HEADS):
        o = acc_ref[hd, :HEAD_DIM, :] * (1.0 / acc_ref[hd, HEAD_DIM:HEAD_DIM + 1, :])
        o_ref[0, :, hd * HEAD_DIM:(hd + 1) * HEAD_DIM] = o.T.astype(BF16)


def _attn_call(qT, k, vT):
    b, n_qb, n_q, tq = qT.shape
    s, n_kv = k.shape[1:]
    tk = min(ATTN_K, s)
    kernel = functools.partial(_attn_kernel, n_chunks=s // tk, tk=tk)
    kv_buffers = pl.Buffered(2 if b > 1 else 1)
    return pl.pallas_call(
        kernel,
        grid=(b, n_qb),
        in_specs=[
            pl.BlockSpec(memory_space=pltpu.SMEM),
            pl.BlockSpec((1, 1, n_q, tq), lambda bi, i: (bi, i, 0, 0)),
            pl.BlockSpec((1, s, n_kv), lambda bi, i: (bi, 0, 0), pipeline_mode=kv_buffers),
            pl.BlockSpec((1, n_kv, s), lambda bi, i: (bi, 0, 0), pipeline_mode=kv_buffers),
        ],
        out_specs=pl.BlockSpec((1, tq, n_q), lambda bi, i: (bi, i, 0)),
        out_shape=jax.ShapeDtypeStruct((b, s, n_q), BF16),
        scratch_shapes=[
            pltpu.VMEM((N_HEADS, tq), F32),
            pltpu.VMEM((N_HEADS, HEAD_DIM + ONES_ROWS, tq), F32),
            pltpu.VMEM((N_HEADS, 1, tq), F32),
        ] + [pltpu.VMEM((1, tk, tq), F32)] * N_HEADS,
        compiler_params=_params(2, True, all_sequential=True),
        name="attn",
    )(jnp.zeros((2,), jnp.int32), qT, k, vT)


def _reset_carry_at_sequence_start(carry_ref):
    @pl.when(pl.program_id(1) == 0)
    def _():
        carry_ref[...] = jnp.zeros(carry_ref.shape, F32)


def _lru_inputs(x_ref, xp_ref, xn_ref, g_ref, wxl_ref, cw_ref, cb_ref, *, t):
    blk = pl.program_id(1)
    nb = pl.num_programs(1)
    d_lru = wxl_ref.shape[1]
    groups = t // SUBLANES
    xp = xp_ref[0] * jnp.where(blk > 0, 1.0, 0.0)
    xn = xn_ref[0] * jnp.where(blk < nb - 1, 1.0, 0.0)
    xe = jnp.concatenate([xp, x_ref[0], xn], axis=0)
    he = (xe * _rms_scale(xe) * g_ref[...]).astype(BF16)
    xl = jnp.dot(he, wxl_ref[...], preferred_element_type=F32)
    x3 = xl.reshape(groups + 2, SUBLANES, d_lru)
    sub = lax.broadcasted_iota(jnp.int32, (1, SUBLANES, d_lru), 1)

    def delayed(k):
        r = pltpu.roll(x3, k, axis=1)
        return jnp.where(sub >= k, r[1:groups + 1], r[0:groups])

    ahead = pltpu.roll(x3, SUBLANES - 1, axis=1)
    ahead = jnp.where(sub < SUBLANES - 1, ahead[1:groups + 1], ahead[2:groups + 2])
    cw = cw_ref[...]
    xc = (cb_ref[...] + cw[0:1] * delayed(2) + cw[1:2] * delayed(1) + cw[2:3] * x3[1:groups + 1]
          + cw[3:4] * ahead)
    return xc.reshape(t, d_lru)


def _lru_gates(xc, wg_ref, bg_ref):
    return [
        jnp.dot(xc[:, n * LRU_BLOCK:(n + 1) * LRU_BLOCK].astype(BF16), wg_ref[n], preferred_element_type=F32)
        + bg_ref[n]
        for n in range(N_LRU_BLOCKS)
    ]


def _lru_scan(xc, gate_pre, cneg_ref, a_scr, u_scr, carry_ref, write_rows, *, reverse, t):
    groups = t // SUBLANES
    sub = lax.broadcasted_iota(jnp.int32, (1, SUBLANES, LRU_BLOCK), 1)
    for n in range(N_LRU_BLOCKS):
        cols = slice(n * LRU_BLOCK, (n + 1) * LRU_BLOCK)
        xcn = xc[:, cols]
        gz = gate_pre[n]
        r = jax.nn.sigmoid(gz[:, :LRU_BLOCK])
        ig = jax.nn.sigmoid(gz[:, LRU_BLOCK:])
        a = jnp.exp2(cneg_ref[:, cols] * r)
        y = (1.0 - a) * (1.0 + a)
        u = (y * lax.rsqrt(jnp.maximum(y, TINY))) * (ig * xcn)
        a3 = a.reshape(groups, SUBLANES, LRU_BLOCK)
        u3 = u.reshape(groups, SUBLANES, LRU_BLOCK)
        for dist in (1, 2, 4):
            if reverse:
                shift, ok = SUBLANES - dist, sub < SUBLANES - dist
            else:
                shift, ok = dist, sub >= dist
            a_nb = pltpu.roll(a3, shift, axis=1)
            u_nb = pltpu.roll(u3, shift, axis=1)
            u3 = jnp.where(ok, a3 * u_nb + u3, u3)
            a3 = jnp.where(ok, a3 * a_nb, a3)
        a_scr[:, :, cols] = a3
        u_scr[:, :, cols] = u3

    edge = 0 if reverse else SUBLANES - 1
    for n in range(N_LRU_BLOCKS):
        cols = slice(n * LRU_BLOCK, (n + 1) * LRU_BLOCK)
        hrow = carry_ref[:, cols]
        for j in range(groups):
            jj = groups - 1 - j if reverse else j
            rows = u_scr[jj, :, cols] + a_scr[jj, :, cols] * hrow
            write_rows(jj, cols, rows)
            hrow = jnp.broadcast_to(rows[edge:edge + 1, :], (SUBLANES, LRU_BLOCK))
        carry_ref[:, cols] = hrow


def _lru_f_kernel(x_ref, xp_ref, xn_ref, g_ref, wxl_ref, cw_ref, cb_ref, wg_ref, bg_ref, cneg_ref,
                  hf_ref, xc_ref, a_scr, u_scr, carry_ref, *, t):
    def write_rows(j, cols, rows):
        hf_ref[0, j, :, cols] = rows

    _reset_carry_at_sequence_start(carry_ref)
    xc = _lru_inputs(x_ref, xp_ref, xn_ref, g_ref, wxl_ref, cw_ref, cb_ref, t=t)
    xc_ref[0] = xc
    gate_pre = _lru_gates(xc, wg_ref, bg_ref)
    _lru_scan(xc, gate_pre, cneg_ref, a_scr, u_scr, carry_ref, write_rows, reverse=False, t=t)


def _merge_kernel(x_ref, g_ref, xc_ref, wg_ref, bg_ref, cneg_ref,
                  hf_ref, att_ref, wgl_ref, wga_ref, wm_ref, bm_ref, wb0_ref, wb1_ref, wo_ref, gfin_ref,
                  y_ref, a_scr, u_scr, carry_ref, hb_scr, *, t, final_norm):
    def write_rows(j, cols, rows):
        hb_scr[j, :, cols] = rows

    _reset_carry_at_sequence_start(carry_ref)
    d = x_ref.shape[2]
    x = x_ref[0]
    h = (x * _rms_scale(x) * g_ref[...]).astype(BF16)
    xc = xc_ref[0]
    gate_pre = _lru_gates(xc, wg_ref, bg_ref)
    _lru_scan(xc, gate_pre, cneg_ref, a_scr, u_scr, carry_ref, write_rows, reverse=True, t=t)
    g_l = jnp.dot(h, wgl_ref[...], preferred_element_type=F32)
    lsum = (hf_ref[0] + hb_scr[...]).reshape(t, xc.shape[1])
    l_out = (lsum * (g_l * jax.nn.sigmoid(g_l))).astype(BF16)
    g_a = jnp.dot(h, wga_ref[...], preferred_element_type=F32)
    a_out = (att_ref[0].astype(F32) * (g_a * jax.nn.sigmoid(g_a))).astype(BF16)
    a_proj = jnp.dot(a_out, wb0_ref[...], preferred_element_type=F32)
    l_proj = jnp.dot(l_out, wb1_ref[...], preferred_element_type=F32)
    gates = jax.nn.sigmoid(jnp.dot(h, wm_ref[...], preferred_element_type=F32) + bm_ref[...])
    merged = gates[:, :d] * a_proj + gates[:, d:] * l_proj
    y = x + jnp.dot(merged.astype(BF16), wo_ref[...], preferred_element_type=F32)
    if final_norm:
        y = y * _rms_scale(y) * gfin_ref[...]
    y_ref[0] = y


def _lru_specs(s, t, d, d_lru, reverse):
    nb = s // t
    groups = t // SUBLANES
    last_group = s // SUBLANES - 1

    def blk(i):
        return nb - 1 - i if reverse else i

    x_spec = pl.BlockSpec((1, t, d), lambda bi, i: (bi, blk(i), 0))
    xp_spec = pl.BlockSpec((1, HALO, d), lambda bi, i: (bi, jnp.maximum(blk(i) * groups - 1, 0), 0))
    xn_spec = pl.BlockSpec((1, HALO, d), lambda bi, i: (bi, jnp.minimum((blk(i) + 1) * groups, last_group), 0))
    hf_spec = pl.BlockSpec((1, groups, SUBLANES, d_lru), lambda bi, i: (bi, blk(i), 0, 0))
    scratch = [
        pltpu.VMEM((groups, SUBLANES, d_lru), F32),
        pltpu.VMEM((groups, SUBLANES, d_lru), F32),
        pltpu.VMEM((SUBLANES, d_lru), F32),
    ]
    return x_spec, xp_spec, xn_spec, hf_spec, scratch, blk


def _lru_f_call(x, g_in, w_xl, conv_w, conv_b, w_gate, b_gate, cneg):
    b, s, d = x.shape
    d_lru = w_xl.shape[1]
    t = min(LRU_ROWS, s)
    x_spec, xp_spec, xn_spec, hf_spec, scratch, _ = _lru_specs(s, t, d, d_lru, False)
    return pl.pallas_call(
        functools.partial(_lru_f_kernel, t=t),
        grid=(b, s // t),
        in_specs=[x_spec, xp_spec, xn_spec, _const_spec((1, d)), _const_spec(w_xl.shape),
                  _const_spec(conv_w.shape), _const_spec(conv_b.shape), _const_spec(w_gate.shape),
                  _const_spec(b_gate.shape), _const_spec(cneg.shape)],
        out_specs=[hf_spec, pl.BlockSpec((1, t, d_lru), lambda bi, i: (bi, i, 0))],
        out_shape=[jax.ShapeDtypeStruct((b, s // SUBLANES, SUBLANES, d_lru), F32),
                   jax.ShapeDtypeStruct((b, s, d_lru), F32)],
        scratch_shapes=scratch,
        compiler_params=_params(2, True),
        name="lru_f",
    )(x, x, x, g_in, w_xl, conv_w, conv_b, w_gate, b_gate, cneg)


def _merge_call(x, g_in, xc, w_gate, b_gate, cneg, hf, att, w_gl, w_ga, w_m, b_m, w_b0, w_b1, w_o, g_fin, final_norm):
    b, s, d = x.shape
    d_lru = xc.shape[2]
    t = min(LRU_ROWS, s)
    _, _, _, hf_spec, scratch, blk = _lru_specs(s, t, d, d_lru, True)
    tok = lambda width: pl.BlockSpec((1, t, width), lambda bi, i: (bi, blk(i), 0))
    consts = [w_gate, b_gate, cneg]
    tail = [w_gl, w_ga, w_m, b_m, w_b0, w_b1, w_o, g_fin]
    return pl.pallas_call(
        functools.partial(_merge_kernel, t=t, final_norm=final_norm),
        grid=(b, s // t),
        in_specs=[tok(d), _const_spec(g_in.shape), tok(d_lru)] + [_const_spec(c.shape) for c in consts]
        + [hf_spec, tok(att.shape[2])] + [_const_spec(c.shape) for c in tail],
        out_specs=tok(d),
        out_shape=jax.ShapeDtypeStruct((b, s, d), F32),
        scratch_shapes=scratch + [pltpu.VMEM((t // SUBLANES, SUBLANES, d_lru), F32)],
        compiler_params=_params(2, True),
        name="merge",
    )(x, g_in, xc, *consts, hf, att, *tail)


def _rope_tables(s, gain, scale):
    rows_n = s // GRID_W
    rows = jnp.repeat(jnp.arange(rows_n, dtype=F32), GRID_W)
    cols = jnp.tile(jnp.arange(GRID_W, dtype=F32), rows_n)
    n_pair_axis = HEAD_DIM // 4
    inv_freq = ROPE_THETA ** (-jnp.arange(n_pair_axis, dtype=F32) / n_pair_axis)
    ang = jnp.concatenate([rows[:, None] * inv_freq, cols[:, None] * inv_freq], axis=-1)
    cos = jnp.cos(ang).T
    sin = jnp.sin(ang).T
    c = jnp.concatenate([cos, cos], axis=0)
    sn = jnp.concatenate([-sin, sin], axis=0)
    gain = gain.astype(F32)
    return (gain[:, None] * c) * scale, (jnp.roll(gain, HALF)[:, None] * sn) * scale


def _layer_params(w_in, q_norm, k_norm, w_rgate, b_rgate, w_igate, b_igate, lam, w_branch, w_out, d):
    d_attn = N_HEADS * HEAD_DIM
    d_kv = N_KV_HEADS * HEAD_DIM
    d_lru = N_LRU_BLOCKS * LRU_BLOCK
    splits = [d_attn, d_kv, d_kv, d_attn, d_lru, d_lru]
    offs = [0]
    for w in splits:
        offs.append(offs[-1] + w)
    w_q, w_k, w_v, w_ga, w_xl, w_gl = (w_in[:, offs[j]:offs[j + 1]] for j in range(6))
    w_m = w_in[:, offs[6]:]
    perm = jnp.concatenate([jnp.arange(0, HEAD_DIM, 2), jnp.arange(1, HEAD_DIM, 2)])
    permute = lambda w, nh: w.reshape(d, nh, HEAD_DIM)[:, :, perm].reshape(d, nh * HEAD_DIM)
    w_qkvT = jnp.concatenate([permute(w_q, N_HEADS), permute(w_k, N_KV_HEADS), w_v], axis=1).T.astype(BF16)
    w_gate = jnp.concatenate([w_rgate, w_igate], axis=-1).astype(BF16)
    b_gate = jnp.concatenate([b_rgate.reshape(2, N_LRU_BLOCKS, 1, LRU_BLOCK),
                              b_igate.reshape(2, N_LRU_BLOCKS, 1, LRU_BLOCK)], axis=-1).astype(F32)
    cneg = (-RG_C * math.log2(math.e) * jax.nn.softplus(-lam.astype(F32))).reshape(2, 1, d_lru)
    return dict(
        w_qkvT=w_qkvT, gq=q_norm[perm], gk=k_norm[perm],
        w_ga=w_ga.astype(BF16), w_xl=w_xl.astype(BF16), w_gl=w_gl.astype(BF16), w_m=w_m.astype(BF16),
        w_gate=w_gate, b_gate=b_gate, cneg=cneg,
        w_b0=w_branch[0].astype(BF16), w_b1=w_branch[1].astype(BF16), w_o=w_out.astype(BF16),
    )


def _layer(x, p, g_in, conv_w, conv_b, b_m, g_fin, final_norm):
    s = x.shape[1]
    q_scale = math.log2(math.e) / math.sqrt(HEAD_DIM)
    cq, sq = _rope_tables(s, p["gq"], q_scale)
    ck, sk = _rope_tables(s, p["gk"], 1.0)
    qT, k, vT = _qkv_call(x, g_in, p["w_qkvT"], cq, sq, ck, sk)
    att = _attn_call(qT, k, vT)
    lru = (g_in, p["w_xl"], conv_w, conv_b)
    hf, xc = _lru_f_call(x, *lru, p["w_gate"][0], p["b_gate"][0], p["cneg"][0])
    return _merge_call(x, g_in, xc, p["w_gate"][1], p["b_gate"][1], p["cneg"][1], hf, att,
                       p["w_gl"], p["w_ga"], p["w_m"], b_m, p["w_b0"], p["w_b1"], p["w_o"], g_fin, final_norm)


def kernel(x_prompt, x_sample, norm_in, w_in, b_merge, q_norm, k_norm, conv_w, conv_b, w_rgate, b_rgate,
           w_igate, b_igate, lam, w_branch, w_out, norm_final):
    depth, d = norm_in.shape
    layers = [
        _layer_params(w_in[l], q_norm[l], k_norm[l], w_rgate[l], b_rgate[l], w_igate[l], b_igate[l], lam[l],
                      w_branch[l], w_out[l], d)
        for l in range(depth)
    ]
    g_fin = norm_final.reshape(1, d).astype(F32)
    outs = []
    for x in (x_prompt, x_sample):
        for l in range(depth):
            x = _layer(x, layers[l], norm_in[l].reshape(1, d).astype(F32), conv_w[l].astype(F32),
                       conv_b[l].reshape(1, -1).astype(F32), b_merge[l].reshape(1, -1).astype(F32), g_fin,
                       l == depth - 1)
        outs.append(x)
    return tuple(outs)
```

```python
import functools
import math

import jax
import jax.numpy as jnp
from jax import lax
from jax.experimental import pallas as pl
from jax.experimental.pallas import tpu as pltpu

N_HEADS = 8
N_KV_HEADS = 2
GROUP = N_HEADS // N_KV_HEADS
HEAD_DIM = 128
HALF = HEAD_DIM // 2
N_LRU_BLOCKS = 8
LRU_BLOCK = 128
CONV_W = 4
RG_C = 8.0
EPS = 1e-6
TINY = 1e-30
GRID_W = 64
ROPE_THETA = 10000.0

SUBLANES = 8
ONES_ROWS = 2 * SUBLANES
HALO = SUBLANES
V7X_VMEM_LIMIT = 56 * 1024 * 1024

QKV_ROWS = 1024
ATTN_Q = 512
ATTN_K = 256
AHEAD = 2
LRU_ROWS = 256

F32 = jnp.float32
BF16 = jnp.bfloat16


def _const_spec(shape):
    nd = len(shape)
    return pl.BlockSpec(shape, lambda *_: (0,) * nd, pipeline_mode=pl.Buffered(1))


def _params(n_axes, sequential_last, all_sequential=False):
    sem = ["arbitrary" if all_sequential else "parallel"] * n_axes
    if sequential_last:
        sem[-1] = "arbitrary"
    return pltpu.CompilerParams(dimension_semantics=tuple(sem), vmem_limit_bytes=V7X_VMEM_LIMIT)


def _rms_scale(x):
    return lax.rsqrt(jnp.mean(x * x, axis=-1, keepdims=True) + EPS)


def _qkv_kernel(x_ref, g_ref, w_ref, cq_ref, sq_ref, ck_ref, sk_ref, qT_ref, k_ref, vT_ref):
    x = x_ref[0]
    h = (x * _rms_scale(x) * g_ref[...]).astype(BF16)
    zT = lax.dot_general(w_ref[...], h, (((1,), (1,)), ((), ())), preferred_element_type=F32)

    def norm_rope(z, c, s):
        n = z * lax.rsqrt(jnp.mean(z * z, axis=0, keepdims=True) + EPS)
        swapped = jnp.concatenate([n[HALF:], n[:HALF]], axis=0)
        return n * c + swapped * s

    cq, sq = cq_ref[...], sq_ref[...]
    for hd in range(N_HEADS):
        rows = slice(hd * HEAD_DIM, (hd + 1) * HEAD_DIM)
        qh = norm_rope(zT[rows], cq, sq).astype(BF16)
        tq = qT_ref.shape[3]
        for qb in range(qT_ref.shape[1]):
            qT_ref[0, qb, rows, :] = qh[:, qb * tq:(qb + 1) * tq]
    ck, sk = ck_ref[...], sk_ref[...]
    k_off = N_HEADS * HEAD_DIM
    for kv in range(N_KV_HEADS):
        rows = slice(k_off + kv * HEAD_DIM, k_off + (kv + 1) * HEAD_DIM)
        kT = norm_rope(zT[rows], ck, sk)
        k_ref[0, :, kv * HEAD_DIM:(kv + 1) * HEAD_DIM] = kT.T.astype(BF16)
    v_off = k_off + N_KV_HEADS * HEAD_DIM
    vT_ref[0] = zT[v_off:v_off + N_KV_HEADS * HEAD_DIM].astype(BF16)


def _qkv_call(x, g_in, w_qkvT, cq, sq, ck, sk):
    b, s, d = x.shape
    t = min(QKV_ROWS, s)
    tq = min(ATTN_Q, t)
    n_q, n_kv = N_HEADS * HEAD_DIM, N_KV_HEADS * HEAD_DIM
    tab = pl.BlockSpec((HEAD_DIM, t), lambda bi, i: (0, i))
    return pl.pallas_call(
        _qkv_kernel,
        grid=(b, s // t),
        in_specs=[
            pl.BlockSpec((1, t, d), lambda bi, i: (bi, i, 0)),
            _const_spec((1, d)),
            _const_spec(w_qkvT.shape),
            tab, tab, tab, tab,
        ],
        out_specs=[
            pl.BlockSpec((1, t // tq, n_q, tq), lambda bi, i: (bi, i, 0, 0)),
            pl.BlockSpec((1, t, n_kv), lambda bi, i: (bi, i, 0)),
            pl.BlockSpec((1, n_kv, t), lambda bi, i: (bi, 0, i)),
        ],
        out_shape=[
            jax.ShapeDtypeStruct((b, s // tq, n_q, tq), BF16),
            jax.ShapeDtypeStruct((b, s, n_kv), BF16),
            jax.ShapeDtypeStruct((b, n_kv, s), BF16),
        ],
        compiler_params=_params(2, False),
        name="qkv",
    )(x, g_in, w_qkvT, cq, sq, ck, sk)


def _attn_kernel(zero_ref, qT_ref, k_ref, vT_ref, o_ref, m_ref, acc_ref, cmax_ref, *s_slots, n_chunks, tk):
    @pl.when((pl.program_id(0) == 0) & (pl.program_id(1) == 0))
    def _():
        acc_ref[...] = jnp.zeros(acc_ref.shape, F32)

    m_ref[...] = jnp.full(m_ref.shape, -jnp.inf, F32)
    ones_rows = jnp.ones((ONES_ROWS, tk), BF16)
    zw, zr = zero_ref[0], zero_ref[1]

    def kv_cols(hd):
        kv = hd // GROUP
        return slice(kv * HEAD_DIM, (kv + 1) * HEAD_DIM)

    def scores(c, hd):
        off = pl.multiple_of(c * tk, tk)
        kc = k_ref[0, pl.ds(off, tk), kv_cols(hd)]
        qT = qT_ref[0, 0, hd * HEAD_DIM:(hd + 1) * HEAD_DIM, :]
        s = jnp.dot(kc, qT, preferred_element_type=F32)
        s_slots[hd][zw] = s
        cmax_ref[hd] = jnp.max(s, axis=0, keepdims=True)

    def accumulate(c, hd):
        off = pl.multiple_of(c * tk, tk)
        m_old = m_ref[hd:hd + 1, :]
        m_new = jnp.maximum(m_old, cmax_ref[hd])
        m_ref[hd:hd + 1, :] = m_new
        p = jnp.exp2(s_slots[hd][zr] - m_new).astype(BF16)
        vc = jnp.concatenate([vT_ref[0, kv_cols(hd), pl.ds(off, tk)], ones_rows], axis=0)
        acc_ref[hd] = jnp.exp2(m_old - m_new) * acc_ref[hd] + jnp.dot(vc, p, preferred_element_type=F32)

    for hd in range(AHEAD):
        scores(0, hd)

    def step(c, carry):
        for hd in range(N_HEADS):
            w = hd + AHEAD
            scores(jnp.minimum(c + w // N_HEADS, n_chunks - 1), w % N_HEADS)
            accumulate(c, hd)
        return carry

    lax.fori_loop(0, n_chunks, step, 0, unroll=32)

    for hd in range(N_HEADS):
        o = acc_ref[hd, :HEAD_DIM, :] * (1.0 / acc_ref[hd, HEAD_DIM:HEAD_DIM + 1, :])
        o_ref[0, :, hd * HEAD_DIM:(hd + 1) * HEAD_DIM] = o.T.astype(BF16)


def _attn_call(qT, k, vT):
    b, n_qb, n_q, tq = qT.shape
    s, n_kv = k.shape[1:]
    tk = min(ATTN_K, s)
    kernel = functools.partial(_attn_kernel, n_chunks=s // tk, tk=tk)
    kv_buffers = pl.Buffered(2 if b > 1 else 1)
    return pl.pallas_call(
        kernel,
        grid=(b, n_qb),
        in_specs=[
            pl.BlockSpec(memory_space=pltpu.SMEM),
            pl.BlockSpec((1, 1, n_q, tq), lambda bi, i: (bi, i, 0, 0)),
            pl.BlockSpec((1, s, n_kv), lambda bi, i: (bi, 0, 0), pipeline_mode=kv_buffers),
            pl.BlockSpec((1, n_kv, s), lambda bi, i: (bi, 0, 0), pipeline_mode=kv_buffers),
        ],
        out_specs=pl.BlockSpec((1, tq, n_q), lambda bi, i: (bi, i, 0)),
        out_shape=jax.ShapeDtypeStruct((b, s, n_q), BF16),
        scratch_shapes=[
            pltpu.VMEM((N_HEADS, tq), F32),
            pltpu.VMEM((N_HEADS, HEAD_DIM + ONES_ROWS, tq), F32),
            pltpu.VMEM((N_HEADS, 1, tq), F32),
        ] + [pltpu.VMEM((1, tk, tq), F32)] * N_HEADS,
        compiler_params=_params(2, True, all_sequential=True),
        name="attn",
    )(jnp.zeros((2,), jnp.int32), qT, k, vT)


def _reset_carry_at_sequence_start(carry_ref):
    @pl.when(pl.program_id(1) == 0)
    def _():
        carry_ref[...] = jnp.zeros(carry_ref.shape, F32)


def _lru_inputs(x_ref, xp_ref, xn_ref, g_ref, wxl_ref, cw_ref, cb_ref, *, t):
    blk = pl.program_id(1)
    nb = pl.num_programs(1)
    d_lru = wxl_ref.shape[1]
    groups = t // SUBLANES
    xp = xp_ref[0] * jnp.where(blk > 0, 1.0, 0.0)
    xn = xn_ref[0] * jnp.where(blk < nb - 1, 1.0, 0.0)
    xe = jnp.concatenate([xp, x_ref[0], xn], axis=0)
    he = (xe * _rms_scale(xe) * g_ref[...]).astype(BF16)
    xl = jnp.dot(he, wxl_ref[...], preferred_element_type=F32)
    x3 = xl.reshape(groups + 2, SUBLANES, d_lru)
    sub = lax.broadcasted_iota(jnp.int32, (1, SUBLANES, d_lru), 1)

    def delayed(k):
        r = pltpu.roll(x3, k, axis=1)
        return jnp.where(sub >= k, r[1:groups + 1], r[0:groups])

    ahead = pltpu.roll(x3, SUBLANES - 1, axis=1)
    ahead = jnp.where(sub < SUBLANES - 1, ahead[1:groups + 1], ahead[2:groups + 2])
    cw = cw_ref[...]
    xc = (cb_ref[...] + cw[0:1] * delayed(2) + cw[1:2] * delayed(1) + cw[2:3] * x3[1:groups + 1]
          + cw[3:4] * ahead)
    return xc.reshape(t, d_lru)


def _lru_gates(xc, wg_ref, bg_ref):
    return [
        jnp.dot(xc[:, n * LRU_BLOCK:(n + 1) * LRU_BLOCK].astype(BF16), wg_ref[n], preferred_element_type=F32)
        + bg_ref[n]
        for n in range(N_LRU_BLOCKS)
    ]


def _lru_scan(xc, gate_pre, cneg_ref, a_scr, u_scr, carry_ref, write_rows, *, reverse, t):
    groups = t // SUBLANES
    sub = lax.broadcasted_iota(jnp.int32, (1, SUBLANES, LRU_BLOCK), 1)
    for n in range(N_LRU_BLOCKS):
        cols = slice(n * LRU_BLOCK, (n + 1) * LRU_BLOCK)
        xcn = xc[:, cols]
        gz = gate_pre[n]
        r = jax.nn.sigmoid(gz[:, :LRU_BLOCK])
        ig = jax.nn.sigmoid(gz[:, LRU_BLOCK:])
        a = jnp.exp2(cneg_ref[:, cols] * r)
        y = (1.0 - a) * (1.0 + a)
        u = (y * lax.rsqrt(jnp.maximum(y, TINY))) * (ig * xcn)
        a3 = a.reshape(groups, SUBLANES, LRU_BLOCK)
        u3 = u.reshape(groups, SUBLANES, LRU_BLOCK)
        for dist in (1, 2, 4):
            if reverse:
                shift, ok = SUBLANES - dist, sub < SUBLANES - dist
            else:
                shift, ok = dist, sub >= dist
            a_nb = pltpu.roll(a3, shift, axis=1)
            u_nb = pltpu.roll(u3, shift, axis=1)
            u3 = jnp.where(ok, a3 * u_nb + u3, u3)
            a3 = jnp.where(ok, a3 * a_nb, a3)
        a_scr[:, :, cols] = a3
        u_scr[:, :, cols] = u3

    edge = 0 if reverse else SUBLANES - 1
    for n in range(N_LRU_BLOCKS):
        cols = slice(n * LRU_BLOCK, (n + 1) * LRU_BLOCK)
        hrow = carry_ref[:, cols]
        for j in range(groups):
            jj = groups - 1 - j if reverse else j
            rows = u_scr[jj, :, cols] + a_scr[jj, :, cols] * hrow
            write_rows(jj, cols, rows)
            hrow = jnp.broadcast_to(rows[edge:edge + 1, :], (SUBLANES, LRU_BLOCK))
        carry_ref[:, cols] = hrow


def _lru_f_kernel(x_ref, xp_ref, xn_ref, g_ref, wxl_ref, cw_ref, cb_ref, wg_ref, bg_ref, cneg_ref,
                  hf_ref, xc_ref, a_scr, u_scr, carry_ref, *, t):
    def write_rows(j, cols, rows):
        hf_ref[0, j, :, cols] = rows

    _reset_carry_at_sequence_start(carry_ref)
    xc = _lru_inputs(x_ref, xp_ref, xn_ref, g_ref, wxl_ref, cw_ref, cb_ref, t=t)
    xc_ref[0] = xc
    gate_pre = _lru_gates(xc, wg_ref, bg_ref)
    _lru_scan(xc, gate_pre, cneg_ref, a_scr, u_scr, carry_ref, write_rows, reverse=False, t=t)


def _merge_kernel(x_ref, g_ref, xc_ref, wg_ref, bg_ref, cneg_ref,
                  hf_ref, att_ref, wgl_ref, wga_ref, wm_ref, bm_ref, wb0_ref, wb1_ref, wo_ref, gfin_ref,
                  y_ref, a_scr, u_scr, carry_ref, hb_scr, *, t, final_norm):
    def write_rows(j, cols, rows):
        hb_scr[j, :, cols] = rows

    _reset_carry_at_sequence_start(carry_ref)
    d = x_ref.shape[2]
    x = x_ref[0]
    h = (x * _rms_scale(x) * g_ref[...]).astype(BF16)
    xc = xc_ref[0]
    gate_pre = _lru_gates(xc, wg_ref, bg_ref)
    _lru_scan(xc, gate_pre, cneg_ref, a_scr, u_scr, carry_ref, write_rows, reverse=True, t=t)
    g_l = jnp.dot(h, wgl_ref[...], preferred_element_type=F32)
    lsum = (hf_ref[0] + hb_scr[...]).reshape(t, xc.shape[1])
    l_out = (lsum * (g_l * jax.nn.sigmoid(g_l))).astype(BF16)
    g_a = jnp.dot(h, wga_ref[...], preferred_element_type=F32)
    a_out = (att_ref[0].astype(F32) * (g_a * jax.nn.sigmoid(g_a))).astype(BF16)
    a_proj = jnp.dot(a_out, wb0_ref[...], preferred_element_type=F32)
    l_proj = jnp.dot(l_out, wb1_ref[...], preferred_element_type=F32)
    gates = jax.nn.sigmoid(jnp.dot(h, wm_ref[...], preferred_element_type=F32) + bm_ref[...])
    merged = gates[:, :d] * a_proj + gates[:, d:] * l_proj
    y = x + jnp.dot(merged.astype(BF16), wo_ref[...], preferred_element_type=F32)
    if final_norm:
        y = y * _rms_scale(y) * gfin_ref[...]
    y_ref[0] = y


def _lru_specs(s, t, d, d_lru, reverse):
    nb = s // t
    groups = t // SUBLANES
    last_group = s // SUBLANES - 1

    def blk(i):
        return nb - 1 - i if reverse else i

    x_spec = pl.BlockSpec((1, t, d), lambda bi, i: (bi, blk(i), 0))
    xp_spec = pl.BlockSpec((1, HALO, d), lambda bi, i: (bi, jnp.maximum(blk(i) * groups - 1, 0), 0))
    xn_spec = pl.BlockSpec((1, HALO, d), lambda bi, i: (bi, jnp.minimum((blk(i) + 1) * groups, last_group), 0))
    hf_spec = pl.BlockSpec((1, groups, SUBLANES, d_lru), lambda bi, i: (bi, blk(i), 0, 0))
    scratch = [
        pltpu.VMEM((groups, SUBLANES, d_lru), F32),
        pltpu.VMEM((groups, SUBLANES, d_lru), F32),
        pltpu.VMEM((SUBLANES, d_lru), F32),
    ]
    return x_spec, xp_spec, xn_spec, hf_spec, scratch, blk


def _lru_f_call(x, g_in, w_xl, conv_w, conv_b, w_gate, b_gate, cneg):
    b, s, d = x.shape
    d_lru = w_xl.shape[1]
    t = min(LRU_ROWS, s)
    x_spec, xp_spec, xn_spec, hf_spec, scratch, _ = _lru_specs(s, t, d, d_lru, False)
    return pl.pallas_call(
        functools.partial(_lru_f_kernel, t=t),
        grid=(b, s // t),
        in_specs=[x_spec, xp_spec, xn_spec, _const_spec((1, d)), _const_spec(w_xl.shape),
                  _const_spec(conv_w.shape), _const_spec(conv_b.shape), _const_spec(w_gate.shape),
                  _const_spec(b_gate.shape), _const_spec(cneg.shape)],
        out_specs=[hf_spec, pl.BlockSpec((1, t, d_lru), lambda bi, i: (bi, i, 0))],
        out_shape=[jax.ShapeDtypeStruct((b, s // SUBLANES, SUBLANES, d_lru), F32),
                   jax.ShapeDtypeStruct((b, s, d_lru), F32)],
        scratch_shapes=scratch,
        compiler_params=_params(2, True),
        name="lru_f",
    )(x, x, x, g_in, w_xl, conv_w, conv_b, w_gate, b_gate, cneg)


def _merge_call(x, g_in, xc, w_gate, b_gate, cneg, hf, att, w_gl, w_ga, w_m, b_m, w_b0, w_b1, w_o, g_fin, final_norm):
    b, s, d = x.shape
    d_lru = xc.shape[2]
    t = min(LRU_ROWS, s)
    _, _, _, hf_spec, scratch, blk = _lru_specs(s, t, d, d_lru, True)
    tok = lambda width: pl.BlockSpec((1, t, width), lambda bi, i: (bi, blk(i), 0))
    consts = [w_gate, b_gate, cneg]
    tail = [w_gl, w_ga, w_m, b_m, w_b0, w_b1, w_o, g_fin]
    return pl.pallas_call(
        functools.partial(_merge_kernel, t=t, final_norm=final_norm),
        grid=(b, s // t),
        in_specs=[tok(d), _const_spec(g_in.shape), tok(d_lru)] + [_const_spec(c.shape) for c in consts]
        + [hf_spec, tok(att.shape[2])] + [_const_spec(c.shape) for c in tail],
        out_specs=tok(d),
        out_shape=jax.ShapeDtypeStruct((b, s, d), F32),
        scratch_shapes=scratch + [pltpu.VMEM((t // SUBLANES, SUBLANES, d_lru), F32)],
        compiler_params=_params(2, True),
        name="merge",
    )(x, g_in, xc, *consts, hf, att, *tail)


def _rope_tables(s, gain, scale):
    rows_n = s // GRID_W
    rows = jnp.repeat(jnp.arange(rows_n, dtype=F32), GRID_W)
    cols = jnp.tile(jnp.arange(GRID_W, dtype=F32), rows_n)
    n_pair_axis = HEAD_DIM // 4
    inv_freq = ROPE_THETA ** (-jnp.arange(n_pair_axis, dtype=F32) / n_pair_axis)
    ang = jnp.concatenate([rows[:, None] * inv_freq, cols[:, None] * inv_freq], axis=-1)
    cos = jnp.cos(ang).T
    sin = jnp.sin(ang).T
    c = jnp.concatenate([cos, cos], axis=0)
    sn = jnp.concatenate([-sin, sin], axis=0)
    gain = gain.astype(F32)
    return (gain[:, None] * c) * scale, (jnp.roll(gain, HALF)[:, None] * sn) * scale


def _layer_params(w_in, q_norm, k_norm, w_rgate, b_rgate, w_igate, b_igate, lam, w_branch, w_out, d):
    d_attn = N_HEADS * HEAD_DIM
    d_kv = N_KV_HEADS * HEAD_DIM
    d_lru = N_LRU_BLOCKS * LRU_BLOCK
    splits = [d_attn, d_kv, d_kv, d_attn, d_lru, d_lru]
    offs = [0]
    for w in splits:
        offs.append(offs[-1] + w)
    w_q, w_k, w_v, w_ga, w_xl, w_gl = (w_in[:, offs[j]:offs[j + 1]] for j in range(6))
    w_m = w_in[:, offs[6]:]
    perm = jnp.concatenate([jnp.arange(0, HEAD_DIM, 2), jnp.arange(1, HEAD_DIM, 2)])
    permute = lambda w, nh: w.reshape(d, nh, HEAD_DIM)[:, :, perm].reshape(d, nh * HEAD_DIM)
    w_qkvT = jnp.concatenate([permute(w_q, N_HEADS), permute(w_k, N_KV_HEADS), w_v], axis=1).T.astype(BF16)
    w_gate = jnp.concatenate([w_rgate, w_igate], axis=-1).astype(BF16)
    b_gate = jnp.concatenate([b_rgate.reshape(2, N_LRU_BLOCKS, 1, LRU_BLOCK),
                              b_igate.reshape(2, N_LRU_BLOCKS, 1, LRU_BLOCK)], axis=-1).astype(F32)
    cneg = (-RG_C * math.log2(math.e) * jax.nn.softplus(-lam.astype(F32))).reshape(2, 1, d_lru)
    return dict(
        w_qkvT=w_qkvT, gq=q_norm[perm], gk=k_norm[perm],
        w_ga=w_ga.astype(BF16), w_xl=w_xl.astype(BF16), w_gl=w_gl.astype(BF16), w_m=w_m.astype(BF16),
        w_gate=w_gate, b_gate=b_gate, cneg=cneg,
        w_b0=w_branch[0].astype(BF16), w_b1=w_branch[1].astype(BF16), w_o=w_out.astype(BF16),
    )


def _layer(x, p, g_in, conv_w, conv_b, b_m, g_fin, final_norm):
    s = x.shape[1]
    q_scale = math.log2(math.e) / math.sqrt(HEAD_DIM)
    cq, sq = _rope_tables(s, p["gq"], q_scale)
    ck, sk = _rope_tables(s, p["gk"], 1.0)
    qT, k, vT = _qkv_call(x, g_in, p["w_qkvT"], cq, sq, ck, sk)
    att = _attn_call(qT, k, vT)
    lru = (g_in, p["w_xl"], conv_w, conv_b)
    hf, xc = _lru_f_call(x, *lru, p["w_gate"][0], p["b_gate"][0], p["cneg"][0])
    return _merge_call(x, g_in, xc, p["w_gate"][1], p["b_gate"][1], p["cneg"][1], hf, att,
                       p["w_gl"], p["w_ga"], p["w_m"], b_m, p["w_b0"], p["w_b1"], p["w_o"], g_fin, final_norm)


def kernel(x_prompt, x_sample, norm_in, w_in, b_merge, q_norm, k_norm, conv_w, conv_b, w_rgate, b_rgate,
           w_igate, b_igate, lam, w_branch, w_out, norm_final):
    depth, d = norm_in.shape
    layers = [
        _layer_params(w_in[l], q_norm[l], k_norm[l], w_rgate[l], b_rgate[l], w_igate[l], b_igate[l], lam[l],
                      w_branch[l], w_out[l], d)
        for l in range(depth)
    ]
    g_fin = norm_final.reshape(1, d).astype(F32)
    outs = []
    for x in (x_prompt, x_sample):
        for l in range(depth):
            x = _layer(x, layers[l], norm_in[l].reshape(1, d).astype(F32), conv_w[l].astype(F32),
                       conv_b[l].reshape(1, -1).astype(F32), b_merge[l].reshape(1, -1).astype(F32), g_fin,
                       l == depth - 1)
        outs.append(x)
    return tuple(outs)
```

```python
import functools
import math

import jax
import jax.numpy as jnp
from jax import lax
from jax.experimental import pallas as pl
from jax.experimental.pallas import tpu as pltpu

N_HEADS = 8
N_KV_HEADS = 2
GROUP = N_HEADS // N_KV_HEADS
HEAD_DIM = 128
HALF = HEAD_DIM // 2
N_LRU_BLOCKS = 8
LRU_BLOCK = 128
CONV_W = 4
RG_C = 8.0
EPS = 1e-6
TINY = 1e-30
GRID_W = 64
ROPE_THETA = 10000.0

SUBLANES = 8
ONES_ROWS = 2 * SUBLANES
HALO = SUBLANES
V7X_VMEM_LIMIT = 56 * 1024 * 1024

QKV_ROWS = 2048
ATTN_Q = 512
ATTN_K = 256
AHEAD = 2
LRU_ROWS = 256

F32 = jnp.float32
BF16 = jnp.bfloat16


def _const_spec(shape):
    nd = len(shape)
    return pl.BlockSpec(shape, lambda *_: (0,) * nd, pipeline_mode=pl.Buffered(1))


def _params(n_axes, sequential_last, all_sequential=False):
    sem = ["arbitrary" if all_sequential else "parallel"] * n_axes
    if sequential_last:
        sem[-1] = "arbitrary"
    return pltpu.CompilerParams(dimension_semantics=tuple(sem), vmem_limit_bytes=V7X_VMEM_LIMIT)


def _rms_scale(x):
    return lax.rsqrt(jnp.mean(x * x, axis=-1, keepdims=True) + EPS)


def _qkv_kernel(x_ref, g_ref, w_ref, cq_ref, sq_ref, ck_ref, sk_ref, qT_ref, k_ref, vT_ref):
    x = x_ref[0]
    h = (x * _rms_scale(x) * g_ref[...]).astype(BF16)
    zT = lax.dot_general(w_ref[...], h, (((1,), (1,)), ((), ())), preferred_element_type=F32)

    def norm_rope(z, c, s):
        n = z * lax.rsqrt(jnp.mean(z * z, axis=0, keepdims=True) + EPS)
        swapped = jnp.concatenate([n[HALF:], n[:HALF]], axis=0)
        return n * c + swapped * s

    cq, sq = cq_ref[...], sq_ref[...]
    for hd in range(N_HEADS):
        rows = slice(hd * HEAD_DIM, (hd + 1) * HEAD_DIM)
        qh = norm_rope(zT[rows], cq, sq).astype(BF16)
        tq = qT_ref.shape[3]
        for qb in range(qT_ref.shape[1]):
            qT_ref[0, qb, rows, :] = qh[:, qb * tq:(qb + 1) * tq]
    ck, sk = ck_ref[...], sk_ref[...]
    k_off = N_HEADS * HEAD_DIM
    for kv in range(N_KV_HEADS):
        rows = slice(k_off + kv * HEAD_DIM, k_off + (kv + 1) * HEAD_DIM)
        kT = norm_rope(zT[rows], ck, sk)
        k_ref[0, :, kv * HEAD_DIM:(kv + 1) * HEAD_DIM] = kT.T.astype(BF16)
    v_off = k_off + N_KV_HEADS * HEAD_DIM
    vT_ref[0] = zT[v_off:v_off + N_KV_HEADS * HEAD_DIM].astype(BF16)


def _qkv_call(x, g_in, w_qkvT, cq, sq, ck, sk):
    b, s, d = x.shape
    t = min(QKV_ROWS, s)
    tq = min(ATTN_Q, t)
    n_q, n_kv = N_HEADS * HEAD_DIM, N_KV_HEADS * HEAD_DIM
    tab = pl.BlockSpec((HEAD_DIM, t), lambda bi, i: (0, i))
    return pl.pallas_call(
        _qkv_kernel,
        grid=(b, s // t),
        in_specs=[
            pl.BlockSpec((1, t, d), lambda bi, i: (bi, i, 0)),
            _const_spec((1, d)),
            _const_spec(w_qkvT.shape),
            tab, tab, tab, tab,
        ],
        out_specs=[
            pl.BlockSpec((1, t // tq, n_q, tq), lambda bi, i: (bi, i, 0, 0)),
            pl.BlockSpec((1, t, n_kv), lambda bi, i: (bi, i, 0)),
            pl.BlockSpec((1, n_kv, t), lambda bi, i: (bi, 0, i)),
        ],
        out_shape=[
            jax.ShapeDtypeStruct((b, s // tq, n_q, tq), BF16),
            jax.ShapeDtypeStruct((b, s, n_kv), BF16),
            jax.ShapeDtypeStruct((b, n_kv, s), BF16),
        ],
        compiler_params=_params(2, False),
        name="qkv",
    )(x, g_in, w_qkvT, cq, sq, ck, sk)


def _attn_kernel(zero_ref, qT_ref, k_ref, vT_ref, o_ref, m_ref, acc_ref, cmax_ref, *s_slots, n_chunks, tk):
    @pl.when((pl.program_id(0) == 0) & (pl.program_id(1) == 0))
    def _():
        acc_ref[...] = jnp.zeros(acc_ref.shape, F32)

    m_ref[...] = jnp.full(m_ref.shape, -jnp.inf, F32)
    ones_rows = jnp.ones((ONES_ROWS, tk), BF16)
    zw, zr = zero_ref[0], zero_ref[1]

    def kv_cols(hd):
        kv = hd // GROUP
        return slice(kv * HEAD_DIM, (kv + 1) * HEAD_DIM)

    def scores(c, hd):
        off = pl.multiple_of(c * tk, tk)
        kc = k_ref[0, pl.ds(off, tk), kv_cols(hd)]
        qT = qT_ref[0, 0, hd * HEAD_DIM:(hd + 1) * HEAD_DIM, :]
        s = jnp.dot(kc, qT, preferred_element_type=F32)
        s_slots[hd][zw] = s
        cmax_ref[hd] = jnp.max(s, axis=0, keepdims=True)

    def accumulate(c, hd):
        off = pl.multiple_of(c * tk, tk)
        m_old = m_ref[hd:hd + 1, :]
        m_new = jnp.maximum(m_old, cmax_ref[hd])
        m_ref[hd:hd + 1, :] = m_new
        p = jnp.exp2(s_slots[hd][zr] - m_new).astype(BF16)
        vc = jnp.concatenate([vT_ref[0, kv_cols(hd), pl.ds(off, tk)], ones_rows], axis=0)
        acc_ref[hd] = jnp.exp2(m_old - m_new) * acc_ref[hd] + jnp.dot(vc, p, preferred_element_type=F32)

    for hd in range(AHEAD):
        scores(0, hd)

    def step(c, carry):
        for hd in range(N_HEADS):
            w = hd + AHEAD
            scores(jnp.minimum(c + w // N_HEADS, n_chunks - 1), w % N_HEADS)
            accumulate(c, hd)
        return carry

    lax.fori_loop(0, n_chunks, step, 0, unroll=16)

    for hd in range(N_HEADS):
        o = acc_ref[hd, :HEAD_DIM, :] * (1.0 / acc_ref[hd, HEAD_DIM:HEAD_DIM + 1, :])
        o_ref[0, :, hd * HEAD_DIM:(hd + 1) * HEAD_DIM] = o.T.astype(BF16)


def _attn_call(qT, k, vT):
    b, n_qb, n_q, tq = qT.shape
    s, n_kv = k.shape[1:]
    tk = min(ATTN_K, s)
    kernel = functools.partial(_attn_kernel, n_chunks=s // tk, tk=tk)
    kv_buffers = pl.Buffered(2 if b > 1 else 1)
    return pl.pallas_call(
        kernel,
        grid=(b, n_qb),
        in_specs=[
            pl.BlockSpec(memory_space=pltpu.SMEM),
            pl.BlockSpec((1, 1, n_q, tq), lambda bi, i: (bi, i, 0, 0)),
            pl.BlockSpec((1, s, n_kv), lambda bi, i: (bi, 0, 0), pipeline_mode=kv_buffers),
            pl.BlockSpec((1, n_kv, s), lambda bi, i: (bi, 0, 0), pipeline_mode=kv_buffers),
        ],
        out_specs=pl.BlockSpec((1, tq, n_q), lambda bi, i: (bi, i, 0)),
        out_shape=jax.ShapeDtypeStruct((b, s, n_q), BF16),
        scratch_shapes=[
            pltpu.VMEM((N_HEADS, tq), F32),
            pltpu.VMEM((N_HEADS, HEAD_DIM + ONES_ROWS, tq), F32),
            pltpu.VMEM((N_HEADS, 1, tq), F32),
        ] + [pltpu.VMEM((1, tk, tq), F32)] * N_HEADS,
        compiler_params=_params(2, True, all_sequential=True),
        name="attn",
    )(jnp.zeros((2,), jnp.int32), qT, k, vT)


def _reset_carry_at_sequence_start(carry_ref):
    @pl.when(pl.program_id(1) == 0)
    def _():
        carry_ref[...] = jnp.zeros(carry_ref.shape, F32)


def _lru_inputs(x_ref, xp_ref, xn_ref, g_ref, wxl_ref, cw_ref, cb_ref, *, t):
    blk = pl.program_id(1)
    nb = pl.num_programs(1)
    d_lru = wxl_ref.shape[1]
    groups = t // SUBLANES
    xp = xp_ref[0] * jnp.where(blk > 0, 1.0, 0.0)
    xn = xn_ref[0] * jnp.where(blk < nb - 1, 1.0, 0.0)
    xe = jnp.concatenate([xp, x_ref[0], xn], axis=0)
    he = (xe * _rms_scale(xe) * g_ref[...]).astype(BF16)
    xl = jnp.dot(he, wxl_ref[...], preferred_element_type=F32)
    x3 = xl.reshape(groups + 2, SUBLANES, d_lru)
    sub = lax.broadcasted_iota(jnp.int32, (1, SUBLANES, d_lru), 1)

    def delayed(k):
        r = pltpu.roll(x3, k, axis=1)
        return jnp.where(sub >= k, r[1:groups + 1], r[0:groups])

    ahead = pltpu.roll(x3, SUBLANES - 1, axis=1)
    ahead = jnp.where(sub < SUBLANES - 1, ahead[1:groups + 1], ahead[2:groups + 2])
    cw = cw_ref[...]
    xc = (cb_ref[...] + cw[0:1] * delayed(2) + cw[1:2] * delayed(1) + cw[2:3] * x3[1:groups + 1]
          + cw[3:4] * ahead)
    return xc.reshape(t, d_lru)


def _lru_gates(xc, wg_ref, bg_ref):
    return [
        jnp.dot(xc[:, n * LRU_BLOCK:(n + 1) * LRU_BLOCK].astype(BF16), wg_ref[n], preferred_element_type=F32)
        + bg_ref[n]
        for n in range(N_LRU_BLOCKS)
    ]


def _lru_scan(xc, gate_pre, cneg_ref, a_scr, u_scr, carry_ref, write_rows, *, reverse, t):
    groups = t // SUBLANES
    sub = lax.broadcasted_iota(jnp.int32, (1, SUBLANES, LRU_BLOCK), 1)
    for n in range(N_LRU_BLOCKS):
        cols = slice(n * LRU_BLOCK, (n + 1) * LRU_BLOCK)
        xcn = xc[:, cols]
        gz = gate_pre[n]
        r = jax.nn.sigmoid(gz[:, :LRU_BLOCK])
        ig = jax.nn.sigmoid(gz[:, LRU_BLOCK:])
        a = jnp.exp2(cneg_ref[:, cols] * r)
        y = (1.0 - a) * (1.0 + a)
        u = (y * lax.rsqrt(jnp.maximum(y, TINY))) * (ig * xcn)
        a3 = a.reshape(groups, SUBLANES, LRU_BLOCK)
        u3 = u.reshape(groups, SUBLANES, LRU_BLOCK)
        for dist in (1, 2, 4):
            if reverse:
                shift, ok = SUBLANES - dist, sub < SUBLANES - dist
            else:
                shift, ok = dist, sub >= dist
            a_nb = pltpu.roll(a3, shift, axis=1)
            u_nb = pltpu.roll(u3, shift, axis=1)
            u3 = jnp.where(ok, a3 * u_nb + u3, u3)
            a3 = jnp.where(ok, a3 * a_nb, a3)
        a_scr[:, :, cols] = a3
        u_scr[:, :, cols] = u3

    edge = 0 if reverse else SUBLANES - 1
    for n in range(N_LRU_BLOCKS):
        cols = slice(n * LRU_BLOCK, (n + 1) * LRU_BLOCK)
        hrow = carry_ref[:, cols]
        for j in range(groups):
            jj = groups - 1 - j if reverse else j
            rows = u_scr[jj, :, cols] + a_scr[jj, :, cols] * hrow
            write_rows(jj, cols, rows)
            hrow = jnp.broadcast_to(rows[edge:edge + 1, :], (SUBLANES, LRU_BLOCK))
        carry_ref[:, cols] = hrow


def _lru_f_kernel(x_ref, xp_ref, xn_ref, g_ref, wxl_ref, cw_ref, cb_ref, wg_ref, bg_ref, cneg_ref,
                  hf_ref, xc_ref, a_scr, u_scr, carry_ref, *, t):
    def write_rows(j, cols, rows):
        hf_ref[0, j, :, cols] = rows

    _reset_carry_at_sequence_start(carry_ref)
    xc = _lru_inputs(x_ref, xp_ref, xn_ref, g_ref, wxl_ref, cw_ref, cb_ref, t=t)
    xc_ref[0] = xc
    gate_pre = _lru_gates(xc, wg_ref, bg_ref)
    _lru_scan(xc, gate_pre, cneg_ref, a_scr, u_scr, carry_ref, write_rows, reverse=False, t=t)


def _merge_kernel(x_ref, g_ref, xc_ref, wg_ref, bg_ref, cneg_ref,
                  hf_ref, att_ref, wgl_ref, wga_ref, wm_ref, bm_ref, wb0_ref, wb1_ref, wo_ref, gfin_ref,
                  y_ref, a_scr, u_scr, carry_ref, hb_scr, *, t, final_norm):
    def write_rows(j, cols, rows):
        hb_scr[j, :, cols] = rows

    _reset_carry_at_sequence_start(carry_ref)
    d = x_ref.shape[2]
    x = x_ref[0]
    h = (x * _rms_scale(x) * g_ref[...]).astype(BF16)
    xc = xc_ref[0]
    gate_pre = _lru_gates(xc, wg_ref, bg_ref)
    _lru_scan(xc, gate_pre, cneg_ref, a_scr, u_scr, carry_ref, write_rows, reverse=True, t=t)
    g_l = jnp.dot(h, wgl_ref[...], preferred_element_type=F32)
    lsum = (hf_ref[0] + hb_scr[...]).reshape(t, xc.shape[1])
    l_out = (lsum * (g_l * jax.nn.sigmoid(g_l))).astype(BF16)
    g_a = jnp.dot(h, wga_ref[...], preferred_element_type=F32)
    a_out = (att_ref[0].astype(F32) * (g_a * jax.nn.sigmoid(g_a))).astype(BF16)
    a_proj = jnp.dot(a_out, wb0_ref[...], preferred_element_type=F32)
    l_proj = jnp.dot(l_out, wb1_ref[...], preferred_element_type=F32)
    gates = jax.nn.sigmoid(jnp.dot(h, wm_ref[...], preferred_element_type=F32) + bm_ref[...])
    merged = gates[:, :d] * a_proj + gates[:, d:] * l_proj
    y = x + jnp.dot(merged.astype(BF16), wo_ref[...], preferred_element_type=F32)
    if final_norm:
        y = y * _rms_scale(y) * gfin_ref[...]
    y_ref[0] = y


def _lru_specs(s, t, d, d_lru, reverse):
    nb = s // t
    groups = t // SUBLANES
    last_group = s // SUBLANES - 1

    def blk(i):
        return nb - 1 - i if reverse else i

    x_spec = pl.BlockSpec((1, t, d), lambda bi, i: (bi, blk(i), 0))
    xp_spec = pl.BlockSpec((1, HALO, d), lambda bi, i: (bi, jnp.maximum(blk(i) * groups - 1, 0), 0))
    xn_spec = pl.BlockSpec((1, HALO, d), lambda bi, i: (bi, jnp.minimum((blk(i) + 1) * groups, last_group), 0))
    hf_spec = pl.BlockSpec((1, groups, SUBLANES, d_lru), lambda bi, i: (bi, blk(i), 0, 0))
    scratch = [
        pltpu.VMEM((groups, SUBLANES, d_lru), F32),
        pltpu.VMEM((groups, SUBLANES, d_lru), F32),
        pltpu.VMEM((SUBLANES, d_lru), F32),
    ]
    return x_spec, xp_spec, xn_spec, hf_spec, scratch, blk


def _lru_f_call(x, g_in, w_xl, conv_w, conv_b, w_gate, b_gate, cneg):
    b, s, d = x.shape
    d_lru = w_xl.shape[1]
    t = min(LRU_ROWS, s)
    x_spec, xp_spec, xn_spec, hf_spec, scratch, _ = _lru_specs(s, t, d, d_lru, False)
    return pl.pallas_call(
        functools.partial(_lru_f_kernel, t=t),
        grid=(b, s // t),
        in_specs=[x_spec, xp_spec, xn_spec, _const_spec((1, d)), _const_spec(w_xl.shape),
                  _const_spec(conv_w.shape), _const_spec(conv_b.shape), _const_spec(w_gate.shape),
                  _const_spec(b_gate.shape), _const_spec(cneg.shape)],
        out_specs=[hf_spec, pl.BlockSpec((1, t, d_lru), lambda bi, i: (bi, i, 0))],
        out_shape=[jax.ShapeDtypeStruct((b, s // SUBLANES, SUBLANES, d_lru), F32),
                   jax.ShapeDtypeStruct((b, s, d_lru), F32)],
        scratch_shapes=scratch,
        compiler_params=_params(2, True),
        name="lru_f",
    )(x, x, x, g_in, w_xl, conv_w, conv_b, w_gate, b_gate, cneg)


def _merge_call(x, g_in, xc, w_gate, b_gate, cneg, hf, att, w_gl, w_ga, w_m, b_m, w_b0, w_b1, w_o, g_fin, final_norm):
    b, s, d = x.shape
    d_lru = xc.shape[2]
    t = min(LRU_ROWS, s)
    _, _, _, hf_spec, scratch, blk = _lru_specs(s, t, d, d_lru, True)
    tok = lambda width: pl.BlockSpec((1, t, width), lambda bi, i: (bi, blk(i), 0))
    consts = [w_gate, b_gate, cneg]
    tail = [w_gl, w_ga, w_m, b_m, w_b0, w_b1, w_o, g_fin]
    return pl.pallas_call(
        functools.partial(_merge_kernel, t=t, final_norm=final_norm),
        grid=(b, s // t),
        in_specs=[tok(d), _const_spec(g_in.shape), tok(d_lru)] + [_const_spec(c.shape) for c in consts]
        + [hf_spec, tok(att.shape[2])] + [_const_spec(c.shape) for c in tail],
        out_specs=tok(d),
        out_shape=jax.ShapeDtypeStruct((b, s, d), F32),
        scratch_shapes=scratch + [pltpu.VMEM((t // SUBLANES, SUBLANES, d_lru), F32)],
        compiler_params=_params(2, True),
        name="merge",
    )(x, g_in, xc, *consts, hf, att, *tail)


def _rope_tables(s, gain, scale):
    rows_n = s // GRID_W
    rows = jnp.repeat(jnp.arange(rows_n, dtype=F32), GRID_W)
    cols = jnp.tile(jnp.arange(GRID_W, dtype=F32), rows_n)
    n_pair_axis = HEAD_DIM // 4
    inv_freq = ROPE_THETA ** (-jnp.arange(n_pair_axis, dtype=F32) / n_pair_axis)
    ang = jnp.concatenate([rows[:, None] * inv_freq, cols[:, None] * inv_freq], axis=-1)
    cos = jnp.cos(ang).T
    sin = jnp.sin(ang).T
    c = jnp.concatenate([cos, cos], axis=0)
    sn = jnp.concatenate([-sin, sin], axis=0)
    gain = gain.astype(F32)
    return (gain[:, None] * c) * scale, (jnp.roll(gain, HALF)[:, None] * sn) * scale


def _layer_params(w_in, q_norm, k_norm, w_rgate, b_rgate, w_igate, b_igate, lam, w_branch, w_out, d):
    d_attn = N_HEADS * HEAD_DIM
    d_kv = N_KV_HEADS * HEAD_DIM
    d_lru = N_LRU_BLOCKS * LRU_BLOCK
    splits = [d_attn, d_kv, d_kv, d_attn, d_lru, d_lru]
    offs = [0]
    for w in splits:
        offs.append(offs[-1] + w)
    w_q, w_k, w_v, w_ga, w_xl, w_gl = (w_in[:, offs[j]:offs[j + 1]] for j in range(6))
    w_m = w_in[:, offs[6]:]
    perm = jnp.concatenate([jnp.arange(0, HEAD_DIM, 2), jnp.arange(1, HEAD_DIM, 2)])
    permute = lambda w, nh: w.reshape(d, nh, HEAD_DIM)[:, :, perm].reshape(d, nh * HEAD_DIM)
    w_qkvT = jnp.concatenate([permute(w_q, N_HEADS), permute(w_k, N_KV_HEADS), w_v], axis=1).T.astype(BF16)
    w_gate = jnp.concatenate([w_rgate, w_igate], axis=-1).astype(BF16)
    b_gate = jnp.concatenate([b_rgate.reshape(2, N_LRU_BLOCKS, 1, LRU_BLOCK),
                              b_igate.reshape(2, N_LRU_BLOCKS, 1, LRU_BLOCK)], axis=-1).astype(F32)
    cneg = (-RG_C * math.log2(math.e) * jax.nn.softplus(-lam.astype(F32))).reshape(2, 1, d_lru)
    return dict(
        w_qkvT=w_qkvT, gq=q_norm[perm], gk=k_norm[perm],
        w_ga=w_ga.astype(BF16), w_xl=w_xl.astype(BF16), w_gl=w_gl.astype(BF16), w_m=w_m.astype(BF16),
        w_gate=w_gate, b_gate=b_gate, cneg=cneg,
        w_b0=w_branch[0].astype(BF16), w_b1=w_branch[1].astype(BF16), w_o=w_out.astype(BF16),
    )


def _layer(x, p, g_in, conv_w, conv_b, b_m, g_fin, final_norm):
    s = x.shape[1]
    q_scale = math.log2(math.e) / math.sqrt(HEAD_DIM)
    cq, sq = _rope_tables(s, p["gq"], q_scale)
    ck, sk = _rope_tables(s, p["gk"], 1.0)
    qT, k, vT = _qkv_call(x, g_in, p["w_qkvT"], cq, sq, ck, sk)
    att = _attn_call(qT, k, vT)
    lru = (g_in, p["w_xl"], conv_w, conv_b)
    hf, xc = _lru_f_call(x, *lru, p["w_gate"][0], p["b_gate"][0], p["cneg"][0])
    return _merge_call(x, g_in, xc, p["w_gate"][1], p["b_gate"][1], p["cneg"][1], hf, att,
                       p["w_gl"], p["w_ga"], p["w_m"], b_m, p["w_b0"], p["w_b1"], p["w_o"], g_fin, final_norm)


def kernel(x_prompt, x_sample, norm_in, w_in, b_merge, q_norm, k_norm, conv_w, conv_b, w_rgate, b_rgate,
           w_igate, b_igate, lam, w_branch, w_out, norm_final):
    depth, d = norm_in.shape
    layers = [
        _layer_params(w_in[l], q_norm[l], k_norm[l], w_rgate[l], b_rgate[l], w_igate[l], b_igate[l], lam[l],
                      w_branch[l], w_out[l], d)
        for l in range(depth)
    ]
    g_fin = norm_final.reshape(1, d).astype(F32)
    outs = []
    for x in (x_prompt, x_sample):
        for l in range(depth):
            x = _layer(x, layers[l], norm_in[l].reshape(1, d).astype(F32), conv_w[l].astype(F32),
                       conv_b[l].reshape(1, -1).astype(F32), b_merge[l].reshape(1, -1).astype(F32), g_fin,
                       l == depth - 1)
        outs.append(x)
    return tuple(outs)
```

```python
import functools
import math

import jax
import jax.numpy as jnp
from jax import lax
from jax.experimental import pallas as pl
from jax.experimental.pallas import tpu as pltpu

N_HEADS = 8
N_KV_HEADS = 2
GROUP = N_HEADS // N_KV_HEADS
HEAD_DIM = 128
HALF = HEAD_DIM // 2
N_LRU_BLOCKS = 8
LRU_BLOCK = 128
CONV_W = 4
RG_C = 8.0
EPS = 1e-6
TINY = 1e-30
GRID_W = 64
ROPE_THETA = 10000.0

SUBLANES = 8
ONES_ROWS = 2 * SUBLANES
HALO = SUBLANES
V7X_VMEM_LIMIT = 56 * 1024 * 1024

QKV_ROWS = 2048
ATTN_Q = 512
ATTN_K = 256
AHEAD = 2
LRU_ROWS = 512

F32 = jnp.float32
BF16 = jnp.bfloat16


def _const_spec(shape):
    nd = len(shape)
    return pl.BlockSpec(shape, lambda *_: (0,) * nd, pipeline_mode=pl.Buffered(1))


def _params(n_axes, sequential_last, all_sequential=False):
    sem = ["arbitrary" if all_sequential else "parallel"] * n_axes
    if sequential_last:
        sem[-1] = "arbitrary"
    return pltpu.CompilerParams(dimension_semantics=tuple(sem), vmem_limit_bytes=V7X_VMEM_LIMIT)


def _rms_scale(x):
    return lax.rsqrt(jnp.mean(x * x, axis=-1, keepdims=True) + EPS)


def _qkv_kernel(x_ref, g_ref, w_ref, cq_ref, sq_ref, ck_ref, sk_ref, qT_ref, k_ref, vT_ref):
    x = x_ref[0]
    h = (x * _rms_scale(x) * g_ref[...]).astype(BF16)
    zT = lax.dot_general(w_ref[...], h, (((1,), (1,)), ((), ())), preferred_element_type=F32)

    def norm_rope(z, c, s):
        n = z * lax.rsqrt(jnp.mean(z * z, axis=0, keepdims=True) + EPS)
        swapped = jnp.concatenate([n[HALF:], n[:HALF]], axis=0)
        return n * c + swapped * s

    cq, sq = cq_ref[...], sq_ref[...]
    for hd in range(N_HEADS):
        rows = slice(hd * HEAD_DIM, (hd + 1) * HEAD_DIM)
        qh = norm_rope(zT[rows], cq, sq).astype(BF16)
        tq = qT_ref.shape[3]
        for qb in range(qT_ref.shape[1]):
            qT_ref[0, qb, rows, :] = qh[:, qb * tq:(qb + 1) * tq]
    ck, sk = ck_ref[...], sk_ref[...]
    k_off = N_HEADS * HEAD_DIM
    for kv in range(N_KV_HEADS):
        rows = slice(k_off + kv * HEAD_DIM, k_off + (kv + 1) * HEAD_DIM)
        kT = norm_rope(zT[rows], ck, sk)
        k_ref[0, :, kv * HEAD_DIM:(kv + 1) * HEAD_DIM] = kT.T.astype(BF16)
    v_off = k_off + N_KV_HEADS * HEAD_DIM
    vT_ref[0] = zT[v_off:v_off + N_KV_HEADS * HEAD_DIM].astype(BF16)


def _qkv_call(x, g_in, w_qkvT, cq, sq, ck, sk):
    b, s, d = x.shape
    t = min(QKV_ROWS, s)
    tq = min(ATTN_Q, t)
    n_q, n_kv = N_HEADS * HEAD_DIM, N_KV_HEADS * HEAD_DIM
    tab = pl.BlockSpec((HEAD_DIM, t), lambda bi, i: (0, i))
    return pl.pallas_call(
        _qkv_kernel,
        grid=(b, s // t),
        in_specs=[
            pl.BlockSpec((1, t, d), lambda bi, i: (bi, i, 0)),
            _const_spec((1, d)),
            _const_spec(w_qkvT.shape),
            tab, tab, tab, tab,
        ],
        out_specs=[
            pl.BlockSpec((1, t // tq, n_q, tq), lambda bi, i: (bi, i, 0, 0)),
            pl.BlockSpec((1, t, n_kv), lambda bi, i: (bi, i, 0)),
            pl.BlockSpec((1, n_kv, t), lambda bi, i: (bi, 0, i)),
        ],
        out_shape=[
            jax.ShapeDtypeStruct((b, s // tq, n_q, tq), BF16),
            jax.ShapeDtypeStruct((b, s, n_kv), BF16),
            jax.ShapeDtypeStruct((b, n_kv, s), BF16),
        ],
        compiler_params=_params(2, False),
        name="qkv",
    )(x, g_in, w_qkvT, cq, sq, ck, sk)


def _attn_kernel(zero_ref, qT_ref, k_ref, vT_ref, o_ref, m_ref, acc_ref, cmax_ref, *s_slots, n_chunks, tk):
    @pl.when((pl.program_id(0) == 0) & (pl.program_id(1) == 0))
    def _():
        acc_ref[...] = jnp.zeros(acc_ref.shape, F32)

    m_ref[...] = jnp.full(m_ref.shape, -jnp.inf, F32)
    ones_rows = jnp.ones((ONES_ROWS, tk), BF16)
    zw, zr = zero_ref[0], zero_ref[1]

    def kv_cols(hd):
        kv = hd // GROUP
        return slice(kv * HEAD_DIM, (kv + 1) * HEAD_DIM)

    def scores(c, hd):
        off = pl.multiple_of(c * tk, tk)
        kc = k_ref[0, pl.ds(off, tk), kv_cols(hd)]
        qT = qT_ref[0, 0, hd * HEAD_DIM:(hd + 1) * HEAD_DIM, :]
        s = jnp.dot(kc, qT, preferred_element_type=F32)
        s_slots[hd][zw] = s
        cmax_ref[hd] = jnp.max(s, axis=0, keepdims=True)

    def accumulate(c, hd):
        off = pl.multiple_of(c * tk, tk)
        m_old = m_ref[hd:hd + 1, :]
        m_new = jnp.maximum(m_old, cmax_ref[hd])
        m_ref[hd:hd + 1, :] = m_new
        p = jnp.exp2(s_slots[hd][zr] - m_new).astype(BF16)
        vc = jnp.concatenate([vT_ref[0, kv_cols(hd), pl.ds(off, tk)], ones_rows], axis=0)
        acc_ref[hd] = jnp.exp2(m_old - m_new) * acc_ref[hd] + jnp.dot(vc, p, preferred_element_type=F32)

    for hd in range(AHEAD):
        scores(0, hd)

    def step(c, carry):
        for hd in range(N_HEADS):
            w = hd + AHEAD
            scores(jnp.minimum(c + w // N_HEADS, n_chunks - 1), w % N_HEADS)
            accumulate(c, hd)
        return carry

    lax.fori_loop(0, n_chunks, step, 0, unroll=16)

    for hd in range(N_HEADS):
        o = acc_ref[hd, :HEAD_DIM, :] * (1.0 / acc_ref[hd, HEAD_DIM:HEAD_DIM + 1, :])
        o_ref[0, :, hd * HEAD_DIM:(hd + 1) * HEAD_DIM] = o.T.astype(BF16)


def _attn_call(qT, k, vT):
    b, n_qb, n_q, tq = qT.shape
    s, n_kv = k.shape[1:]
    tk = min(ATTN_K, s)
    kernel = functools.partial(_attn_kernel, n_chunks=s // tk, tk=tk)
    kv_buffers = pl.Buffered(2 if b > 1 else 1)
    return pl.pallas_call(
        kernel,
        grid=(b, n_qb),
        in_specs=[
            pl.BlockSpec(memory_space=pltpu.SMEM),
            pl.BlockSpec((1, 1, n_q, tq), lambda bi, i: (bi, i, 0, 0)),
            pl.BlockSpec((1, s, n_kv), lambda bi, i: (bi, 0, 0), pipeline_mode=kv_buffers),
            pl.BlockSpec((1, n_kv, s), lambda bi, i: (bi, 0, 0), pipeline_mode=kv_buffers),
        ],
        out_specs=pl.BlockSpec((1, tq, n_q), lambda bi, i: (bi, i, 0)),
        out_shape=jax.ShapeDtypeStruct((b, s, n_q), BF16),
        scratch_shapes=[
            pltpu.VMEM((N_HEADS, tq), F32),
            pltpu.VMEM((N_HEADS, HEAD_DIM + ONES_ROWS, tq), F32),
            pltpu.VMEM((N_HEADS, 1, tq), F32),
        ] + [pltpu.VMEM((1, tk, tq), F32)] * N_HEADS,
        compiler_params=_params(2, True, all_sequential=True),
        name="attn",
    )(jnp.zeros((2,), jnp.int32), qT, k, vT)


def _reset_carry_at_sequence_start(carry_ref):
    @pl.when(pl.program_id(1) == 0)
    def _():
        carry_ref[...] = jnp.zeros(carry_ref.shape, F32)


def _lru_inputs(x_ref, xp_ref, xn_ref, g_ref, wxl_ref, cw_ref, cb_ref, *, t):
    blk = pl.program_id(1)
    nb = pl.num_programs(1)
    d_lru = wxl_ref.shape[1]
    groups = t // SUBLANES
    xp = xp_ref[0] * jnp.where(blk > 0, 1.0, 0.0)
    xn = xn_ref[0] * jnp.where(blk < nb - 1, 1.0, 0.0)
    xe = jnp.concatenate([xp, x_ref[0], xn], axis=0)
    he = (xe * _rms_scale(xe) * g_ref[...]).astype(BF16)
    xl = jnp.dot(he, wxl_ref[...], preferred_element_type=F32)
    x3 = xl.reshape(groups + 2, SUBLANES, d_lru)
    sub = lax.broadcasted_iota(jnp.int32, (1, SUBLANES, d_lru), 1)

    def delayed(k):
        r = pltpu.roll(x3, k, axis=1)
        return jnp.where(sub >= k, r[1:groups + 1], r[0:groups])

    ahead = pltpu.roll(x3, SUBLANES - 1, axis=1)
    ahead = jnp.where(sub < SUBLANES - 1, ahead[1:groups + 1], ahead[2:groups + 2])
    cw = cw_ref[...]
    xc = (cb_ref[...] + cw[0:1] * delayed(2) + cw[1:2] * delayed(1) + cw[2:3] * x3[1:groups + 1]
          + cw[3:4] * ahead)
    return xc.reshape(t, d_lru)


def _lru_gates(xc, wg_ref, bg_ref):
    return [
        jnp.dot(xc[:, n * LRU_BLOCK:(n + 1) * LRU_BLOCK].astype(BF16), wg_ref[n], preferred_element_type=F32)
        + bg_ref[n]
        for n in range(N_LRU_BLOCKS)
    ]


def _lru_scan(xc, gate_pre, cneg_ref, a_scr, u_scr, carry_ref, write_rows, *, reverse, t):
    groups = t // SUBLANES
    sub = lax.broadcasted_iota(jnp.int32, (1, SUBLANES, LRU_BLOCK), 1)
    for n in range(N_LRU_BLOCKS):
        cols = slice(n * LRU_BLOCK, (n + 1) * LRU_BLOCK)
        xcn = xc[:, cols]
        gz = gate_pre[n]
        r = jax.nn.sigmoid(gz[:, :LRU_BLOCK])
        ig = jax.nn.sigmoid(gz[:, LRU_BLOCK:])
        a = jnp.exp2(cneg_ref[:, cols] * r)
        y = (1.0 - a) * (1.0 + a)
        u = (y * lax.rsqrt(jnp.maximum(y, TINY))) * (ig * xcn)
        a3 = a.reshape(groups, SUBLANES, LRU_BLOCK)
        u3 = u.reshape(groups, SUBLANES, LRU_BLOCK)
        for dist in (1, 2, 4):
            if reverse:
                shift, ok = SUBLANES - dist, sub < SUBLANES - dist
            else:
                shift, ok = dist, sub >= dist
            a_nb = pltpu.roll(a3, shift, axis=1)
            u_nb = pltpu.roll(u3, shift, axis=1)
            u3 = jnp.where(ok, a3 * u_nb + u3, u3)
            a3 = jnp.where(ok, a3 * a_nb, a3)
        a_scr[:, :, cols] = a3
        u_scr[:, :, cols] = u3

    edge = 0 if reverse else SUBLANES - 1
    for n in range(N_LRU_BLOCKS):
        cols = slice(n * LRU_BLOCK, (n + 1) * LRU_BLOCK)
        hrow = carry_ref[:, cols]
        for j in range(groups):
            jj = groups - 1 - j if reverse else j
            rows = u_scr[jj, :, cols] + a_scr[jj, :, cols] * hrow
            write_rows(jj, cols, rows)
            hrow = jnp.broadcast_to(rows[edge:edge + 1, :], (SUBLANES, LRU_BLOCK))
        carry_ref[:, cols] = hrow


def _lru_f_kernel(x_ref, xp_ref, xn_ref, g_ref, wxl_ref, cw_ref, cb_ref, wg_ref, bg_ref, cneg_ref,
                  hf_ref, xc_ref, a_scr, u_scr, carry_ref, *, t):
    def write_rows(j, cols, rows):
        hf_ref[0, j, :, cols] = rows

    _reset_carry_at_sequence_start(carry_ref)
    xc = _lru_inputs(x_ref, xp_ref, xn_ref, g_ref, wxl_ref, cw_ref, cb_ref, t=t)
    xc_ref[0] = xc
    gate_pre = _lru_gates(xc, wg_ref, bg_ref)
    _lru_scan(xc, gate_pre, cneg_ref, a_scr, u_scr, carry_ref, write_rows, reverse=False, t=t)


def _merge_kernel(x_ref, g_ref, xc_ref, wg_ref, bg_ref, cneg_ref,
                  hf_ref, att_ref, wgl_ref, wga_ref, wm_ref, bm_ref, wb0_ref, wb1_ref, wo_ref, gfin_ref,
                  y_ref, a_scr, u_scr, carry_ref, hb_scr, *, t, final_norm):
    def write_rows(j, cols, rows):
        hb_scr[j, :, cols] = rows

    _reset_carry_at_sequence_start(carry_ref)
    d = x_ref.shape[2]
    x = x_ref[0]
    h = (x * _rms_scale(x) * g_ref[...]).astype(BF16)
    xc = xc_ref[0]
    gate_pre = _lru_gates(xc, wg_ref, bg_ref)
    _lru_scan(xc, gate_pre, cneg_ref, a_scr, u_scr, carry_ref, write_rows, reverse=True, t=t)
    g_l = jnp.dot(h, wgl_ref[...], preferred_element_type=F32)
    lsum = (hf_ref[0] + hb_scr[...]).reshape(t, xc.shape[1])
    l_out = (lsum * (g_l * jax.nn.sigmoid(g_l))).astype(BF16)
    g_a = jnp.dot(h, wga_ref[...], preferred_element_type=F32)
    a_out = (att_ref[0].astype(F32) * (g_a * jax.nn.sigmoid(g_a))).astype(BF16)
    a_proj = jnp.dot(a_out, wb0_ref[...], preferred_element_type=F32)
    l_proj = jnp.dot(l_out, wb1_ref[...], preferred_element_type=F32)
    gates = jax.nn.sigmoid(jnp.dot(h, wm_ref[...], preferred_element_type=F32) + bm_ref[...])
    merged = gates[:, :d] * a_proj + gates[:, d:] * l_proj
    y = x + jnp.dot(merged.astype(BF16), wo_ref[...], preferred_element_type=F32)
    if final_norm:
        y = y * _rms_scale(y) * gfin_ref[...]
    y_ref[0] = y


def _lru_specs(s, t, d, d_lru, reverse):
    nb = s // t
    groups = t // SUBLANES
    last_group = s // SUBLANES - 1

    def blk(i):
        return nb - 1 - i if reverse else i

    x_spec = pl.BlockSpec((1, t, d), lambda bi, i: (bi, blk(i), 0))
    xp_spec = pl.BlockSpec((1, HALO, d), lambda bi, i: (bi, jnp.maximum(blk(i) * groups - 1, 0), 0))
    xn_spec = pl.BlockSpec((1, HALO, d), lambda bi, i: (bi, jnp.minimum((blk(i) + 1) * groups, last_group), 0))
    hf_spec = pl.BlockSpec((1, groups, SUBLANES, d_lru), lambda bi, i: (bi, blk(i), 0, 0))
    scratch = [
        pltpu.VMEM((groups, SUBLANES, d_lru), F32),
        pltpu.VMEM((groups, SUBLANES, d_lru), F32),
        pltpu.VMEM((SUBLANES, d_lru), F32),
    ]
    return x_spec, xp_spec, xn_spec, hf_spec, scratch, blk


def _lru_f_call(x, g_in, w_xl, conv_w, conv_b, w_gate, b_gate, cneg):
    b, s, d = x.shape
    d_lru = w_xl.shape[1]
    t = min(LRU_ROWS, s)
    x_spec, xp_spec, xn_spec, hf_spec, scratch, _ = _lru_specs(s, t, d, d_lru, False)
    return pl.pallas_call(
        functools.partial(_lru_f_kernel, t=t),
        grid=(b, s // t),
        in_specs=[x_spec, xp_spec, xn_spec, _const_spec((1, d)), _const_spec(w_xl.shape),
                  _const_spec(conv_w.shape), _const_spec(conv_b.shape), _const_spec(w_gate.shape),
                  _const_spec(b_gate.shape), _const_spec(cneg.shape)],
        out_specs=[hf_spec, pl.BlockSpec((1, t, d_lru), lambda bi, i: (bi, i, 0))],
        out_shape=[jax.ShapeDtypeStruct((b, s // SUBLANES, SUBLANES, d_lru), F32),
                   jax.ShapeDtypeStruct((b, s, d_lru), F32)],
        scratch_shapes=scratch,
        compiler_params=_params(2, True),
        name="lru_f",
    )(x, x, x, g_in, w_xl, conv_w, conv_b, w_gate, b_gate, cneg)


def _merge_call(x, g_in, xc, w_gate, b_gate, cneg, hf, att, w_gl, w_ga, w_m, b_m, w_b0, w_b1, w_o, g_fin, final_norm):
    b, s, d = x.shape
    d_lru = xc.shape[2]
    t = min(LRU_ROWS, s)
    _, _, _, hf_spec, scratch, blk = _lru_specs(s, t, d, d_lru, True)
    tok = lambda width: pl.BlockSpec((1, t, width), lambda bi, i: (bi, blk(i), 0))
    consts = [w_gate, b_gate, cneg]
    tail = [w_gl, w_ga, w_m, b_m, w_b0, w_b1, w_o, g_fin]
    return pl.pallas_call(
        functools.partial(_merge_kernel, t=t, final_norm=final_norm),
        grid=(b, s // t),
        in_specs=[tok(d), _const_spec(g_in.shape), tok(d_lru)] + [_const_spec(c.shape) for c in consts]
        + [hf_spec, tok(att.shape[2])] + [_const_spec(c.shape) for c in tail],
        out_specs=tok(d),
        out_shape=jax.ShapeDtypeStruct((b, s, d), F32),
        scratch_shapes=scratch + [pltpu.VMEM((t // SUBLANES, SUBLANES, d_lru), F32)],
        compiler_params=_params(2, True),
        name="merge",
    )(x, g_in, xc, *consts, hf, att, *tail)


def _rope_tables(s, gain, scale):
    rows_n = s // GRID_W
    rows = jnp.repeat(jnp.arange(rows_n, dtype=F32), GRID_W)
    cols = jnp.tile(jnp.arange(GRID_W, dtype=F32), rows_n)
    n_pair_axis = HEAD_DIM // 4
    inv_freq = ROPE_THETA ** (-jnp.arange(n_pair_axis, dtype=F32) / n_pair_axis)
    ang = jnp.concatenate([rows[:, None] * inv_freq, cols[:, None] * inv_freq], axis=-1)
    cos = jnp.cos(ang).T
    sin = jnp.sin(ang).T
    c = jnp.concatenate([cos, cos], axis=0)
    sn = jnp.concatenate([-sin, sin], axis=0)
    gain = gain.astype(F32)
    return (gain[:, None] * c) * scale, (jnp.roll(gain, HALF)[:, None] * sn) * scale


def _layer_params(w_in, q_norm, k_norm, w_rgate, b_rgate, w_igate, b_igate, lam, w_branch, w_out, d):
    d_attn = N_HEADS * HEAD_DIM
    d_kv = N_KV_HEADS * HEAD_DIM
    d_lru = N_LRU_BLOCKS * LRU_BLOCK
    splits = [d_attn, d_kv, d_kv, d_attn, d_lru, d_lru]
    offs = [0]
    for w in splits:
        offs.append(offs[-1] + w)
    w_q, w_k, w_v, w_ga, w_xl, w_gl = (w_in[:, offs[j]:offs[j + 1]] for j in range(6))
    w_m = w_in[:, offs[6]:]
    perm = jnp.concatenate([jnp.arange(0, HEAD_DIM, 2), jnp.arange(1, HEAD_DIM, 2)])
    permute = lambda w, nh: w.reshape(d, nh, HEAD_DIM)[:, :, perm].reshape(d, nh * HEAD_DIM)
    w_qkvT = jnp.concatenate([permute(w_q, N_HEADS), permute(w_k, N_KV_HEADS), w_v], axis=1).T.astype(BF16)
    w_gate = jnp.concatenate([w_rgate, w_igate], axis=-1).astype(BF16)
    b_gate = jnp.concatenate([b_rgate.reshape(2, N_LRU_BLOCKS, 1, LRU_BLOCK),
                              b_igate.reshape(2, N_LRU_BLOCKS, 1, LRU_BLOCK)], axis=-1).astype(F32)
    cneg = (-RG_C * math.log2(math.e) * jax.nn.softplus(-lam.astype(F32))).reshape(2, 1, d_lru)
    return dict(
        w_qkvT=w_qkvT, gq=q_norm[perm], gk=k_norm[perm],
        w_ga=w_ga.astype(BF16), w_xl=w_xl.astype(BF16), w_gl=w_gl.astype(BF16), w_m=w_m.astype(BF16),
        w_gate=w_gate, b_gate=b_gate, cneg=cneg,
        w_b0=w_branch[0].astype(BF16), w_b1=w_branch[1].astype(BF16), w_o=w_out.astype(BF16),
    )


def _layer(x, p, g_in, conv_w, conv_b, b_m, g_fin, final_norm):
    s = x.shape[1]
    q_scale = math.log2(math.e) / math.sqrt(HEAD_DIM)
    cq, sq = _rope_tables(s, p["gq"], q_scale)
    ck, sk = _rope_tables(s, p["gk"], 1.0)
    qT, k, vT = _qkv_call(x, g_in, p["w_qkvT"], cq, sq, ck, sk)
    att = _attn_call(qT, k, vT)
    lru = (g_in, p["w_xl"], conv_w, conv_b)
    hf, xc = _lru_f_call(x, *lru, p["w_gate"][0], p["b_gate"][0], p["cneg"][0])
    return _merge_call(x, g_in, xc, p["w_gate"][1], p["b_gate"][1], p["cneg"][1], hf, att,
                       p["w_gl"], p["w_ga"], p["w_m"], b_m, p["w_b0"], p["w_b1"], p["w_o"], g_fin, final_norm)


def kernel(x_prompt, x_sample, norm_in, w_in, b_merge, q_norm, k_norm, conv_w, conv_b, w_rgate, b_rgate,
           w_igate, b_igate, lam, w_branch, w_out, norm_final):
    depth, d = norm_in.shape
    layers = [
        _layer_params(w_in[l], q_norm[l], k_norm[l], w_rgate[l], b_rgate[l], w_igate[l], b_igate[l], lam[l],
                      w_branch[l], w_out[l], d)
        for l in range(depth)
    ]
    g_fin = norm_final.reshape(1, d).astype(F32)
    outs = []
    for x in (x_prompt, x_sample):
        for l in range(depth):
            x = _layer(x, layers[l], norm_in[l].reshape(1, d).astype(F32), conv_w[l].astype(F32),
                       conv_b[l].reshape(1, -1).astype(F32), b_merge[l].reshape(1, -1).astype(F32), g_fin,
                       l == depth - 1)
        outs.append(x)
    return tuple(outs)
```
